```python
import jax, jax.numpy as jnp
from jax import lax
import numpy as np

D_MODEL = 1024
BATCH = 8
SEQ = 2048
DEPTH = 2

GRID_W = 64
CTX_LEN = 256
D_FOURIER = 256
FOURIER_GROUPS = 4
D_CONV = 256
CONV_WIDTH = 31
N_NA_HEADS = 8
NA_HEAD_DIM = 64
D_NA = N_NA_HEADS * NA_HEAD_DIM
WIN_ROWS = 8
WIN_COLS = 16
QKV_START = D_FOURIER + 2 * D_CONV
KV_START = QKV_START + D_NA
D_IN = QKV_START + 3 * D_NA
D_MIX = D_FOURIER + D_CONV + D_NA
D_FF = 4 * D_MODEL
N_MOD = 6
RMS_EPS = 1e-6
LN_EPS = 1e-5

kernel_name = "hymba_style_fourier_conformer_natten_dit"


def rms_norm(x, g):
    xf = x.astype(jnp.float32)
    y = xf * lax.rsqrt(jnp.mean(xf * xf, axis=-1, keepdims=True) + RMS_EPS)
    return (y * g.astype(jnp.float32)).astype(x.dtype)


def layer_norm(x, g, b):
    xf = x.astype(jnp.float32)
    mu = jnp.mean(xf, axis=-1, keepdims=True)
    var = jnp.mean(jnp.square(xf - mu), axis=-1, keepdims=True)
    y = (xf - mu) * lax.rsqrt(var + LN_EPS)
    return (y * g.astype(jnp.float32) + b.astype(jnp.float32)).astype(x.dtype)


def modulate(x, shift, scale):
    return x * (1 + scale) + shift


def split_heads(u):
    b, l, _ = u.shape
    qkv = u[..., QKV_START:].reshape(b, l, 3, N_NA_HEADS, NA_HEAD_DIM)
    return u[..., :D_FOURIER], u[..., D_FOURIER:QKV_START], qkv[:, :, 0], qkv[:, :, 1], qkv[:, :, 2]


def fourier_mix(u, w):
    b, l, _ = u.shape
    ug = u.reshape(b, l, FOURIER_GROUPS, D_FOURIER // FOURIER_GROUPS).astype(jnp.float32)
    f = jnp.fft.fftn(ug, axes=(1, 3), norm="ortho").real
    return f.reshape(b, l, D_FOURIER).astype(u.dtype) @ w


def conv_module(u, dw_w, dw_b, ln_g, ln_b, pw_w, pw_b):
    a, gt = jnp.split(u, 2, axis=-1)
    v = a * jax.nn.sigmoid(gt)
    pad = CONV_WIDTH // 2
    v = lax.conv_general_dilated(v, dw_w[:, None, :].astype(v.dtype), window_strides=(1,), padding=[(pad, pad)],
                                 dimension_numbers=("NWC", "WIO", "NWC"), feature_group_count=D_CONV) + dw_b
    v = jax.nn.silu(layer_norm(v, ln_g, ln_b))
    return v @ pw_w + pw_b


def context_attention(q, k, v):
    s = jnp.einsum("bqhd,bkhd->bhqk", q, k).astype(jnp.float32) * (NA_HEAD_DIM ** -0.5)
    p = jax.nn.softmax(s, axis=-1).astype(v.dtype)
    o = jnp.einsum("bhqk,bkhd->bqhd", p, v)
    return o.reshape(q.shape[0], q.shape[1], D_NA)


def neighbourhood_attention(q, k, v, k_ctx, v_ctx, rpb):
    b, l, h, dh = q.shape
    rows = l // GRID_W
    kr = min(WIN_ROWS, rows)
    r = np.arange(rows)
    row_start = np.clip(r - kr // 2, 0, rows - kr)
    key_rows = row_start[:, None] + np.arange(kr)[None, :]
    col = np.arange(GRID_W)
    col_start = np.clip(col - WIN_COLS // 2, 0, GRID_W - WIN_COLS)
    col_valid = (col[None, :] >= col_start[:, None]) & (col[None, :] < col_start[:, None] + WIN_COLS)
    row_off = key_rows - r[:, None] + (WIN_ROWS - 1)
    col_off = np.clip(col[None, :] - col[:, None] + (WIN_COLS - 1), 0, 2 * WIN_COLS - 2)
    bias = rpb[:, row_off[:, :, None, None], col_off[None, None]].astype(jnp.float32)
    bias = jnp.where(jnp.asarray(col_valid)[None, None, None], bias, -jnp.inf)
    bias = bias.transpose(0, 1, 3, 2, 4).reshape(h, rows, GRID_W, kr * GRID_W)
    qg = q.reshape(b, rows, GRID_W, h, dh)
    kg = k.reshape(b, rows, GRID_W, h, dh)[:, key_rows].reshape(b, rows, kr * GRID_W, h, dh)
    vg = v.reshape(b, rows, GRID_W, h, dh)[:, key_rows].reshape(b, rows, kr * GRID_W, h, dh)
    scale = NA_HEAD_DIM ** -0.5
    s_loc = jnp.einsum("brqhd,brkhd->bhrqk", qg, kg).astype(jnp.float32) * scale + bias[None]
    s_ctx = jnp.einsum("brqhd,bchd->bhrqc", qg, k_ctx).astype(jnp.float32) * scale
    p = jax.nn.softmax(jnp.concatenate([s_loc, s_ctx], axis=-1), axis=-1).astype(v.dtype)
    n_loc = kr * GRID_W
    o = (jnp.einsum("bhrqk,brkhd->brqhd", p[..., :n_loc], vg)
         + jnp.einsum("bhrqc,bchd->brqhd", p[..., n_loc:], v_ctx))
    return o.reshape(b, l, D_NA)


def setup_inputs(seed: int = 0) -> dict:
    key = jax.random.key(seed)
    ks = jax.random.split(key, 24)

    def nrm(k, shape, scale):
        return jax.random.normal(k, shape, jnp.float32) * scale

    return {
        "x": nrm(ks[0], (BATCH, SEQ, D_MODEL), 1.0),
        "c": nrm(ks[1], (BATCH, D_MODEL), 1.0),
        "ctx": nrm(ks[2], (BATCH, CTX_LEN, D_MODEL), 1.0),
        "c_ctx": nrm(ks[3], (D_MODEL,), 1.0),
        "ada_w": nrm(ks[4], (DEPTH, D_MODEL, N_MOD * D_MODEL), D_MODEL ** -0.5),
        "ada_b": nrm(ks[5], (DEPTH, N_MOD * D_MODEL), 0.02),
        "norm1_g": 1.0 + nrm(ks[6], (DEPTH, D_MODEL), 0.05),
        "norm2_g": 1.0 + nrm(ks[7], (DEPTH, D_MODEL), 0.05),
        "w_in": nrm(ks[8], (DEPTH, D_MODEL, D_IN), D_MODEL ** -0.5),
        "w_fourier": nrm(ks[9], (DEPTH, D_FOURIER, D_FOURIER), D_FOURIER ** -0.5),
        "conv_dw_w": nrm(ks[10], (DEPTH, CONV_WIDTH, D_CONV), CONV_WIDTH ** -0.5),
        "conv_dw_b": nrm(ks[11], (DEPTH, D_CONV), 0.02),
        "conv_norm_g": 1.0 + nrm(ks[12], (DEPTH, D_CONV), 0.05),
        "conv_norm_b": nrm(ks[13], (DEPTH, D_CONV), 0.02),
        "conv_pw_w": nrm(ks[14], (DEPTH, D_CONV, D_CONV), D_CONV ** -0.5),
        "conv_pw_b": nrm(ks[15], (DEPTH, D_CONV), 0.02),
        "na_rpb": nrm(ks[16], (DEPTH, N_NA_HEADS, 2 * WIN_ROWS - 1, 2 * WIN_COLS - 1), 0.1),
        "w_out": nrm(ks[17], (DEPTH, D_MIX, D_MODEL), D_MIX ** -0.5),
        "mlp_w1": nrm(ks[18], (DEPTH, D_MODEL, D_FF), D_MODEL ** -0.5),
        "mlp_w2": nrm(ks[19], (DEPTH, D_FF, D_MODEL), D_FF ** -0.5),
        "final_norm_g": 1.0 + nrm(ks[20], (D_MODEL,), 0.05),
    }


def reference(x, c, ctx, c_ctx, ada_w, ada_b, norm1_g, norm2_g, w_in, w_fourier, conv_dw_w, conv_dw_b,
              conv_norm_g, conv_norm_b, conv_pw_w, conv_pw_b, na_rpb, w_out, mlp_w1, mlp_w2, final_norm_g):
    h_lat, h_ctx = x, ctx
    b = x.shape[0]
    n_ctx = ctx.shape[1]
    for i in range(DEPTH):
        last = i == DEPTH - 1
        conv_p = (conv_dw_w[i], conv_dw_b[i], conv_norm_g[i], conv_norm_b[i], conv_pw_w[i], conv_pw_b[i])
        mod = (jax.nn.silu(c) @ ada_w[i] + ada_b[i])[:, None, :]
        sh1, sc1, g1, sh2, sc2, g2 = jnp.split(mod, N_MOD, axis=-1)
        n_ctx_mod = (2 if last else N_MOD) * D_MODEL
        mod_c = (jax.nn.silu(c_ctx) @ ada_w[i][:, :n_ctx_mod] + ada_b[i][:n_ctx_mod])[None, None, :]
        mods_c = jnp.split(mod_c, n_ctx_mod // D_MODEL, axis=-1)

        hn_ctx = modulate(rms_norm(h_ctx, norm1_g[i]), mods_c[0], mods_c[1])
        if last:
            kv = (hn_ctx @ w_in[i][:, KV_START:]).reshape(b, n_ctx, 2, N_NA_HEADS, NA_HEAD_DIM)
            k_ctx, v_ctx = kv[:, :, 0], kv[:, :, 1]
        else:
            uf_c, uc_c, q_ctx, k_ctx, v_ctx = split_heads(hn_ctx @ w_in[i])
            mix_ctx = jnp.concatenate([fourier_mix(uf_c, w_fourier[i]), conv_module(uc_c, *conv_p),
                                       context_attention(q_ctx, k_ctx, v_ctx)], axis=-1)

        hn_lat = modulate(rms_norm(h_lat, norm1_g[i]), sh1, sc1)
        uf, uc, q, k, v = split_heads(hn_lat @ w_in[i])
        mix_lat = jnp.concatenate([fourier_mix(uf, w_fourier[i]), conv_module(uc, *conv_p),
                                   neighbourhood_attention(q, k, v, k_ctx, v_ctx, na_rpb[i])], axis=-1)
        h_lat = h_lat + g1 * (mix_lat @ w_out[i])

        hn = modulate(rms_norm(h_lat, norm2_g[i]), sh2, sc2)
        h_lat = h_lat + g2 * (jnp.square(jax.nn.relu(hn @ mlp_w1[i])) @ mlp_w2[i])

        if not last:
            h_ctx = h_ctx + mods_c[2] * (mix_ctx @ w_out[i])
            hn_c2 = modulate(rms_norm(h_ctx, norm2_g[i]), mods_c[3], mods_c[4])
            h_ctx = h_ctx + mods_c[5] * (jnp.square(jax.nn.relu(hn_c2 @ mlp_w1[i])) @ mlp_w2[i])
    return rms_norm(h_lat, final_norm_g)
```

```python
import functools

import numpy as np
import jax
import jax.numpy as jnp
from jax import lax
from jax.experimental import pallas as pl
from jax.experimental.pallas import tpu as pltpu

D_MODEL = 1024
GRID_W = 64
D_FOURIER = 256
FOURIER_GROUPS = 4
D_CONV = 256
CONV_WIDTH = 31
N_NA_HEADS = 8
NA_HEAD_DIM = 64
D_NA = N_NA_HEADS * NA_HEAD_DIM
WIN_ROWS = 8
WIN_COLS = 16
QKV_START = D_FOURIER + 2 * D_CONV
KV_START = QKV_START + D_NA
D_MIX = D_FOURIER + D_CONV + D_NA
N_MOD = 6
RMS_EPS = 1e-6
LN_EPS = 1e-5

LANES = 128
HEAD_PAIRS = N_NA_HEADS * NA_HEAD_DIM // LANES
MOD_ROWS = 16
VMEM_LIMIT = 56 * 1024 * 1024

F32 = jnp.float32
BF16 = jnp.bfloat16


def _params(*sem):
    return pltpu.CompilerParams(dimension_semantics=sem, vmem_limit_bytes=VMEM_LIMIT)


def _sigmoid(x):
    return 1.0 / (1.0 + jnp.exp(-x))


def _rms_mod(x, g, shift, scale):
    y = x * lax.rsqrt(jnp.mean(x * x, axis=-1, keepdims=True) + RMS_EPS) * g
    return y * (1.0 + scale) + shift


def _bdot(a, b):
    return jnp.dot(a, b, preferred_element_type=F32)


def _adaln_kernel(c_ref, w_ref, b_ref, o_ref):
    cc = c_ref[...]
    a = cc * _sigmoid(cc)
    o_ref[0] = jnp.dot(a, w_ref[0], preferred_element_type=F32, precision=lax.Precision.HIGHEST) + b_ref[0]


def _adaln(cc, ada_w, ada_b):
    depth, d, n = ada_w.shape
    tn = 1024
    return pl.pallas_call(
        _adaln_kernel,
        grid=(depth, n // tn),
        in_specs=[pl.BlockSpec((MOD_ROWS, d), lambda l, j: (0, 0)),
                  pl.BlockSpec((1, d, tn), lambda l, j: (l, 0, j)),
                  pl.BlockSpec((1, 1, tn), lambda l, j: (l, 0, j))],
        out_specs=pl.BlockSpec((1, MOD_ROWS, tn), lambda l, j: (l, 0, j)),
        out_shape=jax.ShapeDtypeStruct((depth, MOD_ROWS, n), F32),
        compiler_params=_params("parallel", "parallel"),
        name="adaln",
    )(cc, ada_w, ada_b.reshape(depth, 1, n))


def _fold_kernel(cc_ref, sc_ref, w_ref, o_ref):
    w = w_ref[0]
    hi = lax.Precision.HIGHEST
    o_ref[0, :, :D_FOURIER] = jnp.dot(cc_ref[...], w, preferred_element_type=F32, precision=hi).astype(BF16)
    o_ref[0, :, D_FOURIER:] = jnp.dot(sc_ref[...], w, preferred_element_type=F32, precision=hi).astype(BF16)


def _fold_fourier(w_fourier, seq):
    depth = w_fourier.shape[0]
    gs = D_FOURIER // FOURIER_GROUPS
    idx = np.arange(D_FOURIER)
    same = (idx[:, None] // gs) == (idx[None, :] // gs)
    ang = 2.0 * np.pi * (((idx[:, None] % gs) * (idx[None, :] % gs)) % gs) / gs
    scale = 1.0 / np.sqrt(seq * gs)
    cc = jnp.asarray(np.where(same, np.cos(ang), 0.0) * scale, F32)
    sc = jnp.asarray(np.where(same, np.sin(ang), 0.0) * scale, F32)
    full = pl.BlockSpec((D_FOURIER, D_FOURIER), lambda l: (0, 0))
    return pl.pallas_call(
        _fold_kernel,
        grid=(depth,),
        in_specs=[full, full, pl.BlockSpec((1, D_FOURIER, D_FOURIER), lambda l: (l, 0, 0))],
        out_specs=pl.BlockSpec((1, D_FOURIER, 2 * D_FOURIER), lambda l: (l, 0, 0)),
        out_shape=jax.ShapeDtypeStruct((depth, D_FOURIER, 2 * D_FOURIER), BF16),
        compiler_params=_params("parallel"),
        name="fold_fourier",
    )(cc, sc, w_fourier)


def _dft_mats(seq):
    k = np.arange(seq)
    ang = 2.0 * np.pi * ((k[:, None] * k[None, :]) % seq) / seq
    return jnp.asarray(np.cos(ang), F32).astype(BF16), jnp.asarray(np.sin(ang), F32).astype(BF16)


def _inproj_kernel(x_ref, g_ref, mod_ref, w_ref, ab_ref, zf_ref, uc_ref, qkv_ref):
    hn = _rms_mod(x_ref[...], g_ref[...], mod_ref[0, 0:1, :], mod_ref[0, 1:2, :]).astype(BF16)
    uf = _bdot(hn, w_ref[:, :D_FOURIER])
    zf_ref[...] = _bdot(uf.astype(BF16), ab_ref[...]).astype(BF16)
    uc_ref[...] = _bdot(hn, w_ref[:, D_FOURIER:QKV_START])
    qkv_ref[:, :D_NA] = (_bdot(hn, w_ref[:, QKV_START:KV_START]) * (NA_HEAD_DIM ** -0.5)).astype(BF16)
    qkv_ref[:, D_NA:] = _bdot(hn, w_ref[:, KV_START:]).astype(BF16)


def _inproj(h, g, mods, w_in, ab, *, seq, tm, ctx):
    t, d = h.shape
    nb = t // seq
    per = seq // tm
    row = (lambda i: nb) if ctx else (lambda i: i // per)
    d_in = w_in.shape[1]
    return pl.pallas_call(
        _inproj_kernel,
        grid=(t // tm,),
        in_specs=[pl.BlockSpec((tm, d), lambda i: (i, 0)),
                  pl.BlockSpec((1, d), lambda i: (0, 0)),
                  pl.BlockSpec((1, N_MOD, d), lambda i: (row(i), 0, 0)),
                  pl.BlockSpec((d, d_in), lambda i: (0, 0)),
                  pl.BlockSpec((D_FOURIER, 2 * D_FOURIER), lambda i: (0, 0))],
        out_specs=[pl.BlockSpec((tm, 2 * D_FOURIER), lambda i: (i % per, i // per)),
                   pl.BlockSpec((tm, 2 * D_CONV), lambda i: (i, 0)),
                   pl.BlockSpec((tm, 3 * D_NA), lambda i: (i, 0))],
        out_shape=[jax.ShapeDtypeStruct((seq, nb * 2 * D_FOURIER), BF16),
                   jax.ShapeDtypeStruct((t, 2 * D_CONV), F32),
                   jax.ShapeDtypeStruct((t, 3 * D_NA), BF16)],
        compiler_params=_params("parallel"),
        name="inproj_ctx" if ctx else "inproj_lat",
    )(h, g, mods, w_in, ab)


def _kvproj_kernel(x_ref, g_ref, mod_ref, w_ref, kv_ref):
    hn = _rms_mod(x_ref[...], g_ref[...], mod_ref[0, 0:1, :], mod_ref[0, 1:2, :]).astype(BF16)
    kv_ref[...] = _bdot(hn, w_ref[...]).astype(BF16)


def _kvproj(h, g, mods, w_kv, *, nb, tm):
    t, d = h.shape
    n = w_kv.shape[1]
    return pl.pallas_call(
        _kvproj_kernel,
        grid=(t // tm,),
        in_specs=[pl.BlockSpec((tm, d), lambda i: (i, 0)),
                  pl.BlockSpec((1, d), lambda i: (0, 0)),
                  pl.BlockSpec((1, N_MOD, d), lambda i: (nb, 0, 0)),
                  pl.BlockSpec((d, n), lambda i: (0, 0))],
        out_specs=pl.BlockSpec((tm, n), lambda i: (i, 0)),
        out_shape=jax.ShapeDtypeStruct((t, n), BF16),
        compiler_params=_params("parallel"),
        name="kvproj_ctx",
    )(h, g, mods, w_kv)


def _fourier_kernel(c_ref, s_ref, z_ref, y_ref):
    y = _bdot(c_ref[...], z_ref[:, :D_FOURIER]) - _bdot(s_ref[...], z_ref[:, D_FOURIER:])
    y_ref[...] = y.astype(BF16)


def _fourier(cmat, smat, zf, *, tm):
    seq = cmat.shape[0]
    nb = zf.shape[1] // (2 * D_FOURIER)
    return pl.pallas_call(
        _fourier_kernel,
        grid=(seq // tm, nb),
        in_specs=[pl.BlockSpec((tm, seq), lambda i, b: (i, 0)),
                  pl.BlockSpec((tm, seq), lambda i, b: (i, 0)),
                  pl.BlockSpec((seq, 2 * D_FOURIER), lambda i, b: (0, b))],
        out_specs=pl.BlockSpec((tm, D_FOURIER), lambda i, b: (i, b)),
        out_shape=jax.ShapeDtypeStruct((seq, nb * D_FOURIER), BF16),
        compiler_params=_params("parallel", "parallel"),
        name="fourier_pos",
    )(cmat, smat, zf)


CONV_PAD = 16
CONV_CHUNK = 128


def _conv_kernel(u_ref, dww_ref, dwb_ref, lng_ref, lnb_ref, pww_ref, pwb_ref, o_ref, vpad_ref):
    seq = u_ref.shape[0]
    zeros = jnp.zeros((CONV_PAD, D_CONV), F32)
    vpad_ref[0:CONV_PAD, :] = zeros
    vpad_ref[CONV_PAD + seq:2 * CONV_PAD + seq, :] = zeros
    vpad_ref[CONV_PAD:CONV_PAD + seq, :] = u_ref[:, :D_CONV] * _sigmoid(u_ref[:, D_CONV:])
    first = CONV_PAD - CONV_WIDTH // 2

    def chunk(ci, carry):
        base = pl.multiple_of(ci * CONV_CHUNK, CONV_CHUNK)
        acc = jnp.zeros((CONV_CHUNK, D_CONV), F32) + dwb_ref[...]
        win = vpad_ref[pl.ds(base, CONV_CHUNK + 2 * CONV_PAD), :]
        for t in range(CONV_WIDTH):
            acc = acc + win[first + t:first + t + CONV_CHUNK, :] * dww_ref[t:t + 1, :]
        mu = jnp.mean(acc, axis=-1, keepdims=True)
        cen = acc - mu
        var = jnp.mean(cen * cen, axis=-1, keepdims=True)
        y = cen * lax.rsqrt(var + LN_EPS) * lng_ref[...] + lnb_ref[...]
        y = y * _sigmoid(y)
        o_ref[pl.ds(base, CONV_CHUNK), :] = (_bdot(y.astype(BF16), pww_ref[...]) + pwb_ref[...]).astype(BF16)
        return carry

    lax.fori_loop(0, seq // CONV_CHUNK, chunk, 0)


def _conv(uc, dww, dwb, lng, lnb, pww, pwb, *, seq):
    t = uc.shape[0]
    vec = pl.BlockSpec((1, D_CONV), lambda b: (0, 0))
    return pl.pallas_call(
        _conv_kernel,
        grid=(t // seq,),
        in_specs=[pl.BlockSpec((seq, 2 * D_CONV), lambda b: (b, 0)),
                  pl.BlockSpec((CONV_WIDTH, D_CONV), lambda b: (0, 0)),
                  vec, vec, vec,
                  pl.BlockSpec((D_CONV, D_CONV), lambda b: (0, 0)),
                  vec],
        out_specs=pl.BlockSpec((seq, D_CONV), lambda b: (b, 0)),
        out_shape=jax.ShapeDtypeStruct((t, D_CONV), BF16),
        scratch_shapes=[pltpu.VMEM((seq + 2 * CONV_PAD, D_CONV), F32)],
        compiler_params=_params("parallel"),
        name="conv_module",
    )(uc, dww, dwb, lng, lnb, pww, pwb)


def _head_mask():
    return lax.broadcasted_iota(jnp.int32, (1, LANES), 1) < NA_HEAD_DIM


def _na_kernel(q_ref, k_ref, v_ref, kc_ref, vc_ref, bias_ref, o_ref):
    rows = q_ref.shape[0] // GRID_W
    n_loc = WIN_ROWS * GRID_W
    first = _head_mask()
    kc = kc_ref[...]
    vc = vc_ref[...]
    dn = (((1,), (1,)), ((), ()))

    def row(r, carry):
        rs = jnp.clip(r - WIN_ROWS // 2, 0, rows - WIN_ROWS)
        pat = rs - r + (WIN_ROWS - 1)
        q = q_ref[pl.ds(pl.multiple_of(r * GRID_W, GRID_W), GRID_W), :]
        kb = k_ref[pl.ds(pl.multiple_of(rs * GRID_W, GRID_W), n_loc), :]
        vb = v_ref[pl.ds(pl.multiple_of(rs * GRID_W, GRID_W), n_loc), :]
        outs = []
        for a in range(2):
            qa = jnp.where(first if a == 0 else ~first, q, jnp.zeros_like(q))
            s_loc = lax.dot_general(qa, kb, dn, preferred_element_type=F32) + bias_ref[0, a, pat]
            s_ctx = lax.dot_general(qa, kc, dn, preferred_element_type=F32)
            m = jnp.maximum(jnp.max(s_loc, axis=-1, keepdims=True), jnp.max(s_ctx, axis=-1, keepdims=True))
            p_loc = jnp.exp(s_loc - m)
            p_ctx = jnp.exp(s_ctx - m)
            den = jnp.sum(p_loc, axis=-1, keepdims=True) + jnp.sum(p_ctx, axis=-1, keepdims=True)
            o = _bdot(p_loc.astype(BF16), vb) + _bdot(p_ctx.astype(BF16), vc)
            outs.append(o / den)
        o_ref[pl.ds(pl.multiple_of(r * GRID_W, GRID_W), GRID_W), :] = jnp.where(first, outs[0], outs[1]).astype(BF16)
        return carry

    lax.fori_loop(0, rows, row, 0)


def _na_bias(rpb):
    col = np.arange(GRID_W)
    col_start = np.clip(col - WIN_COLS // 2, 0, GRID_W - WIN_COLS)
    col_valid = (col[None, :] >= col_start[:, None]) & (col[None, :] < col_start[:, None] + WIN_COLS)
    col_off = np.clip(col[None, :] - col[:, None] + (WIN_COLS - 1), 0, 2 * WIN_COLS - 2)
    row_off = np.arange(WIN_ROWS)[:, None] + np.arange(WIN_ROWS)[None, :]
    b = rpb.astype(F32)[:, row_off[:, :, None, None], col_off[None, None]]
    b = jnp.where(jnp.asarray(col_valid)[None, None, None], b, -jnp.inf)
    b = b.transpose(0, 1, 3, 2, 4).reshape(N_NA_HEADS, WIN_ROWS, GRID_W, WIN_ROWS * GRID_W)
    return b.reshape(HEAD_PAIRS, 2, WIN_ROWS, GRID_W, WIN_ROWS * GRID_W)


def _na(qkv, kv_ctx, bias, *, seq, n_ctx, kc_off, vc_off):
    t = qkv.shape[0]
    nb = t // seq
    hp = HEAD_PAIRS
    return pl.pallas_call(
        _na_kernel,
        grid=(hp, nb),
        in_specs=[pl.BlockSpec((seq, LANES), lambda h, b: (b, h)),
                  pl.BlockSpec((seq, LANES), lambda h, b: (b, hp + h)),
                  pl.BlockSpec((seq, LANES), lambda h, b: (b, 2 * hp + h)),
                  pl.BlockSpec((n_ctx, LANES), lambda h, b: (b, kc_off + h)),
                  pl.BlockSpec((n_ctx, LANES), lambda h, b: (b, vc_off + h)),
                  pl.BlockSpec((1, 2, WIN_ROWS, GRID_W, WIN_ROWS * GRID_W), lambda h, b: (h, 0, 0, 0, 0))],
        out_specs=pl.BlockSpec((seq, LANES), lambda h, b: (b, h)),
        out_shape=jax.ShapeDtypeStruct((t, D_NA), BF16),
        compiler_params=_params("parallel", "parallel"),
        name="na_attention",
    )(qkv, qkv, qkv, kv_ctx, kv_ctx, bias)


def _ctx_attn_kernel(q_ref, k_ref, v_ref, o_ref):
    first = _head_mask()
    q = q_ref[...]
    k = k_ref[...]
    v = v_ref[...]
    outs = []
    for a in range(2):
        qa = jnp.where(first if a == 0 else ~first, q, jnp.zeros_like(q))
        s = lax.dot_general(qa, k, (((1,), (1,)), ((), ())), preferred_element_type=F32)
        p = jnp.exp(s - jnp.max(s, axis=-1, keepdims=True))
        den = jnp.sum(p, axis=-1, keepdims=True)
        outs.append(_bdot(p.astype(BF16), v) / den)
    o_ref[...] = jnp.where(first, outs[0], outs[1]).astype(BF16)


def _ctx_attn(qkv, *, n_ctx):
    t = qkv.shape[0]
    hp = HEAD_PAIRS
    return pl.pallas_call(
        _ctx_attn_kernel,
        grid=(t // n_ctx, hp),
        in_specs=[pl.BlockSpec((n_ctx, LANES), lambda b, h: (b, h)),
                  pl.BlockSpec((n_ctx, LANES), lambda b, h: (b, hp + h)),
                  pl.BlockSpec((n_ctx, LANES), lambda b, h: (b, 2 * hp + h))],
        out_specs=pl.BlockSpec((n_ctx, LANES), lambda b, h: (b, h)),
        out_shape=jax.ShapeDtypeStruct((t, D_NA), BF16),
        compiler_params=_params("parallel", "parallel"),
        name="ctx_attention",
    )(qkv, qkv, qkv)


def _outmlp_kernel(h_ref, yf_ref, cv_ref, at_ref, mod_ref, g_ref, gf_ref, wo_ref, w1_ref, w2_ref, o_ref,
                   h1_ref, hn_ref, acc_ref, *, final_norm):
    j = pl.program_id(1)

    @pl.when(j == 0)
    def _():
        mix = (_bdot(yf_ref[...], wo_ref[:D_FOURIER, :])
               + _bdot(cv_ref[...], wo_ref[D_FOURIER:D_FOURIER + D_CONV, :])
               + _bdot(at_ref[...], wo_ref[D_FOURIER + D_CONV:, :]))
        h1 = h_ref[...] + mod_ref[0, 2:3, :] * mix
        h1_ref[...] = h1
        hn_ref[...] = _rms_mod(h1, g_ref[...], mod_ref[0, 3:4, :], mod_ref[0, 4:5, :]).astype(BF16)
        acc_ref[...] = jnp.zeros_like(acc_ref)

    a = jnp.maximum(_bdot(hn_ref[...], w1_ref[...]), 0.0)
    acc_ref[...] += _bdot((a * a).astype(BF16), w2_ref[...])

    @pl.when(j == pl.num_programs(1) - 1)
    def _():
        out = h1_ref[...] + mod_ref[0, 5:6, :] * acc_ref[...]
        if final_norm:
            out = out * lax.rsqrt(jnp.mean(out * out, axis=-1, keepdims=True) + RMS_EPS) * gf_ref[...]
        o_ref[...] = out


def _outmlp(h, yf, cv, at, mods, g2, gf, wo, w1, w2, *, seq, tm, tf, ctx, final_norm):
    t, d = h.shape
    nb = t // seq
    per = seq // tm
    row = (lambda i: nb) if ctx else (lambda i: i // per)
    d_ff = w1.shape[1]
    vec = pl.BlockSpec((1, d), lambda i, j: (0, 0))
    return pl.pallas_call(
        functools.partial(_outmlp_kernel, final_norm=final_norm),
        grid=(t // tm, d_ff // tf),
        in_specs=[pl.BlockSpec((tm, d), lambda i, j: (i, 0)),
                  pl.BlockSpec((tm, D_FOURIER), lambda i, j: (i % per, i // per)),
                  pl.BlockSpec((tm, D_CONV), lambda i, j: (i, 0)),
                  pl.BlockSpec((tm, D_NA), lambda i, j: (i, 0)),
                  pl.BlockSpec((1, N_MOD, d), lambda i, j: (row(i), 0, 0)),
                  vec, vec,
                  pl.BlockSpec((D_MIX, d), lambda i, j: (0, 0)),
                  pl.BlockSpec((d, tf), lambda i, j: (0, j)),
                  pl.BlockSpec((tf, d), lambda i, j: (j, 0))],
        out_specs=pl.BlockSpec((tm, d), lambda i, j: (i, 0)),
        out_shape=jax.ShapeDtypeStruct((t, d), F32),
        scratch_shapes=[pltpu.VMEM((tm, d), F32), pltpu.VMEM((tm, d), BF16), pltpu.VMEM((tm, d), F32)],
        compiler_params=_params("parallel", "arbitrary"),
        name="outmlp_ctx" if ctx else "outmlp_lat",
    )(h, yf, cv, at, mods, g2, gf, wo, w1, w2)


def kernel(x, c, ctx, c_ctx, ada_w, ada_b, norm1_g, norm2_g, w_in, w_fourier, conv_dw_w, conv_dw_b, conv_norm_g,
           conv_norm_b, conv_pw_w, conv_pw_b, na_rpb, w_out, mlp_w1, mlp_w2, final_norm_g):
    nb, seq, d = x.shape
    n_ctx = ctx.shape[1]
    depth = ada_w.shape[0]
    assert nb < MOD_ROWS and seq % GRID_W == 0 and d == D_MODEL

    cc = jnp.concatenate([c, c_ctx[None], jnp.zeros((MOD_ROWS - nb - 1, d), F32)], axis=0)
    mods = _adaln(cc, ada_w, ada_b).reshape(depth, MOD_ROWS, N_MOD, d)
    ab_lat = _fold_fourier(w_fourier, seq)
    ab_ctx = _fold_fourier(w_fourier, n_ctx)
    c_lat, s_lat = _dft_mats(seq)
    c_ctx_m, s_ctx_m = _dft_mats(n_ctx)

    w_in_b = w_in.astype(BF16)
    w_out_b = w_out.astype(BF16)
    w1_b = mlp_w1.astype(BF16)
    w2_b = mlp_w2.astype(BF16)
    pww_b = conv_pw_w.astype(BF16)
    gf = final_norm_g.reshape(1, d)

    h_lat = x.reshape(nb * seq, d)
    h_ctx = ctx.reshape(nb * n_ctx, d)
    for i in range(depth):
        last = i == depth - 1
        g1 = norm1_g[i].reshape(1, d)
        g2 = norm2_g[i].reshape(1, d)
        conv_p = (conv_dw_w[i], conv_dw_b[i].reshape(1, -1), conv_norm_g[i].reshape(1, -1),
                  conv_norm_b[i].reshape(1, -1), pww_b[i], conv_pw_b[i].reshape(1, -1))
        bias = _na_bias(na_rpb[i])

        if last:
            kv_ctx = _kvproj(h_ctx, g1, mods[i], w_in_b[i][:, KV_START:], nb=nb, tm=n_ctx)
            kc_off, vc_off = 0, HEAD_PAIRS
        else:
            zf_c, uc_c, qkv_c = _inproj(h_ctx, g1, mods[i], w_in_b[i], ab_ctx[i], seq=n_ctx, tm=n_ctx, ctx=True)
            kv_ctx = qkv_c
            kc_off, vc_off = HEAD_PAIRS, 2 * HEAD_PAIRS

        zf, uc, qkv = _inproj(h_lat, g1, mods[i], w_in_b[i], ab_lat[i], seq=seq, tm=512, ctx=False)
        yf = _fourier(c_lat, s_lat, zf, tm=1024)
        cv = _conv(uc, *conv_p, seq=seq)
        at = _na(qkv, kv_ctx, bias, seq=seq, n_ctx=n_ctx, kc_off=kc_off, vc_off=vc_off)
        h_lat = _outmlp(h_lat, yf, cv, at, mods[i], g2, gf, w_out_b[i], w1_b[i], w2_b[i],
                        seq=seq, tm=1024, tf=1024, ctx=False, final_norm=last)

        if not last:
            yf_c = _fourier(c_ctx_m, s_ctx_m, zf_c, tm=n_ctx)
            cv_c = _conv(uc_c, *conv_p, seq=n_ctx)
            at_c = _ctx_attn(qkv_c, n_ctx=n_ctx)
            h_ctx = _outmlp(h_ctx, yf_c, cv_c, at_c, mods[i], g2, gf, w_out_b[i], w1_b[i], w2_b[i],
                            seq=n_ctx, tm=n_ctx, tf=1024, ctx=True, final_norm=False)
    return h_lat.reshape(nb, seq, d)
```

```python
import functools

import numpy as np
import jax
import jax.numpy as jnp
from jax import lax
from jax.experimental import pallas as pl
from jax.experimental.pallas import tpu as pltpu

D_MODEL = 1024
GRID_W = 64
D_FOURIER = 256
FOURIER_GROUPS = 4
D_CONV = 256
CONV_WIDTH = 31
N_NA_HEADS = 8
NA_HEAD_DIM = 64
D_NA = N_NA_HEADS * NA_HEAD_DIM
WIN_ROWS = 8
WIN_COLS = 16
QKV_START = D_FOURIER + 2 * D_CONV
KV_START = QKV_START + D_NA
D_MIX = D_FOURIER + D_CONV + D_NA
N_MOD = 6
RMS_EPS = 1e-6
LN_EPS = 1e-5

LANES = 128
HEAD_PAIRS = N_NA_HEADS * NA_HEAD_DIM // LANES
MOD_ROWS = 16
VMEM_LIMIT = 56 * 1024 * 1024

F32 = jnp.float32
BF16 = jnp.bfloat16


def _params(*sem):
    return pltpu.CompilerParams(dimension_semantics=sem, vmem_limit_bytes=VMEM_LIMIT)


def _sigmoid(x):
    return 1.0 / (1.0 + jnp.exp(-x))


def _rms_mod(x, g, shift, scale):
    y = x * lax.rsqrt(jnp.mean(x * x, axis=-1, keepdims=True) + RMS_EPS) * g
    return y * (1.0 + scale) + shift


def _bdot(a, b):
    return jnp.dot(a, b, preferred_element_type=F32)


def _adaln_kernel(c_ref, w_ref, b_ref, o_ref):
    cc = c_ref[...]
    a = cc * _sigmoid(cc)
    o_ref[0] = jnp.dot(a, w_ref[0], preferred_element_type=F32, precision=lax.Precision.HIGHEST) + b_ref[0]


def _adaln(cc, ada_w, ada_b):
    depth, d, n = ada_w.shape
    tn = 1024
    return pl.pallas_call(
        _adaln_kernel,
        grid=(depth, n // tn),
        in_specs=[pl.BlockSpec((MOD_ROWS, d), lambda l, j: (0, 0)),
                  pl.BlockSpec((1, d, tn), lambda l, j: (l, 0, j)),
                  pl.BlockSpec((1, 1, tn), lambda l, j: (l, 0, j))],
        out_specs=pl.BlockSpec((1, MOD_ROWS, tn), lambda l, j: (l, 0, j)),
        out_shape=jax.ShapeDtypeStruct((depth, MOD_ROWS, n), F32),
        compiler_params=_params("parallel", "parallel"),
        name="adaln",
    )(cc, ada_w, ada_b.reshape(depth, 1, n))


def _fold_kernel(cc_ref, sc_ref, w_ref, o_ref):
    w = w_ref[0]
    hi = lax.Precision.HIGHEST
    o_ref[0, :, :D_FOURIER] = jnp.dot(cc_ref[...], w, preferred_element_type=F32, precision=hi).astype(BF16)
    o_ref[0, :, D_FOURIER:] = jnp.dot(sc_ref[...], w, preferred_element_type=F32, precision=hi).astype(BF16)


def _fold_fourier(w_fourier, seq):
    depth = w_fourier.shape[0]
    gs = D_FOURIER // FOURIER_GROUPS
    idx = np.arange(D_FOURIER)
    same = (idx[:, None] // gs) == (idx[None, :] // gs)
    ang = 2.0 * np.pi * (((idx[:, None] % gs) * (idx[None, :] % gs)) % gs) / gs
    scale = 1.0 / np.sqrt(seq * gs)
    cc = jnp.asarray(np.where(same, np.cos(ang), 0.0) * scale, F32)
    sc = jnp.asarray(np.where(same, np.sin(ang), 0.0) * scale, F32)
    full = pl.BlockSpec((D_FOURIER, D_FOURIER), lambda l: (0, 0))
    return pl.pallas_call(
        _fold_kernel,
        grid=(depth,),
        in_specs=[full, full, pl.BlockSpec((1, D_FOURIER, D_FOURIER), lambda l: (l, 0, 0))],
        out_specs=pl.BlockSpec((1, D_FOURIER, 2 * D_FOURIER), lambda l: (l, 0, 0)),
        out_shape=jax.ShapeDtypeStruct((depth, D_FOURIER, 2 * D_FOURIER), BF16),
        compiler_params=_params("parallel"),
        name="fold_fourier",
    )(cc, sc, w_fourier)


def _dft_mats(seq):
    k = np.arange(seq)
    ang = 2.0 * np.pi * ((k[:, None] * k[None, :]) % seq) / seq
    return jnp.asarray(np.cos(ang), F32).astype(BF16), jnp.asarray(np.sin(ang), F32).astype(BF16)


def _inproj_kernel(x_ref, g_ref, mod_ref, w_ref, ab_ref, zf_ref, uc_ref, qkv_ref):
    hn = _rms_mod(x_ref[...], g_ref[...], mod_ref[0, 0:1, :], mod_ref[0, 1:2, :]).astype(BF16)
    uf = _bdot(hn, w_ref[:, :D_FOURIER])
    zf_ref[...] = _bdot(uf.astype(BF16), ab_ref[...]).astype(BF16)
    uc_ref[...] = _bdot(hn, w_ref[:, D_FOURIER:QKV_START])
    qkv_ref[:, :D_NA] = (_bdot(hn, w_ref[:, QKV_START:KV_START]) * (NA_HEAD_DIM ** -0.5)).astype(BF16)
    qkv_ref[:, D_NA:] = _bdot(hn, w_ref[:, KV_START:]).astype(BF16)


def _inproj(h, g, mods, w_in, ab, *, seq, tm, ctx):
    t, d = h.shape
    nb = t // seq
    per = seq // tm
    row = (lambda i: nb) if ctx else (lambda i: i // per)
    d_in = w_in.shape[1]
    return pl.pallas_call(
        _inproj_kernel,
        grid=(t // tm,),
        in_specs=[pl.BlockSpec((tm, d), lambda i: (i, 0)),
                  pl.BlockSpec((1, d), lambda i: (0, 0)),
                  pl.BlockSpec((1, N_MOD, d), lambda i: (row(i), 0, 0)),
                  pl.BlockSpec((d, d_in), lambda i: (0, 0)),
                  pl.BlockSpec((D_FOURIER, 2 * D_FOURIER), lambda i: (0, 0))],
        out_specs=[pl.BlockSpec((tm, 2 * D_FOURIER), lambda i: (i % per, i // per)),
                   pl.BlockSpec((tm, 2 * D_CONV), lambda i: (i, 0)),
                   pl.BlockSpec((tm, 3 * D_NA), lambda i: (i, 0))],
        out_shape=[jax.ShapeDtypeStruct((seq, nb * 2 * D_FOURIER), BF16),
                   jax.ShapeDtypeStruct((t, 2 * D_CONV), F32),
                   jax.ShapeDtypeStruct((t, 3 * D_NA), BF16)],
        compiler_params=_params("parallel"),
        name="inproj_ctx" if ctx else "inproj_lat",
    )(h, g, mods, w_in, ab)


def _kvproj_kernel(x_ref, g_ref, mod_ref, w_ref, kv_ref):
    hn = _rms_mod(x_ref[...], g_ref[...], mod_ref[0, 0:1, :], mod_ref[0, 1:2, :]).astype(BF16)
    kv_ref[...] = _bdot(hn, w_ref[...]).astype(BF16)


def _kvproj(h, g, mods, w_kv, *, nb, tm):
    t, d = h.shape
    n = w_kv.shape[1]
    return pl.pallas_call(
        _kvproj_kernel,
        grid=(t // tm,),
        in_specs=[pl.BlockSpec((tm, d), lambda i: (i, 0)),
                  pl.BlockSpec((1, d), lambda i: (0, 0)),
                  pl.BlockSpec((1, N_MOD, d), lambda i: (nb, 0, 0)),
                  pl.BlockSpec((d, n), lambda i: (0, 0))],
        out_specs=pl.BlockSpec((tm, n), lambda i: (i, 0)),
        out_shape=jax.ShapeDtypeStruct((t, n), BF16),
        compiler_params=_params("parallel"),
        name="kvproj_ctx",
    )(h, g, mods, w_kv)


def _fourier_kernel(c_ref, s_ref, z_ref, y_ref):
    y = _bdot(c_ref[...], z_ref[:, :D_FOURIER]) - _bdot(s_ref[...], z_ref[:, D_FOURIER:])
    y_ref[...] = y.astype(BF16)


def _fourier(cmat, smat, zf, *, tm):
    seq = cmat.shape[0]
    nb = zf.shape[1] // (2 * D_FOURIER)
    return pl.pallas_call(
        _fourier_kernel,
        grid=(seq // tm, nb),
        in_specs=[pl.BlockSpec((tm, seq), lambda i, b: (i, 0)),
                  pl.BlockSpec((tm, seq), lambda i, b: (i, 0)),
                  pl.BlockSpec((seq, 2 * D_FOURIER), lambda i, b: (0, b))],
        out_specs=pl.BlockSpec((tm, D_FOURIER), lambda i, b: (i, b)),
        out_shape=jax.ShapeDtypeStruct((seq, nb * D_FOURIER), BF16),
        compiler_params=_params("parallel", "parallel"),
        name="fourier_pos",
    )(cmat, smat, zf)


CONV_PAD = 16
CONV_CHUNK = 128


def _conv_kernel(u_ref, dww_ref, dwb_ref, lng_ref, lnb_ref, pww_ref, pwb_ref, o_ref, vpad_ref):
    seq = u_ref.shape[0]
    zeros = jnp.zeros((CONV_PAD, D_CONV), F32)
    vpad_ref[0:CONV_PAD, :] = zeros
    vpad_ref[CONV_PAD + seq:2 * CONV_PAD + seq, :] = zeros
    vpad_ref[CONV_PAD:CONV_PAD + seq, :] = u_ref[:, :D_CONV] * _sigmoid(u_ref[:, D_CONV:])
    first = CONV_PAD - CONV_WIDTH // 2

    def chunk(ci, carry):
        base = pl.multiple_of(ci * CONV_CHUNK, CONV_CHUNK)
        acc = jnp.zeros((CONV_CHUNK, D_CONV), F32) + dwb_ref[...]
        win = vpad_ref[pl.ds(base, CONV_CHUNK + 2 * CONV_PAD), :]
        for t in range(CONV_WIDTH):
            acc = acc + win[first + t:first + t + CONV_CHUNK, :] * dww_ref[t:t + 1, :]
        mu = jnp.mean(acc, axis=-1, keepdims=True)
        cen = acc - mu
        var = jnp.mean(cen * cen, axis=-1, keepdims=True)
        y = cen * lax.rsqrt(var + LN_EPS) * lng_ref[...] + lnb_ref[...]
        y = y * _sigmoid(y)
        o_ref[pl.ds(base, CONV_CHUNK), :] = (_bdot(y.astype(BF16), pww_ref[...]) + pwb_ref[...]).astype(BF16)
        return carry

    lax.fori_loop(0, seq // CONV_CHUNK, chunk, 0)


def _conv(uc, dww, dwb, lng, lnb, pww, pwb, *, seq):
    t = uc.shape[0]
    vec = pl.BlockSpec((1, D_CONV), lambda b: (0, 0))
    return pl.pallas_call(
        _conv_kernel,
        grid=(t // seq,),
        in_specs=[pl.BlockSpec((seq, 2 * D_CONV), lambda b: (b, 0)),
                  pl.BlockSpec((CONV_WIDTH, D_CONV), lambda b: (0, 0)),
                  vec, vec, vec,
                  pl.BlockSpec((D_CONV, D_CONV), lambda b: (0, 0)),
                  vec],
        out_specs=pl.BlockSpec((seq, D_CONV), lambda b: (b, 0)),
        out_shape=jax.ShapeDtypeStruct((t, D_CONV), BF16),
        scratch_shapes=[pltpu.VMEM((seq + 2 * CONV_PAD, D_CONV), F32)],
        compiler_params=_params("parallel"),
        name="conv_module",
    )(uc, dww, dwb, lng, lnb, pww, pwb)


def _head_mask():
    return lax.broadcasted_iota(jnp.int32, (1, LANES), 1) < NA_HEAD_DIM


Q_ROWS = 2
BAND_ROWS = 10
assert BAND_ROWS >= WIN_ROWS + Q_ROWS - 1 and (BAND_ROWS * GRID_W) % LANES == 0


def _na_patterns(rows):
    starts, sigs = [], []
    for g in range(rows // Q_ROWS):
        start = int(np.clip(Q_ROWS * g - WIN_ROWS // 2, 0, rows - BAND_ROWS))
        r = Q_ROWS * g + np.arange(Q_ROWS)
        rs = np.clip(r - WIN_ROWS // 2, 0, rows - WIN_ROWS)
        starts.append(start)
        sigs.append((start - Q_ROWS * g,) + tuple(rs - r))
    run_starts = [g for g in range(len(sigs)) if g == 0 or sigs[g] != sigs[g - 1]]
    assert len(set(sigs)) == len(run_starts)
    return starts, run_starts


def _na_kernel(q_ref, k_ref, v_ref, kc_ref, vc_ref, bias_ref, o_ref, s_ref, p_ref, rden_ref):
    rows = q_ref.shape[0] // GRID_W
    n_groups = rows // Q_ROWS
    n_q = Q_ROWS * GRID_W
    n_loc = BAND_ROWS * GRID_W
    _, run_starts = _na_patterns(rows)
    first = _head_mask()
    dn = (((1,), (1,)), ((), ()))
    assert n_groups % 2 == 0 and n_groups >= 4

    def band(g):
        start = jnp.clip(Q_ROWS * g - WIN_ROWS // 2, 0, rows - BAND_ROWS)
        return pl.multiple_of(start * GRID_W, LANES)

    def q_start(g):
        return pl.multiple_of(jnp.asarray(g, jnp.int32) * n_q, n_q)

    def scores(g, slot):
        pat = sum(jnp.asarray(g >= s, jnp.int32) for s in run_starts[1:])
        q = q_ref[pl.ds(q_start(g), n_q), :]
        kb = k_ref[pl.ds(band(g), n_loc), :]
        for a in range(2):
            qa = jnp.where(first if a == 0 else ~first, q, jnp.zeros_like(q))
            s_ref[slot, a, :, :n_loc] = lax.dot_general(qa, kb, dn, preferred_element_type=F32) + bias_ref[0, a, pat]
            s_ref[slot, a, :, n_loc:] = lax.dot_general(qa, kc_ref[...], dn, preferred_element_type=F32)

    def softmax(slot):
        for a in range(2):
            s = s_ref[slot, a]
            p = jnp.exp(s - jnp.max(s, axis=-1, keepdims=True))
            p_ref[slot, a] = p.astype(BF16)
            rden_ref[slot, a] = jnp.broadcast_to(1.0 / jnp.sum(p, axis=-1, keepdims=True), (n_q, LANES))

    def values(g, slot):
        vb = v_ref[pl.ds(band(g), n_loc), :]
        outs = []
        for a in range(2):
            o = _bdot(p_ref[slot, a, :, :n_loc], vb) + _bdot(p_ref[slot, a, :, n_loc:], vc_ref[...])
            outs.append(o * rden_ref[slot, a])
        o_ref[pl.ds(q_start(g), n_q), :] = jnp.where(first, outs[0], outs[1]).astype(BF16)

    scores(0, 0)
    scores(1, 1)
    softmax(0)

    def pair(j, carry):
        g = 2 * j + 1
        scores(g + 1, 0)
        softmax(1)
        values(g - 1, 0)
        scores(g + 2, 1)
        softmax(0)
        values(g, 1)
        return carry

    lax.fori_loop(0, n_groups // 2 - 1, pair, 0)
    softmax(1)
    values(n_groups - 2, 0)
    values(n_groups - 1, 1)


def _na_bias(rpb, rows):
    n_heads, n_dr, n_dc = rpb.shape
    lead = GRID_W - WIN_COLS
    wpad = jnp.pad(rpb.astype(F32), ((0, 0), (0, 0), (lead, 2 * GRID_W - n_dc - lead)))
    flat = jnp.tile(wpad, (1, 1, GRID_W))[..., :GRID_W * (2 * GRID_W - 1)]
    toep = flat.reshape(n_heads, n_dr, GRID_W, 2 * GRID_W - 1)[..., GRID_W - 1:]
    col = np.arange(GRID_W)
    col_start = np.clip(col - WIN_COLS // 2, 0, GRID_W - WIN_COLS)
    col_valid = (col[None, :] >= col_start[:, None]) & (col[None, :] < col_start[:, None] + WIN_COLS)
    assert np.all(np.abs(col[None, :] - col[:, None])[col_valid] <= WIN_COLS - 1)
    toep = jnp.where(jnp.asarray(col_valid), toep, -jnp.inf)
    masked = jnp.full((n_heads, GRID_W, GRID_W), -jnp.inf, F32)
    starts, run_starts = _na_patterns(rows)
    tiles = []
    for g in run_starts:
        for qi in range(Q_ROWS):
            r = Q_ROWS * g + qi
            rs = int(np.clip(r - WIN_ROWS // 2, 0, rows - WIN_ROWS))
            for kj in range(BAND_ROWS):
                kr = starts[g] + kj
                tiles.append(toep[:, kr - r + WIN_ROWS - 1] if rs <= kr < rs + WIN_ROWS else masked)
    b = jnp.stack(tiles).reshape(len(run_starts), Q_ROWS, BAND_ROWS, n_heads, GRID_W, GRID_W)
    b = b.transpose(3, 0, 1, 4, 2, 5)
    return b.reshape(HEAD_PAIRS, 2, len(run_starts), Q_ROWS * GRID_W, BAND_ROWS * GRID_W)


def _na(qkv, kv_ctx, bias, *, seq, n_ctx, kc_off, vc_off):
    t = qkv.shape[0]
    nb = t // seq
    hp = HEAD_PAIRS
    return pl.pallas_call(
        _na_kernel,
        grid=(hp, nb),
        in_specs=[pl.BlockSpec((seq, LANES), lambda h, b: (b, h)),
                  pl.BlockSpec((seq, LANES), lambda h, b: (b, hp + h)),
                  pl.BlockSpec((seq, LANES), lambda h, b: (b, 2 * hp + h)),
                  pl.BlockSpec((n_ctx, LANES), lambda h, b: (b, kc_off + h)),
                  pl.BlockSpec((n_ctx, LANES), lambda h, b: (b, vc_off + h)),
                  pl.BlockSpec((1,) + bias.shape[1:], lambda h, b: (h, 0, 0, 0, 0))],
        out_specs=pl.BlockSpec((seq, LANES), lambda h, b: (b, h)),
        out_shape=jax.ShapeDtypeStruct((t, D_NA), BF16),
        scratch_shapes=[pltpu.VMEM((2, 2, Q_ROWS * GRID_W, BAND_ROWS * GRID_W + n_ctx), F32),
                        pltpu.VMEM((2, 2, Q_ROWS * GRID_W, BAND_ROWS * GRID_W + n_ctx), BF16),
                        pltpu.VMEM((2, 2, Q_ROWS * GRID_W, LANES), F32)],
        compiler_params=_params("parallel", "parallel"),
        name="na_attention",
    )(qkv, qkv, qkv, kv_ctx, kv_ctx, bias)


def _ctx_attn_kernel(q_ref, k_ref, v_ref, o_ref):
    first = _head_mask()
    q = q_ref[...]
    k = k_ref[...]
    v = v_ref[...]
    outs = []
    for a in range(2):
        qa = jnp.where(first if a == 0 else ~first, q, jnp.zeros_like(q))
        s = lax.dot_general(qa, k, (((1,), (1,)), ((), ())), preferred_element_type=F32)
        p = jnp.exp(s - jnp.max(s, axis=-1, keepdims=True))
        den = jnp.sum(p, axis=-1, keepdims=True)
        outs.append(_bdot(p.astype(BF16), v) / den)
    o_ref[...] = jnp.where(first, outs[0], outs[1]).astype(BF16)


def _ctx_attn(qkv, *, n_ctx):
    t = qkv.shape[0]
    hp = HEAD_PAIRS
    return pl.pallas_call(
        _ctx_attn_kernel,
        grid=(t // n_ctx, hp),
        in_specs=[pl.BlockSpec((n_ctx, LANES), lambda b, h: (b, h)),
                  pl.BlockSpec((n_ctx, LANES), lambda b, h: (b, hp + h)),
                  pl.BlockSpec((n_ctx, LANES), lambda b, h: (b, 2 * hp + h))],
        out_specs=pl.BlockSpec((n_ctx, LANES), lambda b, h: (b, h)),
        out_shape=jax.ShapeDtypeStruct((t, D_NA), BF16),
        compiler_params=_params("parallel", "parallel"),
        name="ctx_attention",
    )(qkv, qkv, qkv)


def _outmlp_kernel(h_ref, yf_ref, cv_ref, at_ref, mod_ref, g_ref, gf_ref, wo_ref, w1_ref, w2_ref, o_ref,
                   h1_ref, hn_ref, acc_ref, *, final_norm):
    j = pl.program_id(1)

    @pl.when(j == 0)
    def _():
        mix = (_bdot(yf_ref[...], wo_ref[:D_FOURIER, :])
               + _bdot(cv_ref[...], wo_ref[D_FOURIER:D_FOURIER + D_CONV, :])
               + _bdot(at_ref[...], wo_ref[D_FOURIER + D_CONV:, :]))
        h1 = h_ref[...] + mod_ref[0, 2:3, :] * mix
        h1_ref[...] = h1
        hn_ref[...] = _rms_mod(h1, g_ref[...], mod_ref[0, 3:4, :], mod_ref[0, 4:5, :]).astype(BF16)
        acc_ref[...] = jnp.zeros_like(acc_ref)

    a = jnp.maximum(_bdot(hn_ref[...], w1_ref[...]), 0.0)
    acc_ref[...] += _bdot((a * a).astype(BF16), w2_ref[...])

    @pl.when(j == pl.num_programs(1) - 1)
    def _():
        out = h1_ref[...] + mod_ref[0, 5:6, :] * acc_ref[...]
        if final_norm:
            out = out * lax.rsqrt(jnp.mean(out * out, axis=-1, keepdims=True) + RMS_EPS) * gf_ref[...]
        o_ref[...] = out


def _outmlp(h, yf, cv, at, mods, g2, gf, wo, w1, w2, *, seq, tm, tf, ctx, final_norm):
    t, d = h.shape
    nb = t // seq
    per = seq // tm
    row = (lambda i: nb) if ctx else (lambda i: i // per)
    d_ff = w1.shape[1]
    vec = pl.BlockSpec((1, d), lambda i, j: (0, 0))
    return pl.pallas_call(
        functools.partial(_outmlp_kernel, final_norm=final_norm),
        grid=(t // tm, d_ff // tf),
        in_specs=[pl.BlockSpec((tm, d), lambda i, j: (i, 0)),
                  pl.BlockSpec((tm, D_FOURIER), lambda i, j: (i % per, i // per)),
                  pl.BlockSpec((tm, D_CONV), lambda i, j: (i, 0)),
                  pl.BlockSpec((tm, D_NA), lambda i, j: (i, 0)),
                  pl.BlockSpec((1, N_MOD, d), lambda i, j: (row(i), 0, 0)),
                  vec, vec,
                  pl.BlockSpec((D_MIX, d), lambda i, j: (0, 0)),
                  pl.BlockSpec((d, tf), lambda i, j: (0, j)),
                  pl.BlockSpec((tf, d), lambda i, j: (j, 0))],
        out_specs=pl.BlockSpec((tm, d), lambda i, j: (i, 0)),
        out_shape=jax.ShapeDtypeStruct((t, d), F32),
        scratch_shapes=[pltpu.VMEM((tm, d), F32), pltpu.VMEM((tm, d), BF16), pltpu.VMEM((tm, d), F32)],
        compiler_params=_params("parallel", "arbitrary"),
        name="outmlp_ctx" if ctx else "outmlp_lat",
    )(h, yf, cv, at, mods, g2, gf, wo, w1, w2)


def kernel(x, c, ctx, c_ctx, ada_w, ada_b, norm1_g, norm2_g, w_in, w_fourier, conv_dw_w, conv_dw_b, conv_norm_g,
           conv_norm_b, conv_pw_w, conv_pw_b, na_rpb, w_out, mlp_w1, mlp_w2, final_norm_g):
    nb, seq, d = x.shape
    n_ctx = ctx.shape[1]
    depth = ada_w.shape[0]
    assert nb < MOD_ROWS and seq % GRID_W == 0 and d == D_MODEL

    cc = jnp.concatenate([c, c_ctx[None], jnp.zeros((MOD_ROWS - nb - 1, d), F32)], axis=0)
    mods = _adaln(cc, ada_w, ada_b).reshape(depth, MOD_ROWS, N_MOD, d)
    ab_lat = _fold_fourier(w_fourier, seq)
    ab_ctx = _fold_fourier(w_fourier, n_ctx)
    c_lat, s_lat = _dft_mats(seq)
    c_ctx_m, s_ctx_m = _dft_mats(n_ctx)

    w_in_b = w_in.astype(BF16)
    w_out_b = w_out.astype(BF16)
    w1_b = mlp_w1.astype(BF16)
    w2_b = mlp_w2.astype(BF16)
    pww_b = conv_pw_w.astype(BF16)
    gf = final_norm_g.reshape(1, d)

    h_lat = x.reshape(nb * seq, d)
    h_ctx = ctx.reshape(nb * n_ctx, d)
    for i in range(depth):
        last = i == depth - 1
        g1 = norm1_g[i].reshape(1, d)
        g2 = norm2_g[i].reshape(1, d)
        conv_p = (conv_dw_w[i], conv_dw_b[i].reshape(1, -1), conv_norm_g[i].reshape(1, -1),
                  conv_norm_b[i].reshape(1, -1), pww_b[i], conv_pw_b[i].reshape(1, -1))
        bias = _na_bias(na_rpb[i], seq // GRID_W)

        if last:
            kv_ctx = _kvproj(h_ctx, g1, mods[i], w_in_b[i][:, KV_START:], nb=nb, tm=n_ctx)
            kc_off, vc_off = 0, HEAD_PAIRS
        else:
            zf_c, uc_c, qkv_c = _inproj(h_ctx, g1, mods[i], w_in_b[i], ab_ctx[i], seq=n_ctx, tm=n_ctx, ctx=True)
            kv_ctx = qkv_c
            kc_off, vc_off = HEAD_PAIRS, 2 * HEAD_PAIRS

        zf, uc, qkv = _inproj(h_lat, g1, mods[i], w_in_b[i], ab_lat[i], seq=seq, tm=512, ctx=False)
        yf = _fourier(c_lat, s_lat, zf, tm=1024)
        cv = _conv(uc, *conv_p, seq=seq)
        at = _na(qkv, kv_ctx, bias, seq=seq, n_ctx=n_ctx, kc_off=kc_off, vc_off=vc_off)
        h_lat = _outmlp(h_lat, yf, cv, at, mods[i], g2, gf, w_out_b[i], w1_b[i], w2_b[i],
                        seq=seq, tm=1024, tf=1024, ctx=False, final_norm=last)

        if not last:
            yf_c = _fourier(c_ctx_m, s_ctx_m, zf_c, tm=n_ctx)
            cv_c = _conv(uc_c, *conv_p, seq=n_ctx)
            at_c = _ctx_attn(qkv_c, n_ctx=n_ctx)
            h_ctx = _outmlp(h_ctx, yf_c, cv_c, at_c, mods[i], g2, gf, w_out_b[i], w1_b[i], w2_b[i],
                            seq=n_ctx, tm=n_ctx, tf=1024, ctx=True, final_norm=False)
    return h_lat.reshape(nb, seq, d)
```

```python
import functools

import numpy as np
import jax
import jax.numpy as jnp
from jax import lax
from jax.experimental import pallas as pl
from jax.experimental.pallas import tpu as pltpu

D_MODEL = 1024
GRID_W = 64
D_FOURIER = 256
FOURIER_GROUPS = 4
D_CONV = 256
CONV_WIDTH = 31
N_NA_HEADS = 8
NA_HEAD_DIM = 64
D_NA = N_NA_HEADS * NA_HEAD_DIM
WIN_ROWS = 8
WIN_COLS = 16
QKV_START = D_FOURIER + 2 * D_CONV
KV_START = QKV_START + D_NA
D_MIX = D_FOURIER + D_CONV + D_NA
N_MOD = 6
RMS_EPS = 1e-6
LN_EPS = 1e-5

LANES = 128
SUBLANES = 8
HEAD_PAIRS = N_NA_HEADS * NA_HEAD_DIM // LANES
MOD_ROWS = 16
VMEM_LIMIT = 56 * 1024 * 1024

F32 = jnp.float32
BF16 = jnp.bfloat16


def _params(*sem):
    return pltpu.CompilerParams(dimension_semantics=sem, vmem_limit_bytes=VMEM_LIMIT)


def _sigmoid(x):
    return 1.0 / (1.0 + jnp.exp(-x))


def _rms_mod(x, g, shift, scale):
    y = x * lax.rsqrt(jnp.mean(x * x, axis=-1, keepdims=True) + RMS_EPS) * g
    return y * (1.0 + scale) + shift


def _bdot(a, b):
    return jnp.dot(a, b, preferred_element_type=F32)


def _adaln_kernel(c_ref, w_ref, b_ref, o_ref):
    cc = c_ref[...]
    a = cc * _sigmoid(cc)
    o_ref[0] = jnp.dot(a, w_ref[0], preferred_element_type=F32, precision=lax.Precision.HIGHEST) + b_ref[0]


def _adaln(cc, ada_w, ada_b):
    depth, d, n = ada_w.shape
    tn = 1024
    return pl.pallas_call(
        _adaln_kernel,
        grid=(depth, n // tn),
        in_specs=[pl.BlockSpec((MOD_ROWS, d), lambda l, j: (0, 0)),
                  pl.BlockSpec((1, d, tn), lambda l, j: (l, 0, j)),
                  pl.BlockSpec((1, 1, tn), lambda l, j: (l, 0, j))],
        out_specs=pl.BlockSpec((1, MOD_ROWS, tn), lambda l, j: (l, 0, j)),
        out_shape=jax.ShapeDtypeStruct((depth, MOD_ROWS, n), F32),
        compiler_params=_params("parallel", "parallel"),
        name="adaln",
    )(cc, ada_w, ada_b.reshape(depth, 1, n))


def _fold_kernel(cc_ref, sc_ref, w_ref, o_ref):
    w = w_ref[0]
    hi = lax.Precision.HIGHEST
    o_ref[0, :, :D_FOURIER] = jnp.dot(cc_ref[...], w, preferred_element_type=F32, precision=hi).astype(BF16)
    o_ref[0, :, D_FOURIER:] = jnp.dot(sc_ref[...], w, preferred_element_type=F32, precision=hi).astype(BF16)


def _fold_fourier(w_fourier, seq):
    depth = w_fourier.shape[0]
    gs = D_FOURIER // FOURIER_GROUPS
    idx = np.arange(D_FOURIER)
    same = (idx[:, None] // gs) == (idx[None, :] // gs)
    ang = 2.0 * np.pi * (((idx[:, None] % gs) * (idx[None, :] % gs)) % gs) / gs
    scale = 1.0 / np.sqrt(seq * gs)
    cc = jnp.asarray(np.where(same, np.cos(ang), 0.0) * scale, F32)
    sc = jnp.asarray(np.where(same, np.sin(ang), 0.0) * scale, F32)
    full = pl.BlockSpec((D_FOURIER, D_FOURIER), lambda l: (0, 0))
    return pl.pallas_call(
        _fold_kernel,
        grid=(depth,),
        in_specs=[full, full, pl.BlockSpec((1, D_FOURIER, D_FOURIER), lambda l: (l, 0, 0))],
        out_specs=pl.BlockSpec((1, D_FOURIER, 2 * D_FOURIER), lambda l: (l, 0, 0)),
        out_shape=jax.ShapeDtypeStruct((depth, D_FOURIER, 2 * D_FOURIER), BF16),
        compiler_params=_params("parallel"),
        name="fold_fourier",
    )(cc, sc, w_fourier)


def _dft_mats(seq):
    k = np.arange(seq)
    ang = 2.0 * np.pi * ((k[:, None] * k[None, :]) % seq) / seq
    return jnp.asarray(np.cos(ang), F32).astype(BF16), jnp.asarray(np.sin(ang), F32).astype(BF16)


def _inproj_kernel(x_ref, g_ref, mod_ref, w_ref, ab_ref, zf_ref, uc_ref, qkv_ref):
    hn = _rms_mod(x_ref[...], g_ref[...], mod_ref[0, 0:1, :], mod_ref[0, 1:2, :]).astype(BF16)
    uf = _bdot(hn, w_ref[:, :D_FOURIER])
    zf_ref[...] = _bdot(uf.astype(BF16), ab_ref[...]).astype(BF16)
    uc_ref[...] = _bdot(hn, w_ref[:, D_FOURIER:QKV_START])
    qkv_ref[:, :D_NA] = (_bdot(hn, w_ref[:, QKV_START:KV_START]) * (NA_HEAD_DIM ** -0.5)).astype(BF16)
    qkv_ref[:, D_NA:] = _bdot(hn, w_ref[:, KV_START:]).astype(BF16)


def _inproj(h, g, mods, w_in, ab, *, seq, tm, ctx):
    t, d = h.shape
    nb = t // seq
    per = seq // tm
    row = (lambda i: nb) if ctx else (lambda i: i // per)
    d_in = w_in.shape[1]
    return pl.pallas_call(
        _inproj_kernel,
        grid=(t // tm,),
        in_specs=[pl.BlockSpec((tm, d), lambda i: (i, 0)),
                  pl.BlockSpec((1, d), lambda i: (0, 0)),
                  pl.BlockSpec((1, N_MOD, d), lambda i: (row(i), 0, 0)),
                  pl.BlockSpec((d, d_in), lambda i: (0, 0), pipeline_mode=pl.Buffered(1)),
                  pl.BlockSpec((D_FOURIER, 2 * D_FOURIER), lambda i: (0, 0))],
        out_specs=[pl.BlockSpec((tm, 2 * D_FOURIER), lambda i: (i % per, i // per)),
                   pl.BlockSpec((tm, 2 * D_CONV), lambda i: (i, 0)),
                   pl.BlockSpec((tm, 3 * D_NA), lambda i: (i, 0))],
        out_shape=[jax.ShapeDtypeStruct((seq, nb * 2 * D_FOURIER), BF16),
                   jax.ShapeDtypeStruct((t, 2 * D_CONV), F32),
                   jax.ShapeDtypeStruct((t, 3 * D_NA), BF16)],
        compiler_params=_params("parallel"),
        name="inproj_ctx" if ctx else "inproj_lat",
    )(h, g, mods, w_in, ab)


def _kvproj_kernel(x_ref, g_ref, mod_ref, w_ref, kv_ref):
    hn = _rms_mod(x_ref[...], g_ref[...], mod_ref[0, 0:1, :], mod_ref[0, 1:2, :]).astype(BF16)
    kv_ref[...] = _bdot(hn, w_ref[:, KV_START:]).astype(BF16)


def _kvproj(h, g, mods, w_in, *, nb, tm):
    t, d = h.shape
    d_in = w_in.shape[1]
    n = d_in - KV_START
    return pl.pallas_call(
        _kvproj_kernel,
        grid=(t // tm,),
        in_specs=[pl.BlockSpec((tm, d), lambda i: (i, 0)),
                  pl.BlockSpec((1, d), lambda i: (0, 0)),
                  pl.BlockSpec((1, N_MOD, d), lambda i: (nb, 0, 0)),
                  pl.BlockSpec((d, d_in), lambda i: (0, 0))],
        out_specs=pl.BlockSpec((tm, n), lambda i: (i, 0)),
        out_shape=jax.ShapeDtypeStruct((t, n), BF16),
        compiler_params=_params("parallel"),
        name="kvproj_ctx",
    )(h, g, mods, w_in)


def _fourier_kernel(c_ref, s_ref, z_ref, y_ref):
    y = _bdot(c_ref[...], z_ref[:, :D_FOURIER]) - _bdot(s_ref[...], z_ref[:, D_FOURIER:])
    y_ref[...] = y.astype(BF16)


def _fourier(cmat, smat, zf, *, tm):
    seq = cmat.shape[0]
    nb = zf.shape[1] // (2 * D_FOURIER)
    return pl.pallas_call(
        _fourier_kernel,
        grid=(seq // tm, nb),
        in_specs=[pl.BlockSpec((tm, seq), lambda i, b: (i, 0)),
                  pl.BlockSpec((tm, seq), lambda i, b: (i, 0)),
                  pl.BlockSpec((seq, 2 * D_FOURIER), lambda i, b: (0, b))],
        out_specs=pl.BlockSpec((tm, D_FOURIER), lambda i, b: (i, b)),
        out_shape=jax.ShapeDtypeStruct((seq, nb * D_FOURIER), BF16),
        compiler_params=_params("parallel", "parallel"),
        name="fourier_pos",
    )(cmat, smat, zf)


CONV_PAD = 16
CONV_CHUNK = 128


def _conv_kernel(u_ref, dww_ref, dwb_ref, lng_ref, lnb_ref, pww_ref, pwb_ref, o_ref, vs_ref):
    seq = u_ref.shape[0]
    n_pad = seq + 2 * CONV_PAD
    zeros = jnp.zeros((CONV_PAD, D_CONV), F32)
    vs_ref[0, 0:CONV_PAD, :] = zeros
    vs_ref[0, seq + CONV_PAD:n_pad, :] = zeros
    vs_ref[0, CONV_PAD:CONV_PAD + seq, :] = u_ref[:, :D_CONV] * _sigmoid(u_ref[:, D_CONV:])
    for s in range(1, SUBLANES):
        vs_ref[s, 0:n_pad - SUBLANES, :] = vs_ref[0, s:s + n_pad - SUBLANES, :]
    first = CONV_PAD - CONV_WIDTH // 2
    assert (first + CONV_WIDTH - 1) // SUBLANES * SUBLANES + seq <= n_pad - SUBLANES

    def chunk(ci, carry):
        base = pl.multiple_of(ci * CONV_CHUNK, CONV_CHUNK)
        acc = jnp.zeros((CONV_CHUNK, D_CONV), F32) + dwb_ref[...]
        for t in range(CONV_WIDTH):
            s, a = (first + t) % SUBLANES, (first + t) // SUBLANES
            acc = acc + vs_ref[s, pl.ds(base + a * SUBLANES, CONV_CHUNK), :] * dww_ref[t:t + 1, :]
        mu = jnp.mean(acc, axis=-1, keepdims=True)
        cen = acc - mu
        var = jnp.mean(cen * cen, axis=-1, keepdims=True)
        y = cen * lax.rsqrt(var + LN_EPS) * lng_ref[...] + lnb_ref[...]
        y = y * _sigmoid(y)
        o_ref[pl.ds(base, CONV_CHUNK), :] = (_bdot(y.astype(BF16), pww_ref[...]) + pwb_ref[...]).astype(BF16)
        return carry

    lax.fori_loop(0, seq // CONV_CHUNK, chunk, 0, unroll=2)


def _conv(uc, dww, dwb, lng, lnb, pww, pwb, *, seq):
    t = uc.shape[0]
    vec = pl.BlockSpec((1, D_CONV), lambda b: (0, 0))
    return pl.pallas_call(
        _conv_kernel,
        grid=(t // seq,),
        in_specs=[pl.BlockSpec((seq, 2 * D_CONV), lambda b: (b, 0)),
                  pl.BlockSpec((CONV_WIDTH, D_CONV), lambda b: (0, 0)),
                  vec, vec, vec,
                  pl.BlockSpec((D_CONV, D_CONV), lambda b: (0, 0)),
                  vec],
        out_specs=pl.BlockSpec((seq, D_CONV), lambda b: (b, 0)),
        out_shape=jax.ShapeDtypeStruct((t, D_CONV), BF16),
        scratch_shapes=[pltpu.VMEM((SUBLANES, seq + 2 * CONV_PAD, D_CONV), F32)],
        compiler_params=_params("parallel"),
        name="conv_module",
    )(uc, dww, dwb, lng, lnb, pww, pwb)


def _head_mask():
    return lax.broadcasted_iota(jnp.int32, (1, LANES), 1) < NA_HEAD_DIM


Q_ROWS = 2
BAND_ROWS = 10
assert BAND_ROWS >= WIN_ROWS + Q_ROWS - 1 and (BAND_ROWS * GRID_W) % LANES == 0


def _na_patterns(rows):
    starts, sigs = [], []
    for g in range(rows // Q_ROWS):
        start = int(np.clip(Q_ROWS * g - WIN_ROWS // 2, 0, rows - BAND_ROWS))
        r = Q_ROWS * g + np.arange(Q_ROWS)
        rs = np.clip(r - WIN_ROWS // 2, 0, rows - WIN_ROWS)
        starts.append(start)
        sigs.append((start - Q_ROWS * g,) + tuple(rs - r))
    run_starts = [g for g in range(len(sigs)) if g == 0 or sigs[g] != sigs[g - 1]]
    assert len(set(sigs)) == len(run_starts)
    return starts, run_starts


def _na_kernel(q_ref, k_ref, v_ref, kc_ref, vc_ref, bias_ref, o_ref, s_ref, p_ref, rden_ref):
    rows = q_ref.shape[0] // GRID_W
    n_groups = rows // Q_ROWS
    n_q = Q_ROWS * GRID_W
    n_loc = BAND_ROWS * GRID_W
    _, run_starts = _na_patterns(rows)
    first = _head_mask()
    dn = (((1,), (1,)), ((), ()))
    assert n_groups % 2 == 0 and n_groups >= 4

    def band(g):
        start = jnp.clip(Q_ROWS * g - WIN_ROWS // 2, 0, rows - BAND_ROWS)
        return pl.multiple_of(start * GRID_W, LANES)

    def q_start(g):
        return pl.multiple_of(jnp.asarray(g, jnp.int32) * n_q, n_q)

    def scores(g, slot):
        pat = sum(jnp.asarray(g >= s, jnp.int32) for s in run_starts[1:])
        q = q_ref[pl.ds(q_start(g), n_q), :]
        kb = k_ref[pl.ds(band(g), n_loc), :]
        for a in range(2):
            qa = jnp.where(first if a == 0 else ~first, q, jnp.zeros_like(q))
            s_ref[slot, a, :, :n_loc] = lax.dot_general(qa, kb, dn, preferred_element_type=F32) + bias_ref[0, a, pat]
            s_ref[slot, a, :, n_loc:] = lax.dot_general(qa, kc_ref[...], dn, preferred_element_type=F32)

    def softmax(slot):
        for a in range(2):
            s = s_ref[slot, a]
            p = jnp.exp(s - jnp.max(s, axis=-1, keepdims=True))
            p_ref[slot, a] = p.astype(BF16)
            rden_ref[slot, a] = jnp.broadcast_to(1.0 / jnp.sum(p, axis=-1, keepdims=True), (n_q, LANES))

    def values(g, slot):
        vb = v_ref[pl.ds(band(g), n_loc), :]
        outs = []
        for a in range(2):
            o = _bdot(p_ref[slot, a, :, :n_loc], vb) + _bdot(p_ref[slot, a, :, n_loc:], vc_ref[...])
            outs.append(o * rden_ref[slot, a])
        o_ref[pl.ds(q_start(g), n_q), :] = jnp.where(first, outs[0], outs[1]).astype(BF16)

    scores(0, 0)
    scores(1, 1)
    softmax(0)

    def pair(j, carry):
        g = 2 * j + 1
        scores(g + 1, 0)
        softmax(1)
        values(g - 1, 0)
        scores(g + 2, 1)
        softmax(0)
        values(g, 1)
        return carry

    lax.fori_loop(0, n_groups // 2 - 1, pair, 0)
    softmax(1)
    values(n_groups - 2, 0)
    values(n_groups - 1, 1)


def _na_bias(rpb, rows):
    n_heads, n_dr, n_dc = rpb.shape
    lead = GRID_W - WIN_COLS
    wpad = jnp.pad(rpb.astype(F32), ((0, 0), (0, 0), (lead, 2 * GRID_W - n_dc - lead)))
    flat = jnp.tile(wpad, (1, 1, GRID_W))[..., :GRID_W * (2 * GRID_W - 1)]
    toep = flat.reshape(n_heads, n_dr, GRID_W, 2 * GRID_W - 1)[..., GRID_W - 1:]
    col = np.arange(GRID_W)
    col_start = np.clip(col - WIN_COLS // 2, 0, GRID_W - WIN_COLS)
    col_valid = (col[None, :] >= col_start[:, None]) & (col[None, :] < col_start[:, None] + WIN_COLS)
    assert np.all(np.abs(col[None, :] - col[:, None])[col_valid] <= WIN_COLS - 1)
    toep = jnp.where(jnp.asarray(col_valid), toep, -jnp.inf)
    starts, run_starts = _na_patterns(rows)
    slabs = []
    for g in run_starts:
        for qi in range(Q_ROWS):
            r = Q_ROWS * g + qi
            rs = int(np.clip(r - WIN_ROWS // 2, 0, rows - WIN_ROWS))
            below = rs - starts[g]
            dr0 = rs - r + WIN_ROWS - 1
            slabs.append(jnp.pad(toep[:, dr0:dr0 + WIN_ROWS],
                                 ((0, 0), (below, BAND_ROWS - WIN_ROWS - below), (0, 0), (0, 0)),
                                 constant_values=-jnp.inf))
    b = jnp.stack(slabs).reshape(len(run_starts), Q_ROWS, n_heads, BAND_ROWS, GRID_W, GRID_W)
    b = b.transpose(2, 0, 1, 4, 3, 5)
    return b.reshape(HEAD_PAIRS, 2, len(run_starts), Q_ROWS * GRID_W, BAND_ROWS * GRID_W)


def _na(qkv, kv_ctx, bias, *, seq, n_ctx, kc_off, vc_off):
    t = qkv.shape[0]
    nb = t // seq
    hp = HEAD_PAIRS
    return pl.pallas_call(
        _na_kernel,
        grid=(hp, nb),
        in_specs=[pl.BlockSpec((seq, LANES), lambda h, b: (b, h)),
                  pl.BlockSpec((seq, LANES), lambda h, b: (b, hp + h)),
                  pl.BlockSpec((seq, LANES), lambda h, b: (b, 2 * hp + h)),
                  pl.BlockSpec((n_ctx, LANES), lambda h, b: (b, kc_off + h)),
                  pl.BlockSpec((n_ctx, LANES), lambda h, b: (b, vc_off + h)),
                  pl.BlockSpec((1,) + bias.shape[1:], lambda h, b: (h, 0, 0, 0, 0))],
        out_specs=pl.BlockSpec((seq, LANES), lambda h, b: (b, h)),
        out_shape=jax.ShapeDtypeStruct((t, D_NA), BF16),
        scratch_shapes=[pltpu.VMEM((2, 2, Q_ROWS * GRID_W, BAND_ROWS * GRID_W + n_ctx), F32),
                        pltpu.VMEM((2, 2, Q_ROWS * GRID_W, BAND_ROWS * GRID_W + n_ctx), BF16),
                        pltpu.VMEM((2, 2, Q_ROWS * GRID_W, LANES), F32)],
        compiler_params=_params("parallel", "parallel"),
        name="na_attention",
    )(qkv, qkv, qkv, kv_ctx, kv_ctx, bias)


def _ctx_attn_kernel(q_ref, k_ref, v_ref, o_ref):
    first = _head_mask()
    q = q_ref[...]
    k = k_ref[...]
    v = v_ref[...]
    outs = []
    for a in range(2):
        qa = jnp.where(first if a == 0 else ~first, q, jnp.zeros_like(q))
        s = lax.dot_general(qa, k, (((1,), (1,)), ((), ())), preferred_element_type=F32)
        p = jnp.exp(s - jnp.max(s, axis=-1, keepdims=True))
        den = jnp.sum(p, axis=-1, keepdims=True)
        outs.append(_bdot(p.astype(BF16), v) / den)
    o_ref[...] = jnp.where(first, outs[0], outs[1]).astype(BF16)


def _ctx_attn(qkv, *, n_ctx):
    t = qkv.shape[0]
    hp = HEAD_PAIRS
    return pl.pallas_call(
        _ctx_attn_kernel,
        grid=(t // n_ctx, hp),
        in_specs=[pl.BlockSpec((n_ctx, LANES), lambda b, h: (b, h)),
                  pl.BlockSpec((n_ctx, LANES), lambda b, h: (b, hp + h)),
                  pl.BlockSpec((n_ctx, LANES), lambda b, h: (b, 2 * hp + h))],
        out_specs=pl.BlockSpec((n_ctx, LANES), lambda b, h: (b, h)),
        out_shape=jax.ShapeDtypeStruct((t, D_NA), BF16),
        compiler_params=_params("parallel", "parallel"),
        name="ctx_attention",
    )(qkv, qkv, qkv)


def _outmlp_kernel(h_ref, yf_ref, cv_ref, at_ref, mod_ref, g_ref, gf_ref, wo_ref, w1_ref, w2_ref, o_ref, *, final_norm):
    mix = (_bdot(yf_ref[...], wo_ref[:D_FOURIER, :])
           + _bdot(cv_ref[...], wo_ref[D_FOURIER:D_FOURIER + D_CONV, :])
           + _bdot(at_ref[...], wo_ref[D_FOURIER + D_CONV:, :]))
    h1 = h_ref[...] + mod_ref[0, 2:3, :] * mix
    hn = _rms_mod(h1, g_ref[...], mod_ref[0, 3:4, :], mod_ref[0, 4:5, :]).astype(BF16)
    a = jnp.maximum(_bdot(hn, w1_ref[...]), 0.0)
    out = h1 + mod_ref[0, 5:6, :] * _bdot((a * a).astype(BF16), w2_ref[...])
    if final_norm:
        out = out * lax.rsqrt(jnp.mean(out * out, axis=-1, keepdims=True) + RMS_EPS) * gf_ref[...]
    o_ref[...] = out


def _outmlp(h, yf, cv, at, mods, g2, gf, wo, w1, w2, *, seq, tm, ctx, final_norm):
    t, d = h.shape
    nb = t // seq
    per = seq // tm
    row = (lambda i: nb) if ctx else (lambda i: i // per)
    d_ff = w1.shape[1]
    vec = pl.BlockSpec((1, d), lambda i: (0, 0))
    once = dict(pipeline_mode=pl.Buffered(1))
    return pl.pallas_call(
        functools.partial(_outmlp_kernel, final_norm=final_norm),
        grid=(t // tm,),
        in_specs=[pl.BlockSpec((tm, d), lambda i: (i, 0)),
                  pl.BlockSpec((tm, D_FOURIER), lambda i: (i % per, i // per)),
                  pl.BlockSpec((tm, D_CONV), lambda i: (i, 0)),
                  pl.BlockSpec((tm, D_NA), lambda i: (i, 0)),
                  pl.BlockSpec((1, N_MOD, d), lambda i: (row(i), 0, 0)),
                  vec, vec,
                  pl.BlockSpec((D_MIX, d), lambda i: (0, 0), **once),
                  pl.BlockSpec((d, d_ff), lambda i: (0, 0), **once),
                  pl.BlockSpec((d_ff, d), lambda i: (0, 0), **once)],
        out_specs=pl.BlockSpec((tm, d), lambda i: (i, 0)),
        out_shape=jax.ShapeDtypeStruct((t, d), F32),
        compiler_params=_params("parallel"),
        name="outmlp_ctx" if ctx else "outmlp_lat",
    )(h, yf, cv, at, mods, g2, gf, wo, w1, w2)


def kernel(x, c, ctx, c_ctx, ada_w, ada_b, norm1_g, norm2_g, w_in, w_fourier, conv_dw_w, conv_dw_b, conv_norm_g,
           conv_norm_b, conv_pw_w, conv_pw_b, na_rpb, w_out, mlp_w1, mlp_w2, final_norm_g):
    nb, seq, d = x.shape
    n_ctx = ctx.shape[1]
    depth = ada_w.shape[0]
    assert nb < MOD_ROWS and seq % GRID_W == 0 and d == D_MODEL

    cc = jnp.concatenate([c, c_ctx[None], jnp.zeros((MOD_ROWS - nb - 1, d), F32)], axis=0)
    mods = _adaln(cc, ada_w, ada_b).reshape(depth, MOD_ROWS, N_MOD, d)
    ab_lat = _fold_fourier(w_fourier, seq)
    ab_ctx = _fold_fourier(w_fourier, n_ctx)
    c_lat, s_lat = _dft_mats(seq)
    c_ctx_m, s_ctx_m = _dft_mats(n_ctx)

    w_in_b = w_in.astype(BF16)
    w_out_b = w_out.astype(BF16)
    w1_b = mlp_w1.astype(BF16)
    w2_b = mlp_w2.astype(BF16)
    pww_b = conv_pw_w.astype(BF16)
    gf = final_norm_g.reshape(1, d)

    h_lat = x.reshape(nb * seq, d)
    h_ctx = ctx.reshape(nb * n_ctx, d)
    for i in range(depth):
        last = i == depth - 1
        g1 = norm1_g[i].reshape(1, d)
        g2 = norm2_g[i].reshape(1, d)
        conv_p = (conv_dw_w[i], conv_dw_b[i].reshape(1, -1), conv_norm_g[i].reshape(1, -1),
                  conv_norm_b[i].reshape(1, -1), pww_b[i], conv_pw_b[i].reshape(1, -1))
        bias = _na_bias(na_rpb[i], seq // GRID_W)

        if last:
            kv_ctx = _kvproj(h_ctx, g1, mods[i], w_in_b[i], nb=nb, tm=n_ctx)
            kc_off, vc_off = 0, HEAD_PAIRS
        else:
            zf_c, uc_c, qkv_c = _inproj(h_ctx, g1, mods[i], w_in_b[i], ab_ctx[i], seq=n_ctx, tm=n_ctx, ctx=True)
            kv_ctx = qkv_c
            kc_off, vc_off = HEAD_PAIRS, 2 * HEAD_PAIRS

        zf, uc, qkv = _inproj(h_lat, g1, mods[i], w_in_b[i], ab_lat[i], seq=seq, tm=1024, ctx=False)
        yf = _fourier(c_lat, s_lat, zf, tm=1024)
        cv = _conv(uc, *conv_p, seq=seq)
        at = _na(qkv, kv_ctx, bias, seq=seq, n_ctx=n_ctx, kc_off=kc_off, vc_off=vc_off)
        h_lat = _outmlp(h_lat, yf, cv, at, mods[i], g2, gf, w_out_b[i], w1_b[i], w2_b[i],
                        seq=seq, tm=512, ctx=False, final_norm=last)

        if not last:
            yf_c = _fourier(c_ctx_m, s_ctx_m, zf_c, tm=n_ctx)
            cv_c = _conv(uc_c, *conv_p, seq=n_ctx)
            at_c = _ctx_attn(qkv_c, n_ctx=n_ctx)
            h_ctx = _outmlp(h_ctx, yf_c, cv_c, at_c, mods[i], g2, gf, w_out_b[i], w1_b[i], w2_b[i],
                            seq=n_ctx, tm=n_ctx, ctx=True, final_norm=False)
    return h_lat.reshape(nb, seq, d)
```

```python
import functools

import numpy as np
import jax
import jax.numpy as jnp
from jax import lax
from jax.experimental import pallas as pl
from jax.experimental.pallas import tpu as pltpu

D_MODEL = 1024
GRID_W = 64
D_FOURIER = 256
FOURIER_GROUPS = 4
D_CONV = 256
CONV_WIDTH = 31
N_NA_HEADS = 8
NA_HEAD_DIM = 64
D_NA = N_NA_HEADS * NA_HEAD_DIM
WIN_ROWS = 8
WIN_COLS = 16
QKV_START = D_FOURIER + 2 * D_CONV
KV_START = QKV_START + D_NA
D_MIX = D_FOURIER + D_CONV + D_NA
N_MOD = 6
RMS_EPS = 1e-6
LN_EPS = 1e-5

LANES = 128
SUBLANES = 8
HEAD_PAIRS = N_NA_HEADS * NA_HEAD_DIM // LANES
MOD_ROWS = 16
VMEM_LIMIT = 56 * 1024 * 1024

F32 = jnp.float32
BF16 = jnp.bfloat16


def _params(*sem):
    return pltpu.CompilerParams(dimension_semantics=sem, vmem_limit_bytes=VMEM_LIMIT)


def _sigmoid(x):
    return 1.0 / (1.0 + jnp.exp(-x))


def _rms_mod(x, g, shift, scale):
    y = x * lax.rsqrt(jnp.mean(x * x, axis=-1, keepdims=True) + RMS_EPS) * g
    return y * (1.0 + scale) + shift


def _bdot(a, b):
    return jnp.dot(a, b, preferred_element_type=F32)


def _adaln_kernel(c_ref, w_ref, b_ref, o_ref):
    cc = c_ref[...]
    a = cc * _sigmoid(cc)
    o_ref[0] = jnp.dot(a, w_ref[0], preferred_element_type=F32, precision=lax.Precision.HIGHEST) + b_ref[0]


def _adaln(cc, ada_w, ada_b):
    depth, d, n = ada_w.shape
    tn = 1024
    return pl.pallas_call(
        _adaln_kernel,
        grid=(depth, n // tn),
        in_specs=[pl.BlockSpec((MOD_ROWS, d), lambda l, j: (0, 0)),
                  pl.BlockSpec((1, d, tn), lambda l, j: (l, 0, j)),
                  pl.BlockSpec((1, 1, tn), lambda l, j: (l, 0, j))],
        out_specs=pl.BlockSpec((1, MOD_ROWS, tn), lambda l, j: (l, 0, j)),
        out_shape=jax.ShapeDtypeStruct((depth, MOD_ROWS, n), F32),
        compiler_params=_params("parallel", "parallel"),
        name="adaln",
    )(cc, ada_w, ada_b.reshape(depth, 1, n))


def _fold_kernel(cc_ref, sc_ref, w_ref, o_ref):
    w = w_ref[0]
    hi = lax.Precision.HIGHEST
    o_ref[0, :, :D_FOURIER] = jnp.dot(cc_ref[...], w, preferred_element_type=F32, precision=hi).astype(BF16)
    o_ref[0, :, D_FOURIER:] = jnp.dot(sc_ref[...], w, preferred_element_type=F32, precision=hi).astype(BF16)


def _fold_fourier(w_fourier, seq):
    depth = w_fourier.shape[0]
    gs = D_FOURIER // FOURIER_GROUPS
    idx = np.arange(D_FOURIER)
    same = (idx[:, None] // gs) == (idx[None, :] // gs)
    ang = 2.0 * np.pi * (((idx[:, None] % gs) * (idx[None, :] % gs)) % gs) / gs
    scale = 1.0 / np.sqrt(seq * gs)
    cc = jnp.asarray(np.where(same, np.cos(ang), 0.0) * scale, F32)
    sc = jnp.asarray(np.where(same, np.sin(ang), 0.0) * scale, F32)
    full = pl.BlockSpec((D_FOURIER, D_FOURIER), lambda l: (0, 0))
    return pl.pallas_call(
        _fold_kernel,
        grid=(depth,),
        in_specs=[full, full, pl.BlockSpec((1, D_FOURIER, D_FOURIER), lambda l: (l, 0, 0))],
        out_specs=pl.BlockSpec((1, D_FOURIER, 2 * D_FOURIER), lambda l: (l, 0, 0)),
        out_shape=jax.ShapeDtypeStruct((depth, D_FOURIER, 2 * D_FOURIER), BF16),
        compiler_params=_params("parallel"),
        name="fold_fourier",
    )(cc, sc, w_fourier)


def _dft_mats(seq):
    k = np.arange(seq)
    ang = 2.0 * np.pi * ((k[:, None] * k[None, :]) % seq) / seq
    return jnp.asarray(np.cos(ang), F32).astype(BF16), jnp.asarray(np.sin(ang), F32).astype(BF16)


def _inproj_kernel(x_ref, g_ref, mod_ref, w_ref, ab_ref, zf_ref, uc_ref, qkv_ref):
    hn = _rms_mod(x_ref[...], g_ref[...], mod_ref[0, 0:1, :], mod_ref[0, 1:2, :]).astype(BF16)
    uf = _bdot(hn, w_ref[:, :D_FOURIER])
    zf_ref[...] = _bdot(uf.astype(BF16), ab_ref[...]).astype(BF16)
    uc_ref[...] = _bdot(hn, w_ref[:, D_FOURIER:QKV_START])
    qkv_ref[:, :D_NA] = (_bdot(hn, w_ref[:, QKV_START:KV_START]) * (NA_HEAD_DIM ** -0.5)).astype(BF16)
    qkv_ref[:, D_NA:] = _bdot(hn, w_ref[:, KV_START:]).astype(BF16)


def _inproj(h, g, mods, w_in, ab, *, seq, tm, ctx):
    t, d = h.shape
    nb = t // seq
    per = seq // tm
    row = (lambda i: nb) if ctx else (lambda i: i // per)
    d_in = w_in.shape[1]
    return pl.pallas_call(
        _inproj_kernel,
        grid=(t // tm,),
        in_specs=[pl.BlockSpec((tm, d), lambda i: (i, 0)),
                  pl.BlockSpec((1, d), lambda i: (0, 0)),
                  pl.BlockSpec((1, N_MOD, d), lambda i: (row(i), 0, 0)),
                  pl.BlockSpec((d, d_in), lambda i: (0, 0), pipeline_mode=pl.Buffered(1)),
                  pl.BlockSpec((D_FOURIER, 2 * D_FOURIER), lambda i: (0, 0))],
        out_specs=[pl.BlockSpec((tm, 2 * D_FOURIER), lambda i: (i % per, i // per)),
                   pl.BlockSpec((tm, 2 * D_CONV), lambda i: (i, 0)),
                   pl.BlockSpec((tm, 3 * D_NA), lambda i: (i, 0))],
        out_shape=[jax.ShapeDtypeStruct((seq, nb * 2 * D_FOURIER), BF16),
                   jax.ShapeDtypeStruct((t, 2 * D_CONV), F32),
                   jax.ShapeDtypeStruct((t, 3 * D_NA), BF16)],
        compiler_params=_params("parallel"),
        name="inproj_ctx" if ctx else "inproj_lat",
    )(h, g, mods, w_in, ab)


def _kvproj_kernel(x_ref, g_ref, mod_ref, w_ref, kv_ref):
    hn = _rms_mod(x_ref[...], g_ref[...], mod_ref[0, 0:1, :], mod_ref[0, 1:2, :]).astype(BF16)
    kv_ref[...] = _bdot(hn, w_ref[:, KV_START:]).astype(BF16)


def _kvproj(h, g, mods, w_in, *, nb, tm):
    t, d = h.shape
    d_in = w_in.shape[1]
    n = d_in - KV_START
    return pl.pallas_call(
        _kvproj_kernel,
        grid=(t // tm,),
        in_specs=[pl.BlockSpec((tm, d), lambda i: (i, 0)),
                  pl.BlockSpec((1, d), lambda i: (0, 0)),
                  pl.BlockSpec((1, N_MOD, d), lambda i: (nb, 0, 0)),
                  pl.BlockSpec((d, d_in), lambda i: (0, 0))],
        out_specs=pl.BlockSpec((tm, n), lambda i: (i, 0)),
        out_shape=jax.ShapeDtypeStruct((t, n), BF16),
        compiler_params=_params("parallel"),
        name="kvproj_ctx",
    )(h, g, mods, w_in)


def _fourier_kernel(c_ref, s_ref, z_ref, y_ref):
    y = _bdot(c_ref[...], z_ref[:, :D_FOURIER]) - _bdot(s_ref[...], z_ref[:, D_FOURIER:])
    y_ref[...] = y.astype(BF16)


def _fourier(cmat, smat, zf, *, tm):
    seq = cmat.shape[0]
    nb = zf.shape[1] // (2 * D_FOURIER)
    return pl.pallas_call(
        _fourier_kernel,
        grid=(seq // tm, nb),
        in_specs=[pl.BlockSpec((tm, seq), lambda i, b: (i, 0)),
                  pl.BlockSpec((tm, seq), lambda i, b: (i, 0)),
                  pl.BlockSpec((seq, 2 * D_FOURIER), lambda i, b: (0, b))],
        out_specs=pl.BlockSpec((tm, D_FOURIER), lambda i, b: (i, b)),
        out_shape=jax.ShapeDtypeStruct((seq, nb * D_FOURIER), BF16),
        compiler_params=_params("parallel", "parallel"),
        name="fourier_pos",
    )(cmat, smat, zf)


CONV_PAD = 16
CONV_CHUNK = 128


def _conv_kernel(u_ref, dww_ref, dwb_ref, lng_ref, lnb_ref, pww_ref, pwb_ref, o_ref, vs_ref):
    seq = u_ref.shape[0]
    n_pad = seq + 2 * CONV_PAD
    zeros = jnp.zeros((CONV_PAD, D_CONV), F32)
    vs_ref[0, 0:CONV_PAD, :] = zeros
    vs_ref[0, seq + CONV_PAD:n_pad, :] = zeros
    vs_ref[0, CONV_PAD:CONV_PAD + seq, :] = u_ref[:, :D_CONV] * _sigmoid(u_ref[:, D_CONV:])
    for s in range(1, SUBLANES):
        vs_ref[s, 0:n_pad - SUBLANES, :] = vs_ref[0, s:s + n_pad - SUBLANES, :]
    first = CONV_PAD - CONV_WIDTH // 2
    assert (first + CONV_WIDTH - 1) // SUBLANES * SUBLANES + seq <= n_pad - SUBLANES

    def chunk(ci, carry):
        base = pl.multiple_of(ci * CONV_CHUNK, CONV_CHUNK)
        acc = jnp.zeros((CONV_CHUNK, D_CONV), F32) + dwb_ref[...]
        for t in range(CONV_WIDTH):
            s, a = (first + t) % SUBLANES, (first + t) // SUBLANES
            acc = acc + vs_ref[s, pl.ds(base + a * SUBLANES, CONV_CHUNK), :] * dww_ref[t:t + 1, :]
        mu = jnp.mean(acc, axis=-1, keepdims=True)
        cen = acc - mu
        var = jnp.mean(cen * cen, axis=-1, keepdims=True)
        y = cen * lax.rsqrt(var + LN_EPS) * lng_ref[...] + lnb_ref[...]
        y = y * _sigmoid(y)
        o_ref[pl.ds(base, CONV_CHUNK), :] = (_bdot(y.astype(BF16), pww_ref[...]) + pwb_ref[...]).astype(BF16)
        return carry

    lax.fori_loop(0, seq // CONV_CHUNK, chunk, 0, unroll=2)


def _conv(uc, dww, dwb, lng, lnb, pww, pwb, *, seq):
    t = uc.shape[0]
    vec = pl.BlockSpec((1, D_CONV), lambda b: (0, 0))
    return pl.pallas_call(
        _conv_kernel,
        grid=(t // seq,),
        in_specs=[pl.BlockSpec((seq, 2 * D_CONV), lambda b: (b, 0)),
                  pl.BlockSpec((CONV_WIDTH, D_CONV), lambda b: (0, 0)),
                  vec, vec, vec,
                  pl.BlockSpec((D_CONV, D_CONV), lambda b: (0, 0)),
                  vec],
        out_specs=pl.BlockSpec((seq, D_CONV), lambda b: (b, 0)),
        out_shape=jax.ShapeDtypeStruct((t, D_CONV), BF16),
        scratch_shapes=[pltpu.VMEM((SUBLANES, seq + 2 * CONV_PAD, D_CONV), F32)],
        compiler_params=_params("parallel"),
        name="conv_module",
    )(uc, dww, dwb, lng, lnb, pww, pwb)


def _head_mask():
    return lax.broadcasted_iota(jnp.int32, (1, LANES), 1) < NA_HEAD_DIM


Q_ROWS = 2
BAND_ROWS = 10
assert BAND_ROWS >= WIN_ROWS + Q_ROWS - 1 and (BAND_ROWS * GRID_W) % LANES == 0


def _na_patterns(rows):
    starts, sigs = [], []
    for g in range(rows // Q_ROWS):
        start = int(np.clip(Q_ROWS * g - WIN_ROWS // 2, 0, rows - BAND_ROWS))
        r = Q_ROWS * g + np.arange(Q_ROWS)
        rs = np.clip(r - WIN_ROWS // 2, 0, rows - WIN_ROWS)
        starts.append(start)
        sigs.append((start - Q_ROWS * g,) + tuple(rs - r))
    run_starts = [g for g in range(len(sigs)) if g == 0 or sigs[g] != sigs[g - 1]]
    assert len(set(sigs)) == len(run_starts)
    return starts, run_starts


def _na_kernel(q_ref, k_ref, v_ref, kc_ref, vc_ref, pair_ref, o_ref, s_ref, p_ref, rden_ref, bias_ref):
    rows = q_ref.shape[0] // GRID_W
    n_groups = rows // Q_ROWS
    n_q = Q_ROWS * GRID_W
    n_loc = BAND_ROWS * GRID_W
    _, run_starts = _na_patterns(rows)
    first = _head_mask()
    dn = (((1,), (1,)), ((), ()))
    assert n_groups % 2 == 0 and n_groups >= 4

    def band(g):
        start = jnp.clip(Q_ROWS * g - WIN_ROWS // 2, 0, rows - BAND_ROWS)
        return pl.multiple_of(start * GRID_W, LANES)

    def q_start(g):
        return pl.multiple_of(jnp.asarray(g, jnp.int32) * n_q, n_q)

    def scores(g, slot):
        pat = sum(jnp.asarray(g >= s, jnp.int32) for s in run_starts[1:])
        q = q_ref[pl.ds(q_start(g), n_q), :]
        kb = k_ref[pl.ds(band(g), n_loc), :]
        for a in range(2):
            qa = jnp.where(first if a == 0 else ~first, q, jnp.zeros_like(q))
            s_ref[slot, a, :, :n_loc] = lax.dot_general(qa, kb, dn, preferred_element_type=F32) + bias_ref[a, pat]
            s_ref[slot, a, :, n_loc:] = lax.dot_general(qa, kc_ref[...], dn, preferred_element_type=F32)

    def softmax(slot):
        for a in range(2):
            s = s_ref[slot, a]
            p = jnp.exp(s - jnp.max(s, axis=-1, keepdims=True))
            p_ref[slot, a] = p.astype(BF16)
            rden_ref[slot, a] = jnp.broadcast_to(1.0 / jnp.sum(p, axis=-1, keepdims=True), (n_q, LANES))

    def values(g, slot):
        vb = v_ref[pl.ds(band(g), n_loc), :]
        outs = []
        for a in range(2):
            o = _bdot(p_ref[slot, a, :, :n_loc], vb) + _bdot(p_ref[slot, a, :, n_loc:], vc_ref[...])
            outs.append(o * rden_ref[slot, a])
        o_ref[pl.ds(q_start(g), n_q), :] = jnp.where(first, outs[0], outs[1]).astype(BF16)

    @pl.when(pl.program_id(1) == 0)
    def _():
        _na_assemble_bias(pair_ref, bias_ref, rows)

    scores(0, 0)
    scores(1, 1)
    softmax(0)

    def pair(j, carry):
        g = 2 * j + 1
        scores(g + 1, 0)
        softmax(1)
        values(g - 1, 0)
        scores(g + 2, 1)
        softmax(0)
        values(g, 1)
        return carry

    lax.fori_loop(0, n_groups // 2 - 1, pair, 0)
    softmax(1)
    values(n_groups - 2, 0)
    values(n_groups - 1, 1)


def _na_bias(rpb):
    n_heads, n_dr, n_dc = rpb.shape
    lead = GRID_W - WIN_COLS
    wpad = jnp.pad(rpb.astype(F32), ((0, 0), (0, 0), (lead, 2 * GRID_W - n_dc - lead)))
    flat = jnp.tile(wpad, (1, 1, GRID_W))[..., :GRID_W * (2 * GRID_W - 1)]
    toep = flat.reshape(n_heads, n_dr, GRID_W, 2 * GRID_W - 1)[..., GRID_W - 1:]
    col = np.arange(GRID_W)
    col_start = np.clip(col - WIN_COLS // 2, 0, GRID_W - WIN_COLS)
    col_valid = (col[None, :] >= col_start[:, None]) & (col[None, :] < col_start[:, None] + WIN_COLS)
    assert np.all(np.abs(col[None, :] - col[:, None])[col_valid] <= WIN_COLS - 1)
    toep = jnp.where(jnp.asarray(col_valid), toep, -jnp.inf)
    ext = jnp.pad(toep, ((0, 0), (1, 1), (0, 0), (0, 0)), constant_values=-jnp.inf)
    pairs = jnp.concatenate([ext[:, :-1], ext[:, 1:]], axis=-1)
    return pairs.reshape(HEAD_PAIRS, 2, n_dr + 1, GRID_W, 2 * GRID_W)


def _na_assemble_bias(pair_ref, bias_ref, rows):
    starts, run_starts = _na_patterns(rows)
    low = lax.broadcasted_iota(jnp.int32, (GRID_W, LANES), 1) < GRID_W
    masked = jnp.full((GRID_W, LANES), -jnp.inf, F32)
    for a in range(2):
        for p, g in enumerate(run_starts):
            for qi in range(Q_ROWS):
                r = Q_ROWS * g + qi
                rs = int(np.clip(r - WIN_ROWS // 2, 0, rows - WIN_ROWS))
                below = rs - starts[g]
                dr0 = rs - r + WIN_ROWS - 1
                for t in range(BAND_ROWS // 2):
                    d0 = dr0 + 2 * t - below
                    v0 = below <= 2 * t < below + WIN_ROWS
                    v1 = below <= 2 * t + 1 < below + WIN_ROWS
                    if v0 and v1:
                        tile = pair_ref[0, a, d0 + 1]
                    elif v0:
                        tile = jnp.where(low, pair_ref[0, a, d0 + 1], masked)
                    elif v1:
                        tile = jnp.where(low, masked, pair_ref[0, a, d0 + 1])
                    else:
                        tile = masked
                    bias_ref[a, p, qi * GRID_W:(qi + 1) * GRID_W, t * LANES:(t + 1) * LANES] = tile


def _na(qkv, kv_ctx, pairs, *, seq, n_ctx, kc_off, vc_off):
    t = qkv.shape[0]
    nb = t // seq
    hp = HEAD_PAIRS
    n_pat = len(_na_patterns(seq // GRID_W)[1])
    return pl.pallas_call(
        _na_kernel,
        grid=(hp, nb),
        in_specs=[pl.BlockSpec((seq, LANES), lambda h, b: (b, h)),
                  pl.BlockSpec((seq, LANES), lambda h, b: (b, hp + h)),
                  pl.BlockSpec((seq, LANES), lambda h, b: (b, 2 * hp + h)),
                  pl.BlockSpec((n_ctx, LANES), lambda h, b: (b, kc_off + h)),
                  pl.BlockSpec((n_ctx, LANES), lambda h, b: (b, vc_off + h)),
                  pl.BlockSpec((1,) + pairs.shape[1:], lambda h, b: (h, 0, 0, 0, 0))],
        out_specs=pl.BlockSpec((seq, LANES), lambda h, b: (b, h)),
        out_shape=jax.ShapeDtypeStruct((t, D_NA), BF16),
        scratch_shapes=[pltpu.VMEM((2, 2, Q_ROWS * GRID_W, BAND_ROWS * GRID_W + n_ctx), F32),
                        pltpu.VMEM((2, 2, Q_ROWS * GRID_W, BAND_ROWS * GRID_W + n_ctx), BF16),
                        pltpu.VMEM((2, 2, Q_ROWS * GRID_W, LANES), F32),
                        pltpu.VMEM((2, n_pat, Q_ROWS * GRID_W, BAND_ROWS * GRID_W), F32)],
        compiler_params=_params("parallel", "arbitrary"),
        name="na_attention",
    )(qkv, qkv, qkv, kv_ctx, kv_ctx, pairs)


def _ctx_attn_kernel(q_ref, k_ref, v_ref, o_ref):
    first = _head_mask()
    q = q_ref[...]
    k = k_ref[...]
    v = v_ref[...]
    outs = []
    for a in range(2):
        qa = jnp.where(first if a == 0 else ~first, q, jnp.zeros_like(q))
        s = lax.dot_general(qa, k, (((1,), (1,)), ((), ())), preferred_element_type=F32)
        p = jnp.exp(s - jnp.max(s, axis=-1, keepdims=True))
        den = jnp.sum(p, axis=-1, keepdims=True)
        outs.append(_bdot(p.astype(BF16), v) / den)
    o_ref[...] = jnp.where(first, outs[0], outs[1]).astype(BF16)


def _ctx_attn(qkv, *, n_ctx):
    t = qkv.shape[0]
    hp = HEAD_PAIRS
    return pl.pallas_call(
        _ctx_attn_kernel,
        grid=(t // n_ctx, hp),
        in_specs=[pl.BlockSpec((n_ctx, LANES), lambda b, h: (b, h)),
                  pl.BlockSpec((n_ctx, LANES), lambda b, h: (b, hp + h)),
                  pl.BlockSpec((n_ctx, LANES), lambda b, h: (b, 2 * hp + h))],
        out_specs=pl.BlockSpec((n_ctx, LANES), lambda b, h: (b, h)),
        out_shape=jax.ShapeDtypeStruct((t, D_NA), BF16),
        compiler_params=_params("parallel", "parallel"),
        name="ctx_attention",
    )(qkv, qkv, qkv)


def _outmlp_kernel(h_ref, yf_ref, cv_ref, at_ref, mod_ref, g_ref, gf_ref, wo_ref, w1_ref, w2_ref, o_ref, *, final_norm):
    mix = (_bdot(yf_ref[...], wo_ref[:D_FOURIER, :])
           + _bdot(cv_ref[...], wo_ref[D_FOURIER:D_FOURIER + D_CONV, :])
           + _bdot(at_ref[...], wo_ref[D_FOURIER + D_CONV:, :]))
    h1 = h_ref[...] + mod_ref[0, 2:3, :] * mix
    hn = _rms_mod(h1, g_ref[...], mod_ref[0, 3:4, :], mod_ref[0, 4:5, :]).astype(BF16)
    a = jnp.maximum(_bdot(hn, w1_ref[...]), 0.0)
    out = h1 + mod_ref[0, 5:6, :] * _bdot((a * a).astype(BF16), w2_ref[...])
    if final_norm:
        out = out * lax.rsqrt(jnp.mean(out * out, axis=-1, keepdims=True) + RMS_EPS) * gf_ref[...]
    o_ref[...] = out


def _outmlp(h, yf, cv, at, mods, g2, gf, wo, w1, w2, *, seq, tm, ctx, final_norm):
    t, d = h.shape
    nb = t // seq
    per = seq // tm
    row = (lambda i: nb) if ctx else (lambda i: i // per)
    d_ff = w1.shape[1]
    vec = pl.BlockSpec((1, d), lambda i: (0, 0))
    once = dict(pipeline_mode=pl.Buffered(1))
    return pl.pallas_call(
        functools.partial(_outmlp_kernel, final_norm=final_norm),
        grid=(t // tm,),
        in_specs=[pl.BlockSpec((tm, d), lambda i: (i, 0)),
                  pl.BlockSpec((tm, D_FOURIER), lambda i: (i % per, i // per)),
                  pl.BlockSpec((tm, D_CONV), lambda i: (i, 0)),
                  pl.BlockSpec((tm, D_NA), lambda i: (i, 0)),
                  pl.BlockSpec((1, N_MOD, d), lambda i: (row(i), 0, 0)),
                  vec, vec,
                  pl.BlockSpec((D_MIX, d), lambda i: (0, 0), **once),
                  pl.BlockSpec((d, d_ff), lambda i: (0, 0), **once),
                  pl.BlockSpec((d_ff, d), lambda i: (0, 0), **once)],
        out_specs=pl.BlockSpec((tm, d), lambda i: (i, 0)),
        out_shape=jax.ShapeDtypeStruct((t, d), F32),
        compiler_params=_params("parallel"),
        name="outmlp_ctx" if ctx else "outmlp_lat",
    )(h, yf, cv, at, mods, g2, gf, wo, w1, w2)


def kernel(x, c, ctx, c_ctx, ada_w, ada_b, norm1_g, norm2_g, w_in, w_fourier, conv_dw_w, conv_dw_b, conv_norm_g,
           conv_norm_b, conv_pw_w, conv_pw_b, na_rpb, w_out, mlp_w1, mlp_w2, final_norm_g):
    nb, seq, d = x.shape
    n_ctx = ctx.shape[1]
    depth = ada_w.shape[0]
    assert nb < MOD_ROWS and seq % GRID_W == 0 and d == D_MODEL

    cc = jnp.concatenate([c, c_ctx[None], jnp.zeros((MOD_ROWS - nb - 1, d), F32)], axis=0)
    mods = _adaln(cc, ada_w, ada_b).reshape(depth, MOD_ROWS, N_MOD, d)
    ab_lat = _fold_fourier(w_fourier, seq)
    ab_ctx = _fold_fourier(w_fourier, n_ctx)
    c_lat, s_lat = _dft_mats(seq)
    c_ctx_m, s_ctx_m = _dft_mats(n_ctx)

    w_in_b = w_in.astype(BF16)
    w_out_b = w_out.astype(BF16)
    w1_b = mlp_w1.astype(BF16)
    w2_b = mlp_w2.astype(BF16)
    pww_b = conv_pw_w.astype(BF16)
    gf = final_norm_g.reshape(1, d)

    h_lat = x.reshape(nb * seq, d)
    h_ctx = ctx.reshape(nb * n_ctx, d)
    for i in range(depth):
        last = i == depth - 1
        g1 = norm1_g[i].reshape(1, d)
        g2 = norm2_g[i].reshape(1, d)
        conv_p = (conv_dw_w[i], conv_dw_b[i].reshape(1, -1), conv_norm_g[i].reshape(1, -1),
                  conv_norm_b[i].reshape(1, -1), pww_b[i], conv_pw_b[i].reshape(1, -1))
        bias = _na_bias(na_rpb[i])

        if last:
            kv_ctx = _kvproj(h_ctx, g1, mods[i], w_in_b[i], nb=nb, tm=n_ctx)
            kc_off, vc_off = 0, HEAD_PAIRS
        else:
            zf_c, uc_c, qkv_c = _inproj(h_ctx, g1, mods[i], w_in_b[i], ab_ctx[i], seq=n_ctx, tm=n_ctx, ctx=True)
            kv_ctx = qkv_c
            kc_off, vc_off = HEAD_PAIRS, 2 * HEAD_PAIRS

        zf, uc, qkv = _inproj(h_lat, g1, mods[i], w_in_b[i], ab_lat[i], seq=seq, tm=1024, ctx=False)
        yf = _fourier(c_lat, s_lat, zf, tm=1024)
        cv = _conv(uc, *conv_p, seq=seq)
        at = _na(qkv, kv_ctx, bias, seq=seq, n_ctx=n_ctx, kc_off=kc_off, vc_off=vc_off)
        h_lat = _outmlp(h_lat, yf, cv, at, mods[i], g2, gf, w_out_b[i], w1_b[i], w2_b[i],
                        seq=seq, tm=512, ctx=False, final_norm=last)

        if not last:
            yf_c = _fourier(c_ctx_m, s_ctx_m, zf_c, tm=n_ctx)
            cv_c = _conv(uc_c, *conv_p, seq=n_ctx)
            at_c = _ctx_attn(qkv_c, n_ctx=n_ctx)
            h_ctx = _outmlp(h_ctx, yf_c, cv_c, at_c, mods[i], g2, gf, w_out_b[i], w1_b[i], w2_b[i],
                            seq=n_ctx, tm=n_ctx, ctx=True, final_norm=False)
    return h_lat.reshape(nb, seq, d)
```

```python
import functools

import numpy as np
import jax
import jax.numpy as jnp
from jax import lax
from jax.experimental import pallas as pl
from jax.experimental.pallas import tpu as pltpu

D_MODEL = 1024
GRID_W = 64
D_FOURIER = 256
FOURIER_GROUPS = 4
D_CONV = 256
CONV_WIDTH = 31
N_NA_HEADS = 8
NA_HEAD_DIM = 64
D_NA = N_NA_HEADS * NA_HEAD_DIM
WIN_ROWS = 8
WIN_COLS = 16
QKV_START = D_FOURIER + 2 * D_CONV
KV_START = QKV_START + D_NA
D_MIX = D_FOURIER + D_CONV + D_NA
N_MOD = 6
RMS_EPS = 1e-6
LN_EPS = 1e-5

LANES = 128
SUBLANES = 8
HEAD_PAIRS = N_NA_HEADS * NA_HEAD_DIM // LANES
MOD_ROWS = 16
VMEM_LIMIT = 56 * 1024 * 1024

F32 = jnp.float32
BF16 = jnp.bfloat16


def _params(*sem):
    return pltpu.CompilerParams(dimension_semantics=sem, vmem_limit_bytes=VMEM_LIMIT)


def _sigmoid(x):
    return 1.0 / (1.0 + jnp.exp(-x))


def _rms_mod(x, g, shift, scale):
    y = x * lax.rsqrt(jnp.mean(x * x, axis=-1, keepdims=True) + RMS_EPS) * g
    return y * (1.0 + scale) + shift


def _bdot(a, b):
    return jnp.dot(a, b, preferred_element_type=F32)


def _adaln_kernel(c_ref, w_ref, b_ref, o_ref):
    cc = c_ref[...]
    a = cc * _sigmoid(cc)
    o_ref[0] = jnp.dot(a, w_ref[0], preferred_element_type=F32, precision=lax.Precision.HIGHEST) + b_ref[0]


def _adaln(cc, ada_w, ada_b):
    depth, d, n = ada_w.shape
    tn = 1024
    return pl.pallas_call(
        _adaln_kernel,
        grid=(depth, n // tn),
        in_specs=[pl.BlockSpec((MOD_ROWS, d), lambda l, j: (0, 0)),
                  pl.BlockSpec((1, d, tn), lambda l, j: (l, 0, j)),
                  pl.BlockSpec((1, 1, tn), lambda l, j: (l, 0, j))],
        out_specs=pl.BlockSpec((1, MOD_ROWS, tn), lambda l, j: (l, 0, j)),
        out_shape=jax.ShapeDtypeStruct((depth, MOD_ROWS, n), F32),
        compiler_params=_params("parallel", "parallel"),
        name="adaln",
    )(cc, ada_w, ada_b.reshape(depth, 1, n))


def _fold_kernel(cc_ref, sc_ref, w_ref, o_ref):
    w = w_ref[0]
    hi = lax.Precision.HIGHEST
    o_ref[0, :, :D_FOURIER] = jnp.dot(cc_ref[...], w, preferred_element_type=F32, precision=hi).astype(BF16)
    o_ref[0, :, D_FOURIER:] = jnp.dot(sc_ref[...], w, preferred_element_type=F32, precision=hi).astype(BF16)


def _fold_fourier(w_fourier, seq):
    depth = w_fourier.shape[0]
    gs = D_FOURIER // FOURIER_GROUPS
    idx = np.arange(D_FOURIER)
    same = (idx[:, None] // gs) == (idx[None, :] // gs)
    ang = 2.0 * np.pi * (((idx[:, None] % gs) * (idx[None, :] % gs)) % gs) / gs
    scale = 1.0 / np.sqrt(seq * gs)
    cc = jnp.asarray(np.where(same, np.cos(ang), 0.0) * scale, F32)
    sc = jnp.asarray(np.where(same, np.sin(ang), 0.0) * scale, F32)
    full = pl.BlockSpec((D_FOURIER, D_FOURIER), lambda l: (0, 0))
    return pl.pallas_call(
        _fold_kernel,
        grid=(depth,),
        in_specs=[full, full, pl.BlockSpec((1, D_FOURIER, D_FOURIER), lambda l: (l, 0, 0))],
        out_specs=pl.BlockSpec((1, D_FOURIER, 2 * D_FOURIER), lambda l: (l, 0, 0)),
        out_shape=jax.ShapeDtypeStruct((depth, D_FOURIER, 2 * D_FOURIER), BF16),
        compiler_params=_params("parallel"),
        name="fold_fourier",
    )(cc, sc, w_fourier)


def _dft_mats(seq):
    k = np.arange(seq)
    ang = 2.0 * np.pi * ((k[:, None] * k[None, :]) % seq) / seq
    return jnp.asarray(np.cos(ang), F32).astype(BF16), jnp.asarray(np.sin(ang), F32).astype(BF16)


def _inproj_kernel(x_ref, g_ref, mod_ref, w_ref, ab_ref, zf_ref, uc_ref, qkv_ref):
    hn = _rms_mod(x_ref[...], g_ref[...], mod_ref[0, 0:1, :], mod_ref[0, 1:2, :]).astype(BF16)
    uf = _bdot(hn, w_ref[:, :D_FOURIER])
    zf_ref[...] = _bdot(uf.astype(BF16), ab_ref[...]).astype(BF16)
    uc_ref[...] = _bdot(hn, w_ref[:, D_FOURIER:QKV_START])
    qkv_ref[:, :D_NA] = (_bdot(hn, w_ref[:, QKV_START:KV_START]) * (NA_HEAD_DIM ** -0.5)).astype(BF16)
    qkv_ref[:, D_NA:] = _bdot(hn, w_ref[:, KV_START:]).astype(BF16)


def _inproj(h, g, mods, w_in, ab, *, seq, tm, ctx):
    t, d = h.shape
    nb = t // seq
    per = seq // tm
    row = (lambda i: nb) if ctx else (lambda i: i // per)
    d_in = w_in.shape[1]
    return pl.pallas_call(
        _inproj_kernel,
        grid=(t // tm,),
        in_specs=[pl.BlockSpec((tm, d), lambda i: (i, 0)),
                  pl.BlockSpec((1, d), lambda i: (0, 0)),
                  pl.BlockSpec((1, N_MOD, d), lambda i: (row(i), 0, 0)),
                  pl.BlockSpec((d, d_in), lambda i: (0, 0), pipeline_mode=pl.Buffered(1)),
                  pl.BlockSpec((D_FOURIER, 2 * D_FOURIER), lambda i: (0, 0))],
        out_specs=[pl.BlockSpec((tm, 2 * D_FOURIER), lambda i: (i % per, i // per)),
                   pl.BlockSpec((tm, 2 * D_CONV), lambda i: (i, 0)),
                   pl.BlockSpec((tm, 3 * D_NA), lambda i: (i, 0))],
        out_shape=[jax.ShapeDtypeStruct((seq, nb * 2 * D_FOURIER), BF16),
                   jax.ShapeDtypeStruct((t, 2 * D_CONV), F32),
                   jax.ShapeDtypeStruct((t, 3 * D_NA), BF16)],
        compiler_params=_params("parallel"),
        name="inproj_ctx" if ctx else "inproj_lat",
    )(h, g, mods, w_in, ab)


def _kvproj_kernel(x_ref, g_ref, mod_ref, w_ref, kv_ref):
    hn = _rms_mod(x_ref[...], g_ref[...], mod_ref[0, 0:1, :], mod_ref[0, 1:2, :]).astype(BF16)
    kv_ref[...] = _bdot(hn, w_ref[:, KV_START:]).astype(BF16)


def _kvproj(h, g, mods, w_in, *, nb, tm):
    t, d = h.shape
    d_in = w_in.shape[1]
    n = d_in - KV_START
    return pl.pallas_call(
        _kvproj_kernel,
        grid=(t // tm,),
        in_specs=[pl.BlockSpec((tm, d), lambda i: (i, 0)),
                  pl.BlockSpec((1, d), lambda i: (0, 0)),
                  pl.BlockSpec((1, N_MOD, d), lambda i: (nb, 0, 0)),
                  pl.BlockSpec((d, d_in), lambda i: (0, 0))],
        out_specs=pl.BlockSpec((tm, n), lambda i: (i, 0)),
        out_shape=jax.ShapeDtypeStruct((t, n), BF16),
        compiler_params=_params("parallel"),
        name="kvproj_ctx",
    )(h, g, mods, w_in)


def _fourier_kernel(c_ref, s_ref, z_ref, y_ref):
    y = _bdot(c_ref[...], z_ref[:, :D_FOURIER]) - _bdot(s_ref[...], z_ref[:, D_FOURIER:])
    y_ref[...] = y.astype(BF16)


def _fourier(cmat, smat, zf, *, tm):
    seq = cmat.shape[0]
    nb = zf.shape[1] // (2 * D_FOURIER)
    return pl.pallas_call(
        _fourier_kernel,
        grid=(seq // tm, nb),
        in_specs=[pl.BlockSpec((tm, seq), lambda i, b: (i, 0)),
                  pl.BlockSpec((tm, seq), lambda i, b: (i, 0)),
                  pl.BlockSpec((seq, 2 * D_FOURIER), lambda i, b: (0, b))],
        out_specs=pl.BlockSpec((tm, D_FOURIER), lambda i, b: (i, b)),
        out_shape=jax.ShapeDtypeStruct((seq, nb * D_FOURIER), BF16),
        compiler_params=_params("parallel", "parallel"),
        name="fourier_pos",
    )(cmat, smat, zf)


CONV_PAD = 16
CONV_CHUNK = 128


def _conv_kernel(u_ref, dww_ref, dwb_ref, lng_ref, lnb_ref, pww_ref, pwb_ref, o_ref, vs_ref):
    seq = u_ref.shape[0]
    n_pad = seq + 2 * CONV_PAD
    zeros = jnp.zeros((CONV_PAD, D_CONV), F32)
    vs_ref[0, 0:CONV_PAD, :] = zeros
    vs_ref[0, seq + CONV_PAD:n_pad, :] = zeros
    vs_ref[0, CONV_PAD:CONV_PAD + seq, :] = u_ref[:, :D_CONV] * _sigmoid(u_ref[:, D_CONV:])
    for s in range(1, SUBLANES):
        vs_ref[s, 0:n_pad - SUBLANES, :] = vs_ref[0, s:s + n_pad - SUBLANES, :]
    first = CONV_PAD - CONV_WIDTH // 2
    assert (first + CONV_WIDTH - 1) // SUBLANES * SUBLANES + seq <= n_pad - SUBLANES

    def chunk(ci, carry):
        base = pl.multiple_of(ci * CONV_CHUNK, CONV_CHUNK)
        acc = jnp.zeros((CONV_CHUNK, D_CONV), F32) + dwb_ref[...]
        for t in range(CONV_WIDTH):
            s, a = (first + t) % SUBLANES, (first + t) // SUBLANES
            acc = acc + vs_ref[s, pl.ds(base + a * SUBLANES, CONV_CHUNK), :] * dww_ref[t:t + 1, :]
        mu = jnp.mean(acc, axis=-1, keepdims=True)
        cen = acc - mu
        var = jnp.mean(cen * cen, axis=-1, keepdims=True)
        y = cen * lax.rsqrt(var + LN_EPS) * lng_ref[...] + lnb_ref[...]
        y = y * _sigmoid(y)
        o_ref[pl.ds(base, CONV_CHUNK), :] = (_bdot(y.astype(BF16), pww_ref[...]) + pwb_ref[...]).astype(BF16)
        return carry

    lax.fori_loop(0, seq // CONV_CHUNK, chunk, 0, unroll=2)


def _conv(uc, dww, dwb, lng, lnb, pww, pwb, *, seq):
    t = uc.shape[0]
    vec = pl.BlockSpec((1, D_CONV), lambda b: (0, 0))
    return pl.pallas_call(
        _conv_kernel,
        grid=(t // seq,),
        in_specs=[pl.BlockSpec((seq, 2 * D_CONV), lambda b: (b, 0)),
                  pl.BlockSpec((CONV_WIDTH, D_CONV), lambda b: (0, 0)),
                  vec, vec, vec,
                  pl.BlockSpec((D_CONV, D_CONV), lambda b: (0, 0)),
                  vec],
        out_specs=pl.BlockSpec((seq, D_CONV), lambda b: (b, 0)),
        out_shape=jax.ShapeDtypeStruct((t, D_CONV), BF16),
        scratch_shapes=[pltpu.VMEM((SUBLANES, seq + 2 * CONV_PAD, D_CONV), F32)],
        compiler_params=_params("parallel"),
        name="conv_module",
    )(uc, dww, dwb, lng, lnb, pww, pwb)


def _head_mask():
    return lax.broadcasted_iota(jnp.int32, (1, LANES), 1) < NA_HEAD_DIM


Q_ROWS = 2
BAND_ROWS = 10
assert BAND_ROWS >= WIN_ROWS + Q_ROWS - 1 and (BAND_ROWS * GRID_W) % LANES == 0


def _na_patterns(rows):
    starts, sigs = [], []
    for g in range(rows // Q_ROWS):
        start = int(np.clip(Q_ROWS * g - WIN_ROWS // 2, 0, rows - BAND_ROWS))
        r = Q_ROWS * g + np.arange(Q_ROWS)
        rs = np.clip(r - WIN_ROWS // 2, 0, rows - WIN_ROWS)
        starts.append(start)
        sigs.append((start - Q_ROWS * g,) + tuple(rs - r))
    run_starts = [g for g in range(len(sigs)) if g == 0 or sigs[g] != sigs[g - 1]]
    assert len(set(sigs)) == len(run_starts)
    return starts, run_starts


def _na_kernel(q_ref, k_ref, v_ref, kc_ref, vc_ref, pair_ref, o_ref, s_ref, p_ref, bias_ref, *, seq, n_ctx):
    rows = seq // GRID_W
    per_batch = rows // Q_ROWS
    n_groups = q_ref.shape[0] // seq * per_batch
    n_q = Q_ROWS * GRID_W
    n_loc = BAND_ROWS * GRID_W
    _, run_starts = _na_patterns(rows)
    first = _head_mask()
    ones = jnp.ones((1, LANES), BF16)
    dn = (((1,), (1,)), ((), ()))
    assert n_groups >= 4 and seq == per_batch * n_q

    def split(t):
        t = jnp.asarray(t, jnp.int32)
        return t // per_batch, t % per_batch

    def band(t):
        b, g = split(t)
        start = jnp.clip(Q_ROWS * g - WIN_ROWS // 2, 0, rows - BAND_ROWS)
        return pl.multiple_of(b * seq + start * GRID_W, LANES)

    def q_start(t):
        return pl.multiple_of(jnp.asarray(t, jnp.int32) * n_q, n_q)

    def ctx_start(t):
        return pl.multiple_of(split(t)[0] * n_ctx, n_ctx)

    def scores(t, slot):
        g = split(t)[1]
        pat = sum((g >= s).astype(jnp.int32) for s in run_starts[1:])
        q = q_ref[pl.ds(q_start(t), n_q), :]
        kb = k_ref[pl.ds(band(t), n_loc), :]
        kc = kc_ref[pl.ds(ctx_start(t), n_ctx), :]
        for a in range(2):
            qa = jnp.where(first if a == 0 else ~first, q, jnp.zeros_like(q))
            s_ref[slot, a, :, :n_loc] = lax.dot_general(qa, kb, dn, preferred_element_type=F32) + bias_ref[a, pat]
            s_ref[slot, a, :, n_loc:] = lax.dot_general(qa, kc, dn, preferred_element_type=F32)

    def softmax(slot):
        for a in range(2):
            s = s_ref[slot, a]
            p_ref[slot, a] = jnp.exp((s - jnp.max(s, axis=-1, keepdims=True)).astype(BF16))

    def values(t, slot):
        vb = v_ref[pl.ds(band(t), n_loc), :]
        vc = vc_ref[pl.ds(ctx_start(t), n_ctx), :]
        outs = []
        for a in range(2):
            keep = first if a == 0 else ~first
            o = (_bdot(p_ref[slot, a, :, :n_loc], jnp.where(keep, vb, ones))
                 + _bdot(p_ref[slot, a, :, n_loc:], jnp.where(keep, vc, ones)))
            outs.append(o * (1.0 / pltpu.roll(o, NA_HEAD_DIM, axis=1)))
        o_ref[pl.ds(q_start(t), n_q), :] = jnp.where(first, outs[0], outs[1]).astype(BF16)

    def tick(t, parity):
        values(t - 1, 1 - parity)
        scores(t + 1, 1 - parity)
        softmax(parity)

    def quad(j, carry):
        t = 4 * j + 1
        tick(t, 1)
        tick(t + 1, 0)
        tick(t + 2, 1)
        tick(t + 3, 0)
        return carry

    @pl.when(pl.program_id(1) == 0)
    def _():
        _na_assemble_bias(pair_ref, bias_ref, rows)

    scores(0, 0)
    scores(1, 1)
    softmax(0)
    n_quads = (n_groups - 2) // 4
    lax.fori_loop(0, n_quads, quad, 0)
    for t in range(4 * n_quads + 1, n_groups - 1):
        tick(t, t % 2)
    softmax((n_groups - 1) % 2)
    values(n_groups - 2, n_groups % 2)
    values(n_groups - 1, (n_groups - 1) % 2)


def _na_bias(rpb):
    n_heads, n_dr, n_dc = rpb.shape
    lead = GRID_W - WIN_COLS
    wpad = jnp.pad(rpb.astype(F32), ((0, 0), (0, 0), (lead, 2 * GRID_W - n_dc - lead)))
    flat = jnp.tile(wpad, (1, 1, GRID_W))[..., :GRID_W * (2 * GRID_W - 1)]
    toep = flat.reshape(n_heads, n_dr, GRID_W, 2 * GRID_W - 1)[..., GRID_W - 1:]
    col = np.arange(GRID_W)
    col_start = np.clip(col - WIN_COLS // 2, 0, GRID_W - WIN_COLS)
    col_valid = (col[None, :] >= col_start[:, None]) & (col[None, :] < col_start[:, None] + WIN_COLS)
    assert np.all(np.abs(col[None, :] - col[:, None])[col_valid] <= WIN_COLS - 1)
    toep = jnp.where(jnp.asarray(col_valid), toep, -jnp.inf)
    ext = jnp.pad(toep, ((0, 0), (1, 1), (0, 0), (0, 0)), constant_values=-jnp.inf)
    pairs = jnp.concatenate([ext[:, :-1], ext[:, 1:]], axis=-1)
    return pairs.reshape(HEAD_PAIRS, 2, n_dr + 1, GRID_W, 2 * GRID_W)


def _na_assemble_bias(pair_ref, bias_ref, rows):
    starts, run_starts = _na_patterns(rows)
    low = lax.broadcasted_iota(jnp.int32, (GRID_W, LANES), 1) < GRID_W
    masked = jnp.full((GRID_W, LANES), -jnp.inf, F32)
    for a in range(2):
        for p, g in enumerate(run_starts):
            for qi in range(Q_ROWS):
                r = Q_ROWS * g + qi
                rs = int(np.clip(r - WIN_ROWS // 2, 0, rows - WIN_ROWS))
                below = rs - starts[g]
                dr0 = rs - r + WIN_ROWS - 1
                for t in range(BAND_ROWS // 2):
                    d0 = dr0 + 2 * t - below
                    v0 = below <= 2 * t < below + WIN_ROWS
                    v1 = below <= 2 * t + 1 < below + WIN_ROWS
                    if v0 and v1:
                        tile = pair_ref[0, a, d0 + 1]
                    elif v0:
                        tile = jnp.where(low, pair_ref[0, a, d0 + 1], masked)
                    elif v1:
                        tile = jnp.where(low, masked, pair_ref[0, a, d0 + 1])
                    else:
                        tile = masked
                    bias_ref[a, p, qi * GRID_W:(qi + 1) * GRID_W, t * LANES:(t + 1) * LANES] = tile


NA_BATCHES = 4


def _na(qkv, kv_ctx, pairs, *, seq, n_ctx, kc_off, vc_off):
    t = qkv.shape[0]
    hp = HEAD_PAIRS
    tb, tbc = NA_BATCHES * seq, NA_BATCHES * n_ctx
    n_pat = len(_na_patterns(seq // GRID_W)[1])
    n_keys = BAND_ROWS * GRID_W + n_ctx
    assert t % tb == 0
    return pl.pallas_call(
        functools.partial(_na_kernel, seq=seq, n_ctx=n_ctx),
        grid=(hp, t // tb),
        in_specs=[pl.BlockSpec((tb, LANES), lambda h, b: (b, h)),
                  pl.BlockSpec((tb, LANES), lambda h, b: (b, hp + h)),
                  pl.BlockSpec((tb, LANES), lambda h, b: (b, 2 * hp + h)),
                  pl.BlockSpec((tbc, LANES), lambda h, b: (b, kc_off + h)),
                  pl.BlockSpec((tbc, LANES), lambda h, b: (b, vc_off + h)),
                  pl.BlockSpec((1,) + pairs.shape[1:], lambda h, b: (h, 0, 0, 0, 0))],
        out_specs=pl.BlockSpec((tb, LANES), lambda h, b: (b, h)),
        out_shape=jax.ShapeDtypeStruct((t, D_NA), BF16),
        scratch_shapes=[pltpu.VMEM((2, 2, Q_ROWS * GRID_W, n_keys), F32),
                        pltpu.VMEM((2, 2, Q_ROWS * GRID_W, n_keys), BF16),
                        pltpu.VMEM((2, n_pat, Q_ROWS * GRID_W, BAND_ROWS * GRID_W), F32)],
        compiler_params=_params("parallel", "arbitrary"),
        name="na_attention",
    )(qkv, qkv, qkv, kv_ctx, kv_ctx, pairs)


def _ctx_attn_kernel(q_ref, k_ref, v_ref, o_ref):
    first = _head_mask()
    q = q_ref[...]
    k = k_ref[...]
    v = v_ref[...]
    outs = []
    for a in range(2):
        qa = jnp.where(first if a == 0 else ~first, q, jnp.zeros_like(q))
        s = lax.dot_general(qa, k, (((1,), (1,)), ((), ())), preferred_element_type=F32)
        p = jnp.exp(s - jnp.max(s, axis=-1, keepdims=True))
        den = jnp.sum(p, axis=-1, keepdims=True)
        outs.append(_bdot(p.astype(BF16), v) / den)
    o_ref[...] = jnp.where(first, outs[0], outs[1]).astype(BF16)


def _ctx_attn(qkv, *, n_ctx):
    t = qkv.shape[0]
    hp = HEAD_PAIRS
    return pl.pallas_call(
        _ctx_attn_kernel,
        grid=(t // n_ctx, hp),
        in_specs=[pl.BlockSpec((n_ctx, LANES), lambda b, h: (b, h)),
                  pl.BlockSpec((n_ctx, LANES), lambda b, h: (b, hp + h)),
                  pl.BlockSpec((n_ctx, LANES), lambda b, h: (b, 2 * hp + h))],
        out_specs=pl.BlockSpec((n_ctx, LANES), lambda b, h: (b, h)),
        out_shape=jax.ShapeDtypeStruct((t, D_NA), BF16),
        compiler_params=_params("parallel", "parallel"),
        name="ctx_attention",
    )(qkv, qkv, qkv)


def _outmlp_kernel(h_ref, yf_ref, cv_ref, at_ref, mod_ref, g_ref, gf_ref, wo_ref, w1_ref, w2_ref, o_ref, *, final_norm):
    mix = (_bdot(yf_ref[...], wo_ref[:D_FOURIER, :])
           + _bdot(cv_ref[...], wo_ref[D_FOURIER:D_FOURIER + D_CONV, :])
           + _bdot(at_ref[...], wo_ref[D_FOURIER + D_CONV:, :]))
    h1 = h_ref[...] + mod_ref[0, 2:3, :] * mix
    hn = _rms_mod(h1, g_ref[...], mod_ref[0, 3:4, :], mod_ref[0, 4:5, :]).astype(BF16)
    a = jnp.maximum(_bdot(hn, w1_ref[...]), 0.0)
    out = h1 + mod_ref[0, 5:6, :] * _bdot((a * a).astype(BF16), w2_ref[...])
    if final_norm:
        out = out * lax.rsqrt(jnp.mean(out * out, axis=-1, keepdims=True) + RMS_EPS) * gf_ref[...]
    o_ref[...] = out


def _outmlp(h, yf, cv, at, mods, g2, gf, wo, w1, w2, *, seq, tm, ctx, final_norm):
    t, d = h.shape
    nb = t // seq
    per = seq // tm
    row = (lambda i: nb) if ctx else (lambda i: i // per)
    d_ff = w1.shape[1]
    vec = pl.BlockSpec((1, d), lambda i: (0, 0))
    once = dict(pipeline_mode=pl.Buffered(1))
    return pl.pallas_call(
        functools.partial(_outmlp_kernel, final_norm=final_norm),
        grid=(t // tm,),
        in_specs=[pl.BlockSpec((tm, d), lambda i: (i, 0)),
                  pl.BlockSpec((tm, D_FOURIER), lambda i: (i % per, i // per)),
                  pl.BlockSpec((tm, D_CONV), lambda i: (i, 0)),
                  pl.BlockSpec((tm, D_NA), lambda i: (i, 0)),
                  pl.BlockSpec((1, N_MOD, d), lambda i: (row(i), 0, 0)),
                  vec, vec,
                  pl.BlockSpec((D_MIX, d), lambda i: (0, 0), **once),
                  pl.BlockSpec((d, d_ff), lambda i: (0, 0), **once),
                  pl.BlockSpec((d_ff, d), lambda i: (0, 0), **once)],
        out_specs=pl.BlockSpec((tm, d), lambda i: (i, 0)),
        out_shape=jax.ShapeDtypeStruct((t, d), F32),
        compiler_params=_params("parallel"),
        name="outmlp_ctx" if ctx else "outmlp_lat",
    )(h, yf, cv, at, mods, g2, gf, wo, w1, w2)


def kernel(x, c, ctx, c_ctx, ada_w, ada_b, norm1_g, norm2_g, w_in, w_fourier, conv_dw_w, conv_dw_b, conv_norm_g,
           conv_norm_b, conv_pw_w, conv_pw_b, na_rpb, w_out, mlp_w1, mlp_w2, final_norm_g):
    nb, seq, d = x.shape
    n_ctx = ctx.shape[1]
    depth = ada_w.shape[0]
    assert nb < MOD_ROWS and seq % GRID_W == 0 and d == D_MODEL

    cc = jnp.concatenate([c, c_ctx[None], jnp.zeros((MOD_ROWS - nb - 1, d), F32)], axis=0)
    mods = _adaln(cc, ada_w, ada_b).reshape(depth, MOD_ROWS, N_MOD, d)
    ab_lat = _fold_fourier(w_fourier, seq)
    ab_ctx = _fold_fourier(w_fourier, n_ctx)
    c_lat, s_lat = _dft_mats(seq)
    c_ctx_m, s_ctx_m = _dft_mats(n_ctx)

    w_in_b = w_in.astype(BF16)
    w_out_b = w_out.astype(BF16)
    w1_b = mlp_w1.astype(BF16)
    w2_b = mlp_w2.astype(BF16)
    pww_b = conv_pw_w.astype(BF16)
    gf = final_norm_g.reshape(1, d)

    h_lat = x.reshape(nb * seq, d)
    h_ctx = ctx.reshape(nb * n_ctx, d)
    for i in range(depth):
        last = i == depth - 1
        g1 = norm1_g[i].reshape(1, d)
        g2 = norm2_g[i].reshape(1, d)
        conv_p = (conv_dw_w[i], conv_dw_b[i].reshape(1, -1), conv_norm_g[i].reshape(1, -1),
                  conv_norm_b[i].reshape(1, -1), pww_b[i], conv_pw_b[i].reshape(1, -1))
        bias = _na_bias(na_rpb[i])

        if last:
            kv_ctx = _kvproj(h_ctx, g1, mods[i], w_in_b[i], nb=nb, tm=n_ctx)
            kc_off, vc_off = 0, HEAD_PAIRS
        else:
            zf_c, uc_c, qkv_c = _inproj(h_ctx, g1, mods[i], w_in_b[i], ab_ctx[i], seq=n_ctx, tm=n_ctx, ctx=True)
            kv_ctx = qkv_c
            kc_off, vc_off = HEAD_PAIRS, 2 * HEAD_PAIRS

        zf, uc, qkv = _inproj(h_lat, g1, mods[i], w_in_b[i], ab_lat[i], seq=seq, tm=1024, ctx=False)
        yf = _fourier(c_lat, s_lat, zf, tm=1024)
        cv = _conv(uc, *conv_p, seq=seq)
        at = _na(qkv, kv_ctx, bias, seq=seq, n_ctx=n_ctx, kc_off=kc_off, vc_off=vc_off)
        h_lat = _outmlp(h_lat, yf, cv, at, mods[i], g2, gf, w_out_b[i], w1_b[i], w2_b[i],
                        seq=seq, tm=512, ctx=False, final_norm=last)

        if not last:
            yf_c = _fourier(c_ctx_m, s_ctx_m, zf_c, tm=n_ctx)
            cv_c = _conv(uc_c, *conv_p, seq=n_ctx)
            at_c = _ctx_attn(qkv_c, n_ctx=n_ctx)
            h_ctx = _outmlp(h_ctx, yf_c, cv_c, at_c, mods[i], g2, gf, w_out_b[i], w1_b[i], w2_b[i],
                            seq=n_ctx, tm=n_ctx, ctx=True, final_norm=False)
    return h_lat.reshape(nb, seq, d)
```

```python
import functools

import numpy as np
import jax
import jax.numpy as jnp
from jax import lax
from jax.experimental import pallas as pl
from jax.experimental.pallas import tpu as pltpu

D_MODEL = 1024
GRID_W = 64
D_FOURIER = 256
FOURIER_GROUPS = 4
D_CONV = 256
CONV_WIDTH = 31
N_NA_HEADS = 8
NA_HEAD_DIM = 64
D_NA = N_NA_HEADS * NA_HEAD_DIM
WIN_ROWS = 8
WIN_COLS = 16
QKV_START = D_FOURIER + 2 * D_CONV
KV_START = QKV_START + D_NA
D_MIX = D_FOURIER + D_CONV + D_NA
N_MOD = 6
RMS_EPS = 1e-6
LN_EPS = 1e-5

LANES = 128
SUBLANES = 8
HEAD_PAIRS = N_NA_HEADS * NA_HEAD_DIM // LANES
MOD_ROWS = 16
VMEM_LIMIT = 56 * 1024 * 1024

F32 = jnp.float32
BF16 = jnp.bfloat16


def _params(*sem):
    return pltpu.CompilerParams(dimension_semantics=sem, vmem_limit_bytes=VMEM_LIMIT)


def _sigmoid(x):
    return 1.0 / (1.0 + jnp.exp(-x))


def _rms_mod(x, g, shift, scale):
    y = x * lax.rsqrt(jnp.mean(x * x, axis=-1, keepdims=True) + RMS_EPS) * g
    return y * (1.0 + scale) + shift


def _bdot(a, b):
    return jnp.dot(a, b, preferred_element_type=F32)


def _adaln_kernel(c_ref, w_ref, b_ref, o_ref):
    cc = c_ref[...]
    a = cc * _sigmoid(cc)
    w = w_ref[0]
    a_hi = a.astype(BF16)
    a_lo = (a - a_hi.astype(F32)).astype(BF16)
    w_hi = w.astype(BF16)
    w_lo = (w - w_hi.astype(F32)).astype(BF16)
    by_hi = _bdot(jnp.concatenate([a_hi, a_lo], axis=0), w_hi)
    o_ref[0] = by_hi[:MOD_ROWS] + by_hi[MOD_ROWS:] + _bdot(a_hi, w_lo) + b_ref[0]


def _adaln(cc, ada_w, ada_b):
    depth, d, n = ada_w.shape
    tn = 1024
    return pl.pallas_call(
        _adaln_kernel,
        grid=(depth, n // tn),
        in_specs=[pl.BlockSpec((MOD_ROWS, d), lambda l, j: (0, 0)),
                  pl.BlockSpec((1, d, tn), lambda l, j: (l, 0, j)),
                  pl.BlockSpec((1, 1, tn), lambda l, j: (l, 0, j))],
        out_specs=pl.BlockSpec((1, MOD_ROWS, tn), lambda l, j: (l, 0, j)),
        out_shape=jax.ShapeDtypeStruct((depth, MOD_ROWS, n), F32),
        compiler_params=_params("parallel", "parallel"),
        name="adaln",
    )(cc, ada_w, ada_b.reshape(depth, 1, n))


def _fold_kernel(cc_ref, sc_ref, w_ref, o_ref):
    w = w_ref[0]
    hi = lax.Precision.HIGHEST
    o_ref[0, :, :D_FOURIER] = jnp.dot(cc_ref[...], w, preferred_element_type=F32, precision=hi).astype(BF16)
    o_ref[0, :, D_FOURIER:] = jnp.dot(sc_ref[...], w, preferred_element_type=F32, precision=hi).astype(BF16)


def _fold_fourier(w_fourier, seq):
    depth = w_fourier.shape[0]
    gs = D_FOURIER // FOURIER_GROUPS
    idx = np.arange(D_FOURIER)
    same = (idx[:, None] // gs) == (idx[None, :] // gs)
    ang = 2.0 * np.pi * (((idx[:, None] % gs) * (idx[None, :] % gs)) % gs) / gs
    scale = 1.0 / np.sqrt(seq * gs)
    cc = jnp.asarray(np.where(same, np.cos(ang), 0.0) * scale, F32)
    sc = jnp.asarray(np.where(same, np.sin(ang), 0.0) * scale, F32)
    full = pl.BlockSpec((D_FOURIER, D_FOURIER), lambda l: (0, 0))
    return pl.pallas_call(
        _fold_kernel,
        grid=(depth,),
        in_specs=[full, full, pl.BlockSpec((1, D_FOURIER, D_FOURIER), lambda l: (l, 0, 0))],
        out_specs=pl.BlockSpec((1, D_FOURIER, 2 * D_FOURIER), lambda l: (l, 0, 0)),
        out_shape=jax.ShapeDtypeStruct((depth, D_FOURIER, 2 * D_FOURIER), BF16),
        compiler_params=_params("parallel"),
        name="fold_fourier",
    )(cc, sc, w_fourier)


FLIP_BLOCK = 128


def _dft_mats(seq):
    half = seq // 2
    k = np.arange(half)
    ang = 2.0 * np.pi * ((k[:, None] * k[None, :]) % seq) / seq
    r = np.arange(FLIP_BLOCK)
    perm = (np.arange(2 * FLIP_BLOCK)[None, :] == FLIP_BLOCK - r[:, None]).astype(np.float32)
    alt_col = np.where(k % 2 == 0, 1.0, -1.0).astype(np.float32)[:, None]
    alt_row = np.zeros((SUBLANES, seq), np.float32)
    alt_row[0] = np.where(np.arange(seq) % 2 == 0, 1.0, -1.0)
    return (jnp.asarray(np.cos(ang), F32).astype(BF16), jnp.asarray(np.sin(ang), F32).astype(BF16),
            jnp.asarray(perm, F32).astype(BF16), jnp.asarray(alt_col), jnp.asarray(alt_row, F32).astype(BF16))


def _inproj_kernel(x_ref, g_ref, mod_ref, w_ref, ab_ref, zf_ref, uc_ref, qkv_ref):
    hn = _rms_mod(x_ref[...], g_ref[...], mod_ref[0, 0:1, :], mod_ref[0, 1:2, :]).astype(BF16)
    uf = _bdot(hn, w_ref[:, :D_FOURIER])
    zf_ref[...] = _bdot(uf.astype(BF16), ab_ref[...]).astype(BF16)
    uc_ref[...] = _bdot(hn, w_ref[:, D_FOURIER:QKV_START])
    qkv_ref[:, :D_NA] = (_bdot(hn, w_ref[:, QKV_START:KV_START]) * (NA_HEAD_DIM ** -0.5)).astype(BF16)
    qkv_ref[:, D_NA:] = _bdot(hn, w_ref[:, KV_START:]).astype(BF16)


def _inproj(h, g, mods, w_in, ab, *, seq, tm, ctx):
    t, d = h.shape
    nb = t // seq
    per = seq // tm
    row = (lambda i: nb) if ctx else (lambda i: i // per)
    d_in = w_in.shape[1]
    return pl.pallas_call(
        _inproj_kernel,
        grid=(t // tm,),
        in_specs=[pl.BlockSpec((tm, d), lambda i: (i, 0)),
                  pl.BlockSpec((1, d), lambda i: (0, 0)),
                  pl.BlockSpec((1, N_MOD, d), lambda i: (row(i), 0, 0)),
                  pl.BlockSpec((d, d_in), lambda i: (0, 0), pipeline_mode=pl.Buffered(1)),
                  pl.BlockSpec((D_FOURIER, 2 * D_FOURIER), lambda i: (0, 0))],
        out_specs=[pl.BlockSpec((tm, 2 * D_FOURIER), lambda i: (i % per, i // per)),
                   pl.BlockSpec((tm, 2 * D_CONV), lambda i: (i, 0)),
                   pl.BlockSpec((tm, 3 * D_NA), lambda i: (i, 0))],
        out_shape=[jax.ShapeDtypeStruct((seq, nb * 2 * D_FOURIER), BF16),
                   jax.ShapeDtypeStruct((t, 2 * D_CONV), F32),
                   jax.ShapeDtypeStruct((t, 3 * D_NA), BF16)],
        compiler_params=_params("parallel"),
        name="inproj_ctx" if ctx else "inproj_lat",
    )(h, g, mods, w_in, ab)


def _kvproj_kernel(x_ref, g_ref, mod_ref, w_ref, kv_ref):
    hn = _rms_mod(x_ref[...], g_ref[...], mod_ref[0, 0:1, :], mod_ref[0, 1:2, :]).astype(BF16)
    kv_ref[...] = _bdot(hn, w_ref[:, KV_START:]).astype(BF16)


def _kvproj(h, g, mods, w_in, *, nb, tm):
    t, d = h.shape
    d_in = w_in.shape[1]
    n = d_in - KV_START
    return pl.pallas_call(
        _kvproj_kernel,
        grid=(t // tm,),
        in_specs=[pl.BlockSpec((tm, d), lambda i: (i, 0)),
                  pl.BlockSpec((1, d), lambda i: (0, 0)),
                  pl.BlockSpec((1, N_MOD, d), lambda i: (nb, 0, 0)),
                  pl.BlockSpec((d, d_in), lambda i: (0, 0))],
        out_specs=pl.BlockSpec((tm, n), lambda i: (i, 0)),
        out_shape=jax.ShapeDtypeStruct((t, n), BF16),
        compiler_params=_params("parallel"),
        name="kvproj_ctx",
    )(h, g, mods, w_in)


def _fourier_kernel(c_ref, s_ref, perm_ref, altc_ref, altr_ref, z_ref, y_ref, ze_ref, yr_ref):
    seq = z_ref.shape[0]
    half = seq // 2
    nblk = half // FLIP_BLOCK
    fb = FLIP_BLOCK

    def reversed_block(ref, i, n):
        if i == 0:
            return _bdot(perm_ref[:, :fb], ref[n - fb:n, :])
        lo = n - (i + 1) * fb
        return _bdot(perm_ref[...], ref[lo:lo + 2 * fb, :])

    for i in range(nblk):
        rows_i = slice(i * fb, (i + 1) * fb)
        zr = reversed_block(z_ref, i, seq)
        ze_ref[rows_i, :D_FOURIER] = (z_ref[rows_i, :D_FOURIER].astype(F32) + zr[:, :D_FOURIER]).astype(BF16)
        ze_ref[rows_i, D_FOURIER:] = (z_ref[rows_i, D_FOURIER:].astype(F32) - zr[:, D_FOURIER:]).astype(BF16)
    nyq = z_ref[half:half + 1, :D_FOURIER].astype(F32)
    yc = _bdot(c_ref[...], ze_ref[:, :D_FOURIER]) + altc_ref[...] * nyq
    ys = _bdot(s_ref[...], ze_ref[:, D_FOURIER:])
    y_ref[:half, :] = (yc - ys).astype(BF16)
    yr_ref[...] = (yc + ys).astype(BF16)
    y_mid = _bdot(altr_ref[...], z_ref[:, :D_FOURIER])[0:1, :]
    first_row = lax.broadcasted_iota(jnp.int32, (fb, 1), 0) == 0
    for i in range(nblk):
        blk = reversed_block(yr_ref, i, half)
        if i == 0:
            blk = jnp.where(first_row, y_mid, blk)
        y_ref[half + i * fb:half + (i + 1) * fb, :] = blk.astype(BF16)


def _fourier(mats, zf):
    cmat, smat, perm, alt_col, alt_row = mats
    seq = alt_row.shape[1]
    half = seq // 2
    nb = zf.shape[1] // (2 * D_FOURIER)
    const = lambda a: pl.BlockSpec(a.shape, lambda b: (0, 0))
    return pl.pallas_call(
        _fourier_kernel,
        grid=(nb,),
        in_specs=[const(cmat), const(smat), const(perm), const(alt_col), const(alt_row),
                  pl.BlockSpec((seq, 2 * D_FOURIER), lambda b: (0, b))],
        out_specs=pl.BlockSpec((seq, D_FOURIER), lambda b: (0, b)),
        out_shape=jax.ShapeDtypeStruct((seq, nb * D_FOURIER), BF16),
        scratch_shapes=[pltpu.VMEM((half, 2 * D_FOURIER), BF16), pltpu.VMEM((half, D_FOURIER), BF16)],
        compiler_params=_params("parallel"),
        name="fourier_pos",
    )(cmat, smat, perm, alt_col, alt_row, zf)


CONV_PAD = 16
CONV_CHUNK = 128


def _conv_kernel(u_ref, dww_ref, dwb_ref, lng_ref, lnb_ref, pww_ref, pwb_ref, o_ref, vs_ref):
    seq = u_ref.shape[0]
    n_pad = seq + 2 * CONV_PAD
    zeros = jnp.zeros((CONV_PAD, D_CONV), F32)
    vs_ref[0, 0:CONV_PAD, :] = zeros
    vs_ref[0, seq + CONV_PAD:n_pad, :] = zeros
    vs_ref[0, CONV_PAD:CONV_PAD + seq, :] = u_ref[:, :D_CONV] * _sigmoid(u_ref[:, D_CONV:])
    for s in range(1, SUBLANES):
        vs_ref[s, 0:n_pad - SUBLANES, :] = vs_ref[0, s:s + n_pad - SUBLANES, :]
    first = CONV_PAD - CONV_WIDTH // 2
    assert (first + CONV_WIDTH - 1) // SUBLANES * SUBLANES + seq <= n_pad - SUBLANES

    def chunk(ci, carry):
        base = pl.multiple_of(ci * CONV_CHUNK, CONV_CHUNK)
        acc = jnp.zeros((CONV_CHUNK, D_CONV), F32) + dwb_ref[...]
        for t in range(CONV_WIDTH):
            s, a = (first + t) % SUBLANES, (first + t) // SUBLANES
            acc = acc + vs_ref[s, pl.ds(base + a * SUBLANES, CONV_CHUNK), :] * dww_ref[t:t + 1, :]
        mu = jnp.mean(acc, axis=-1, keepdims=True)
        cen = acc - mu
        var = jnp.mean(cen * cen, axis=-1, keepdims=True)
        y = cen * lax.rsqrt(var + LN_EPS) * lng_ref[...] + lnb_ref[...]
        y = y * _sigmoid(y)
        o_ref[pl.ds(base, CONV_CHUNK), :] = (_bdot(y.astype(BF16), pww_ref[...]) + pwb_ref[...]).astype(BF16)
        return carry

    lax.fori_loop(0, seq // CONV_CHUNK, chunk, 0, unroll=2)


def _conv(uc, dww, dwb, lng, lnb, pww, pwb, *, seq):
    t = uc.shape[0]
    vec = pl.BlockSpec((1, D_CONV), lambda b: (0, 0))
    return pl.pallas_call(
        _conv_kernel,
        grid=(t // seq,),
        in_specs=[pl.BlockSpec((seq, 2 * D_CONV), lambda b: (b, 0)),
                  pl.BlockSpec((CONV_WIDTH, D_CONV), lambda b: (0, 0)),
                  vec, vec, vec,
                  pl.BlockSpec((D_CONV, D_CONV), lambda b: (0, 0)),
                  vec],
        out_specs=pl.BlockSpec((seq, D_CONV), lambda b: (b, 0)),
        out_shape=jax.ShapeDtypeStruct((t, D_CONV), BF16),
        scratch_shapes=[pltpu.VMEM((SUBLANES, seq + 2 * CONV_PAD, D_CONV), F32)],
        compiler_params=_params("parallel"),
        name="conv_module",
    )(uc, dww, dwb, lng, lnb, pww, pwb)


def _head_mask():
    return lax.broadcasted_iota(jnp.int32, (1, LANES), 1) < NA_HEAD_DIM


Q_ROWS = 2
BAND_ROWS = 10
assert BAND_ROWS >= WIN_ROWS + Q_ROWS - 1 and (BAND_ROWS * GRID_W) % LANES == 0


def _na_patterns(rows):
    starts, sigs = [], []
    for g in range(rows // Q_ROWS):
        start = int(np.clip(Q_ROWS * g - WIN_ROWS // 2, 0, rows - BAND_ROWS))
        r = Q_ROWS * g + np.arange(Q_ROWS)
        rs = np.clip(r - WIN_ROWS // 2, 0, rows - WIN_ROWS)
        starts.append(start)
        sigs.append((start - Q_ROWS * g,) + tuple(rs - r))
    run_starts = [g for g in range(len(sigs)) if g == 0 or sigs[g] != sigs[g - 1]]
    assert len(set(sigs)) == len(run_starts)
    return starts, run_starts


def _na_kernel(q_ref, k_ref, v_ref, kc_ref, vc_ref, pair_ref, o_ref, s_ref, p_ref, bias_ref, *, seq, n_ctx):
    rows = seq // GRID_W
    per_batch = rows // Q_ROWS
    n_groups = q_ref.shape[0] // seq * per_batch
    n_q = Q_ROWS * GRID_W
    n_loc = BAND_ROWS * GRID_W
    _, run_starts = _na_patterns(rows)
    first = _head_mask()
    ones = jnp.ones((1, LANES), BF16)
    dn = (((1,), (1,)), ((), ()))
    assert n_groups >= 4 and seq == per_batch * n_q

    def split(t):
        t = jnp.asarray(t, jnp.int32)
        return t // per_batch, t % per_batch

    def band(t):
        b, g = split(t)
        start = jnp.clip(Q_ROWS * g - WIN_ROWS // 2, 0, rows - BAND_ROWS)
        return pl.multiple_of(b * seq + start * GRID_W, LANES)

    def q_start(t):
        return pl.multiple_of(jnp.asarray(t, jnp.int32) * n_q, n_q)

    def ctx_start(t):
        return pl.multiple_of(split(t)[0] * n_ctx, n_ctx)

    def scores(t, slot):
        g = split(t)[1]
        pat = sum((g >= s).astype(jnp.int32) for s in run_starts[1:])
        q = q_ref[pl.ds(q_start(t), n_q), :]
        kb = k_ref[pl.ds(band(t), n_loc), :]
        kc = kc_ref[pl.ds(ctx_start(t), n_ctx), :]
        for a in range(2):
            qa = jnp.where(first if a == 0 else ~first, q, jnp.zeros_like(q))
            s_ref[slot, a, :, :n_loc] = lax.dot_general(qa, kb, dn, preferred_element_type=F32) + bias_ref[a, pat]
            s_ref[slot, a, :, n_loc:] = lax.dot_general(qa, kc, dn, preferred_element_type=F32)

    def softmax(slot):
        for a in range(2):
            s = s_ref[slot, a]
            p_ref[slot, a] = jnp.exp((s - jnp.max(s, axis=-1, keepdims=True)).astype(BF16))

    def values(t, slot):
        vb = v_ref[pl.ds(band(t), n_loc), :]
        vc = vc_ref[pl.ds(ctx_start(t), n_ctx), :]
        outs = []
        for a in range(2):
            keep = first if a == 0 else ~first
            o = (_bdot(p_ref[slot, a, :, :n_loc], jnp.where(keep, vb, ones))
                 + _bdot(p_ref[slot, a, :, n_loc:], jnp.where(keep, vc, ones)))
            outs.append(o * (1.0 / pltpu.roll(o, NA_HEAD_DIM, axis=1)))
        o_ref[pl.ds(q_start(t), n_q), :] = jnp.where(first, outs[0], outs[1]).astype(BF16)

    def tick(t, parity):
        values(t - 1, 1 - parity)
        scores(t + 1, 1 - parity)
        softmax(parity)

    def quad(j, carry):
        t = 4 * j + 1
        tick(t, 1)
        tick(t + 1, 0)
        tick(t + 2, 1)
        tick(t + 3, 0)
        return carry

    @pl.when(pl.program_id(1) == 0)
    def _():
        _na_assemble_bias(pair_ref, bias_ref, rows)

    scores(0, 0)
    scores(1, 1)
    softmax(0)
    n_quads = (n_groups - 2) // 4
    lax.fori_loop(0, n_quads, quad, 0)
    for t in range(4 * n_quads + 1, n_groups - 1):
        tick(t, t % 2)
    softmax((n_groups - 1) % 2)
    values(n_groups - 2, n_groups % 2)
    values(n_groups - 1, (n_groups - 1) % 2)


def _na_bias(rpb):
    n_heads, n_dr, n_dc = rpb.shape
    lead = GRID_W - WIN_COLS
    wpad = jnp.pad(rpb.astype(F32), ((0, 0), (0, 0), (lead, 2 * GRID_W - n_dc - lead)))
    flat = jnp.tile(wpad, (1, 1, GRID_W))[..., :GRID_W * (2 * GRID_W - 1)]
    toep = flat.reshape(n_heads, n_dr, GRID_W, 2 * GRID_W - 1)[..., GRID_W - 1:]
    col = np.arange(GRID_W)
    col_start = np.clip(col - WIN_COLS // 2, 0, GRID_W - WIN_COLS)
    col_valid = (col[None, :] >= col_start[:, None]) & (col[None, :] < col_start[:, None] + WIN_COLS)
    assert np.all(np.abs(col[None, :] - col[:, None])[col_valid] <= WIN_COLS - 1)
    toep = jnp.where(jnp.asarray(col_valid), toep, -jnp.inf)
    ext = jnp.pad(toep, ((0, 0), (1, 1), (0, 0), (0, 0)), constant_values=-jnp.inf)
    pairs = jnp.concatenate([ext[:, :-1], ext[:, 1:]], axis=-1)
    return pairs.reshape(HEAD_PAIRS, 2, n_dr + 1, GRID_W, 2 * GRID_W)


def _na_assemble_bias(pair_ref, bias_ref, rows):
    starts, run_starts = _na_patterns(rows)
    low = lax.broadcasted_iota(jnp.int32, (GRID_W, LANES), 1) < GRID_W
    masked = jnp.full((GRID_W, LANES), -jnp.inf, F32)
    for a in range(2):
        for p, g in enumerate(run_starts):
            for qi in range(Q_ROWS):
                r = Q_ROWS * g + qi
                rs = int(np.clip(r - WIN_ROWS // 2, 0, rows - WIN_ROWS))
                below = rs - starts[g]
                dr0 = rs - r + WIN_ROWS - 1
                for t in range(BAND_ROWS // 2):
                    d0 = dr0 + 2 * t - below
                    v0 = below <= 2 * t < below + WIN_ROWS
                    v1 = below <= 2 * t + 1 < below + WIN_ROWS
                    if v0 and v1:
                        tile = pair_ref[0, a, d0 + 1]
                    elif v0:
                        tile = jnp.where(low, pair_ref[0, a, d0 + 1], masked)
                    elif v1:
                        tile = jnp.where(low, masked, pair_ref[0, a, d0 + 1])
                    else:
                        tile = masked
                    bias_ref[a, p, qi * GRID_W:(qi + 1) * GRID_W, t * LANES:(t + 1) * LANES] = tile


NA_BATCHES = 4


def _na(qkv, kv_ctx, pairs, *, seq, n_ctx, kc_off, vc_off):
    t = qkv.shape[0]
    hp = HEAD_PAIRS
    tb, tbc = NA_BATCHES * seq, NA_BATCHES * n_ctx
    n_pat = len(_na_patterns(seq // GRID_W)[1])
    n_keys = BAND_ROWS * GRID_W + n_ctx
    assert t % tb == 0
    return pl.pallas_call(
        functools.partial(_na_kernel, seq=seq, n_ctx=n_ctx),
        grid=(hp, t // tb),
        in_specs=[pl.BlockSpec((tb, LANES), lambda h, b: (b, h)),
                  pl.BlockSpec((tb, LANES), lambda h, b: (b, hp + h)),
                  pl.BlockSpec((tb, LANES), lambda h, b: (b, 2 * hp + h)),
                  pl.BlockSpec((tbc, LANES), lambda h, b: (b, kc_off + h)),
                  pl.BlockSpec((tbc, LANES), lambda h, b: (b, vc_off + h)),
                  pl.BlockSpec((1,) + pairs.shape[1:], lambda h, b: (h, 0, 0, 0, 0))],
        out_specs=pl.BlockSpec((tb, LANES), lambda h, b: (b, h)),
        out_shape=jax.ShapeDtypeStruct((t, D_NA), BF16),
        scratch_shapes=[pltpu.VMEM((2, 2, Q_ROWS * GRID_W, n_keys), F32),
                        pltpu.VMEM((2, 2, Q_ROWS * GRID_W, n_keys), BF16),
                        pltpu.VMEM((2, n_pat, Q_ROWS * GRID_W, BAND_ROWS * GRID_W), F32)],
        compiler_params=_params("parallel", "arbitrary"),
        name="na_attention",
    )(qkv, qkv, qkv, kv_ctx, kv_ctx, pairs)


def _ctx_attn_kernel(q_ref, k_ref, v_ref, o_ref):
    first = _head_mask()
    for h in range(HEAD_PAIRS):
        lanes = slice(h * LANES, (h + 1) * LANES)
        q = q_ref[:, lanes]
        k = k_ref[:, lanes]
        v = v_ref[:, lanes]
        outs = []
        for a in range(2):
            qa = jnp.where(first if a == 0 else ~first, q, jnp.zeros_like(q))
            s = lax.dot_general(qa, k, (((1,), (1,)), ((), ())), preferred_element_type=F32)
            p = jnp.exp(s - jnp.max(s, axis=-1, keepdims=True))
            den = jnp.sum(p, axis=-1, keepdims=True)
            outs.append(_bdot(p.astype(BF16), v) * (1.0 / den))
        o_ref[:, lanes] = jnp.where(first, outs[0], outs[1]).astype(BF16)


def _ctx_attn(qkv, *, n_ctx):
    t = qkv.shape[0]
    return pl.pallas_call(
        _ctx_attn_kernel,
        grid=(t // n_ctx,),
        in_specs=[pl.BlockSpec((n_ctx, D_NA), lambda b: (b, 0)),
                  pl.BlockSpec((n_ctx, D_NA), lambda b: (b, 1)),
                  pl.BlockSpec((n_ctx, D_NA), lambda b: (b, 2))],
        out_specs=pl.BlockSpec((n_ctx, D_NA), lambda b: (b, 0)),
        out_shape=jax.ShapeDtypeStruct((t, D_NA), BF16),
        compiler_params=_params("parallel"),
        name="ctx_attention",
    )(qkv, qkv, qkv)


def _outmlp_kernel(h_ref, yf_ref, cv_ref, at_ref, mod_ref, g_ref, gf_ref, wo_ref, w1_ref, w2_ref, o_ref, *, final_norm):
    mix = (_bdot(yf_ref[...], wo_ref[:D_FOURIER, :])
           + _bdot(cv_ref[...], wo_ref[D_FOURIER:D_FOURIER + D_CONV, :])
           + _bdot(at_ref[...], wo_ref[D_FOURIER + D_CONV:, :]))
    h1 = h_ref[...] + mod_ref[0, 2:3, :] * mix
    hn = _rms_mod(h1, g_ref[...], mod_ref[0, 3:4, :], mod_ref[0, 4:5, :]).astype(BF16)
    a = jnp.maximum(_bdot(hn, w1_ref[...]), 0.0)
    out = h1 + mod_ref[0, 5:6, :] * _bdot((a * a).astype(BF16), w2_ref[...])
    if final_norm:
        out = out * lax.rsqrt(jnp.mean(out * out, axis=-1, keepdims=True) + RMS_EPS) * gf_ref[...]
    o_ref[...] = out


def _outmlp(h, yf, cv, at, mods, g2, gf, wo, w1, w2, *, seq, tm, ctx, final_norm):
    t, d = h.shape
    nb = t // seq
    per = seq // tm
    row = (lambda i: nb) if ctx else (lambda i: i // per)
    d_ff = w1.shape[1]
    vec = pl.BlockSpec((1, d), lambda i: (0, 0))
    once = dict(pipeline_mode=pl.Buffered(1))
    return pl.pallas_call(
        functools.partial(_outmlp_kernel, final_norm=final_norm),
        grid=(t // tm,),
        in_specs=[pl.BlockSpec((tm, d), lambda i: (i, 0)),
                  pl.BlockSpec((tm, D_FOURIER), lambda i: (i % per, i // per)),
                  pl.BlockSpec((tm, D_CONV), lambda i: (i, 0)),
                  pl.BlockSpec((tm, D_NA), lambda i: (i, 0)),
                  pl.BlockSpec((1, N_MOD, d), lambda i: (row(i), 0, 0)),
                  vec, vec,
                  pl.BlockSpec((D_MIX, d), lambda i: (0, 0), **once),
                  pl.BlockSpec((d, d_ff), lambda i: (0, 0), **once),
                  pl.BlockSpec((d_ff, d), lambda i: (0, 0), **once)],
        out_specs=pl.BlockSpec((tm, d), lambda i: (i, 0)),
        out_shape=jax.ShapeDtypeStruct((t, d), F32),
        compiler_params=_params("parallel"),
        name="outmlp_ctx" if ctx else "outmlp_lat",
    )(h, yf, cv, at, mods, g2, gf, wo, w1, w2)


def kernel(x, c, ctx, c_ctx, ada_w, ada_b, norm1_g, norm2_g, w_in, w_fourier, conv_dw_w, conv_dw_b, conv_norm_g,
           conv_norm_b, conv_pw_w, conv_pw_b, na_rpb, w_out, mlp_w1, mlp_w2, final_norm_g):
    nb, seq, d = x.shape
    n_ctx = ctx.shape[1]
    depth = ada_w.shape[0]
    assert nb < MOD_ROWS and seq % GRID_W == 0 and d == D_MODEL

    cc = jnp.concatenate([c, c_ctx[None], jnp.zeros((MOD_ROWS - nb - 1, d), F32)], axis=0)
    mods = _adaln(cc, ada_w, ada_b).reshape(depth, MOD_ROWS, N_MOD, d)
    ab_lat = _fold_fourier(w_fourier, seq)
    ab_ctx = _fold_fourier(w_fourier, n_ctx)
    dft_lat = _dft_mats(seq)
    dft_ctx = _dft_mats(n_ctx)

    w_in_b = w_in.astype(BF16)
    w_out_b = w_out.astype(BF16)
    w1_b = mlp_w1.astype(BF16)
    w2_b = mlp_w2.astype(BF16)
    pww_b = conv_pw_w.astype(BF16)
    gf = final_norm_g.reshape(1, d)

    h_lat = x.reshape(nb * seq, d)
    h_ctx = ctx.reshape(nb * n_ctx, d)
    for i in range(depth):
        last = i == depth - 1
        g1 = norm1_g[i].reshape(1, d)
        g2 = norm2_g[i].reshape(1, d)
        conv_p = (conv_dw_w[i], conv_dw_b[i].reshape(1, -1), conv_norm_g[i].reshape(1, -1),
                  conv_norm_b[i].reshape(1, -1), pww_b[i], conv_pw_b[i].reshape(1, -1))
        bias = _na_bias(na_rpb[i])

        if last:
            kv_ctx = _kvproj(h_ctx, g1, mods[i], w_in_b[i], nb=nb, tm=n_ctx)
            kc_off, vc_off = 0, HEAD_PAIRS
        else:
            zf_c, uc_c, qkv_c = _inproj(h_ctx, g1, mods[i], w_in_b[i], ab_ctx[i], seq=n_ctx, tm=n_ctx, ctx=True)
            kv_ctx = qkv_c
            kc_off, vc_off = HEAD_PAIRS, 2 * HEAD_PAIRS

        zf, uc, qkv = _inproj(h_lat, g1, mods[i], w_in_b[i], ab_lat[i], seq=seq, tm=1024, ctx=False)
        yf = _fourier(dft_lat, zf)
        cv = _conv(uc, *conv_p, seq=seq)
        at = _na(qkv, kv_ctx, bias, seq=seq, n_ctx=n_ctx, kc_off=kc_off, vc_off=vc_off)
        h_lat = _outmlp(h_lat, yf, cv, at, mods[i], g2, gf, w_out_b[i], w1_b[i], w2_b[i],
                        seq=seq, tm=512, ctx=False, final_norm=last)

        if not last:
            yf_c = _fourier(dft_ctx, zf_c)
            cv_c = _conv(uc_c, *conv_p, seq=n_ctx)
            at_c = _ctx_attn(qkv_c, n_ctx=n_ctx)
            h_ctx = _outmlp(h_ctx, yf_c, cv_c, at_c, mods[i], g2, gf, w_out_b[i], w1_b[i], w2_b[i],
                            seq=n_ctx, tm=n_ctx, ctx=True, final_norm=False)
    return h_lat.reshape(nb, seq, d)
```

```python
import functools

import numpy as np
import jax
import jax.numpy as jnp
from jax import lax
from jax.experimental import pallas as pl
from jax.experimental.pallas import tpu as pltpu

D_MODEL = 1024
GRID_W = 64
D_FOURIER = 256
FOURIER_GROUPS = 4
D_CONV = 256
CONV_WIDTH = 31
N_NA_HEADS = 8
NA_HEAD_DIM = 64
D_NA = N_NA_HEADS * NA_HEAD_DIM
WIN_ROWS = 8
WIN_COLS = 16
QKV_START = D_FOURIER + 2 * D_CONV
KV_START = QKV_START + D_NA
D_MIX = D_FOURIER + D_CONV + D_NA
N_MOD = 6
RMS_EPS = 1e-6
LN_EPS = 1e-5

LANES = 128
SUBLANES = 8
HEAD_PAIRS = N_NA_HEADS * NA_HEAD_DIM // LANES
MOD_ROWS = 16
VMEM_LIMIT = 56 * 1024 * 1024

F32 = jnp.float32
BF16 = jnp.bfloat16


def _params(*sem):
    return pltpu.CompilerParams(dimension_semantics=sem, vmem_limit_bytes=VMEM_LIMIT)


def _sigmoid(x):
    return 1.0 / (1.0 + jnp.exp(-x))


def _rms_mod(x, g, shift, scale):
    y = x * lax.rsqrt(jnp.mean(x * x, axis=-1, keepdims=True) + RMS_EPS) * g
    return y * (1.0 + scale) + shift


def _bdot(a, b):
    return jnp.dot(a, b, preferred_element_type=F32)


def _adaln_kernel(c_ref, w_ref, b_ref, o_ref):
    cc = c_ref[...]
    a = cc * _sigmoid(cc)
    w = w_ref[0]
    a_hi = a.astype(BF16)
    a_lo = (a - a_hi.astype(F32)).astype(BF16)
    w_hi = w.astype(BF16)
    w_lo = (w - w_hi.astype(F32)).astype(BF16)
    by_hi = _bdot(jnp.concatenate([a_hi, a_lo], axis=0), w_hi)
    o_ref[0] = by_hi[:MOD_ROWS] + by_hi[MOD_ROWS:] + _bdot(a_hi, w_lo) + b_ref[0]


def _adaln(cc, ada_w, ada_b):
    depth, d, n = ada_w.shape
    tn = 1024
    return pl.pallas_call(
        _adaln_kernel,
        grid=(depth, n // tn),
        in_specs=[pl.BlockSpec((MOD_ROWS, d), lambda l, j: (0, 0)),
                  pl.BlockSpec((1, d, tn), lambda l, j: (l, 0, j)),
                  pl.BlockSpec((1, 1, tn), lambda l, j: (l, 0, j))],
        out_specs=pl.BlockSpec((1, MOD_ROWS, tn), lambda l, j: (l, 0, j)),
        out_shape=jax.ShapeDtypeStruct((depth, MOD_ROWS, n), F32),
        compiler_params=_params("parallel", "parallel"),
        name="adaln",
    )(cc, ada_w, ada_b.reshape(depth, 1, n))


def _fold_kernel(cc_ref, sc_ref, w_ref, o_ref):
    w = w_ref[0]
    hi = lax.Precision.HIGHEST
    o_ref[0, :, :D_FOURIER] = jnp.dot(cc_ref[...], w, preferred_element_type=F32, precision=hi).astype(BF16)
    o_ref[0, :, D_FOURIER:] = jnp.dot(sc_ref[...], w, preferred_element_type=F32, precision=hi).astype(BF16)


def _fold_fourier(w_fourier, seq):
    depth = w_fourier.shape[0]
    gs = D_FOURIER // FOURIER_GROUPS
    idx = np.arange(D_FOURIER)
    same = (idx[:, None] // gs) == (idx[None, :] // gs)
    ang = 2.0 * np.pi * (((idx[:, None] % gs) * (idx[None, :] % gs)) % gs) / gs
    scale = 1.0 / np.sqrt(seq * gs)
    cc = jnp.asarray(np.where(same, np.cos(ang), 0.0) * scale, F32)
    sc = jnp.asarray(np.where(same, np.sin(ang), 0.0) * scale, F32)
    full = pl.BlockSpec((D_FOURIER, D_FOURIER), lambda l: (0, 0))
    return pl.pallas_call(
        _fold_kernel,
        grid=(depth,),
        in_specs=[full, full, pl.BlockSpec((1, D_FOURIER, D_FOURIER), lambda l: (l, 0, 0))],
        out_specs=pl.BlockSpec((1, D_FOURIER, 2 * D_FOURIER), lambda l: (l, 0, 0)),
        out_shape=jax.ShapeDtypeStruct((depth, D_FOURIER, 2 * D_FOURIER), BF16),
        compiler_params=_params("parallel"),
        name="fold_fourier",
    )(cc, sc, w_fourier)


FLIP_BLOCK = 128


def _dft_mats(seq):
    half = seq // 2
    k = np.arange(half)
    ang = 2.0 * np.pi * ((k[:, None] * k[None, :]) % seq) / seq
    r = np.arange(FLIP_BLOCK)
    perm = (np.arange(2 * FLIP_BLOCK)[None, :] == FLIP_BLOCK - r[:, None]).astype(np.float32)
    alt_col = np.where(k % 2 == 0, 1.0, -1.0).astype(np.float32)[:, None]
    alt_row = np.zeros((SUBLANES, seq), np.float32)
    alt_row[0] = np.where(np.arange(seq) % 2 == 0, 1.0, -1.0)
    return (jnp.asarray(np.cos(ang), F32).astype(BF16), jnp.asarray(np.sin(ang), F32).astype(BF16),
            jnp.asarray(perm, F32).astype(BF16), jnp.asarray(alt_col), jnp.asarray(alt_row, F32).astype(BF16))


def _store_kv(hn, w_ref, out_ref, col):
    kv = _bdot(hn, w_ref[:, KV_START:])
    v = kv[:, D_NA:]
    first = lax.broadcasted_iota(jnp.int32, (1, D_NA), 1) % LANES < NA_HEAD_DIM
    out_ref[:, col:col + D_NA] = kv[:, :D_NA].astype(BF16)
    out_ref[:, col + D_NA:col + 2 * D_NA] = jnp.where(first, v, 1.0).astype(BF16)
    out_ref[:, col + 2 * D_NA:col + 3 * D_NA] = jnp.where(first, 1.0, v).astype(BF16)


def _inproj_kernel(x_ref, g_ref, mod_ref, w_ref, ab_ref, zf_ref, uc_ref, qkv_ref):
    hn = _rms_mod(x_ref[...], g_ref[...], mod_ref[0, 0:1, :], mod_ref[0, 1:2, :]).astype(BF16)
    uf = _bdot(hn, w_ref[:, :D_FOURIER])
    uc_ref[...] = _bdot(hn, w_ref[:, D_FOURIER:QKV_START])
    qkv_ref[:, :D_NA] = (_bdot(hn, w_ref[:, QKV_START:KV_START]) * (NA_HEAD_DIM ** -0.5)).astype(BF16)
    _store_kv(hn, w_ref, qkv_ref, D_NA)
    zf_ref[...] = _bdot(uf.astype(BF16), ab_ref[...]).astype(BF16)


def _inproj(h, g, mods, w_in, ab, *, seq, tm, ctx):
    t, d = h.shape
    nb = t // seq
    per = seq // tm
    row = (lambda i: nb) if ctx else (lambda i: i // per)
    d_in = w_in.shape[1]
    return pl.pallas_call(
        _inproj_kernel,
        grid=(t // tm,),
        in_specs=[pl.BlockSpec((tm, d), lambda i: (i, 0)),
                  pl.BlockSpec((1, d), lambda i: (0, 0)),
                  pl.BlockSpec((1, N_MOD, d), lambda i: (row(i), 0, 0)),
                  pl.BlockSpec((d, d_in), lambda i: (0, 0), pipeline_mode=pl.Buffered(1)),
                  pl.BlockSpec((D_FOURIER, 2 * D_FOURIER), lambda i: (0, 0))],
        out_specs=[pl.BlockSpec((tm, 2 * D_FOURIER), lambda i: (i % per, i // per)),
                   pl.BlockSpec((tm, 2 * D_CONV), lambda i: (i, 0)),
                   pl.BlockSpec((tm, 4 * D_NA), lambda i: (i, 0))],
        out_shape=[jax.ShapeDtypeStruct((seq, nb * 2 * D_FOURIER), BF16),
                   jax.ShapeDtypeStruct((t, 2 * D_CONV), F32),
                   jax.ShapeDtypeStruct((t, 4 * D_NA), BF16)],
        compiler_params=_params("parallel"),
        name="inproj_ctx" if ctx else "inproj_lat",
    )(h, g, mods, w_in, ab)


def _kvproj_kernel(x_ref, g_ref, mod_ref, w_ref, kv_ref):
    hn = _rms_mod(x_ref[...], g_ref[...], mod_ref[0, 0:1, :], mod_ref[0, 1:2, :]).astype(BF16)
    _store_kv(hn, w_ref, kv_ref, 0)


def _kvproj(h, g, mods, w_in, *, nb, tm):
    t, d = h.shape
    d_in = w_in.shape[1]
    n = 3 * D_NA
    return pl.pallas_call(
        _kvproj_kernel,
        grid=(t // tm,),
        in_specs=[pl.BlockSpec((tm, d), lambda i: (i, 0)),
                  pl.BlockSpec((1, d), lambda i: (0, 0)),
                  pl.BlockSpec((1, N_MOD, d), lambda i: (nb, 0, 0)),
                  pl.BlockSpec((d, d_in), lambda i: (0, 0))],
        out_specs=pl.BlockSpec((tm, n), lambda i: (i, 0)),
        out_shape=jax.ShapeDtypeStruct((t, n), BF16),
        compiler_params=_params("parallel"),
        name="kvproj_ctx",
    )(h, g, mods, w_in)


def _fourier_kernel(c_ref, s_ref, perm_ref, altc_ref, altr_ref, z_ref, y_ref, ze_ref, yr_ref):
    seq = z_ref.shape[0]
    half = seq // 2
    nblk = half // FLIP_BLOCK
    fb = FLIP_BLOCK

    def reversed_block(ref, i, n):
        if i == 0:
            return _bdot(perm_ref[:, :fb], ref[n - fb:n, :])
        lo = n - (i + 1) * fb
        return _bdot(perm_ref[...], ref[lo:lo + 2 * fb, :])

    for i in range(nblk):
        rows_i = slice(i * fb, (i + 1) * fb)
        zr = reversed_block(z_ref, i, seq)
        ze_ref[rows_i, :D_FOURIER] = (z_ref[rows_i, :D_FOURIER].astype(F32) + zr[:, :D_FOURIER]).astype(BF16)
        ze_ref[rows_i, D_FOURIER:] = (z_ref[rows_i, D_FOURIER:].astype(F32) - zr[:, D_FOURIER:]).astype(BF16)
    nyq = z_ref[half:half + 1, :D_FOURIER].astype(F32)
    yc = _bdot(c_ref[...], ze_ref[:, :D_FOURIER]) + altc_ref[...] * nyq
    ys = _bdot(s_ref[...], ze_ref[:, D_FOURIER:])
    y_ref[:half, :] = (yc - ys).astype(BF16)
    yr_ref[...] = (yc + ys).astype(BF16)
    y_mid = _bdot(altr_ref[...], z_ref[:, :D_FOURIER])[0:1, :]
    first_row = lax.broadcasted_iota(jnp.int32, (fb, 1), 0) == 0
    for i in range(nblk):
        blk = reversed_block(yr_ref, i, half)
        if i == 0:
            blk = jnp.where(first_row, y_mid, blk)
        y_ref[half + i * fb:half + (i + 1) * fb, :] = blk.astype(BF16)


def _fourier(mats, zf):
    cmat, smat, perm, alt_col, alt_row = mats
    seq = alt_row.shape[1]
    half = seq // 2
    nb = zf.shape[1] // (2 * D_FOURIER)
    const = lambda a: pl.BlockSpec(a.shape, lambda b: (0, 0))
    return pl.pallas_call(
        _fourier_kernel,
        grid=(nb,),
        in_specs=[const(cmat), const(smat), const(perm), const(alt_col), const(alt_row),
                  pl.BlockSpec((seq, 2 * D_FOURIER), lambda b: (0, b))],
        out_specs=pl.BlockSpec((seq, D_FOURIER), lambda b: (0, b)),
        out_shape=jax.ShapeDtypeStruct((seq, nb * D_FOURIER), BF16),
        scratch_shapes=[pltpu.VMEM((half, 2 * D_FOURIER), BF16), pltpu.VMEM((half, D_FOURIER), BF16)],
        compiler_params=_params("parallel"),
        name="fourier_pos",
    )(cmat, smat, perm, alt_col, alt_row, zf)


CONV_PAD = 16
CONV_CHUNK = 128


def _conv_kernel(u_ref, dww_ref, dwb_ref, lng_ref, lnb_ref, pww_ref, pwb_ref, o_ref, vs_ref):
    seq = u_ref.shape[0]
    n_pad = seq + 2 * CONV_PAD
    zeros = jnp.zeros((CONV_PAD, D_CONV), F32)
    vs_ref[0, 0:CONV_PAD, :] = zeros
    vs_ref[0, seq + CONV_PAD:n_pad, :] = zeros
    vs_ref[0, CONV_PAD:CONV_PAD + seq, :] = u_ref[:, :D_CONV] * _sigmoid(u_ref[:, D_CONV:])
    for s in range(1, SUBLANES):
        vs_ref[s, 0:n_pad - SUBLANES, :] = vs_ref[0, s:s + n_pad - SUBLANES, :]
    first = CONV_PAD - CONV_WIDTH // 2
    assert (first + CONV_WIDTH - 1) // SUBLANES * SUBLANES + seq <= n_pad - SUBLANES

    def chunk(ci, carry):
        base = pl.multiple_of(ci * CONV_CHUNK, CONV_CHUNK)
        acc = jnp.zeros((CONV_CHUNK, D_CONV), F32) + dwb_ref[...]
        for t in range(CONV_WIDTH):
            s, a = (first + t) % SUBLANES, (first + t) // SUBLANES
            acc = acc + vs_ref[s, pl.ds(base + a * SUBLANES, CONV_CHUNK), :] * dww_ref[t:t + 1, :]
        mu = jnp.mean(acc, axis=-1, keepdims=True)
        cen = acc - mu
        var = jnp.mean(cen * cen, axis=-1, keepdims=True)
        y = cen * lax.rsqrt(var + LN_EPS) * lng_ref[...] + lnb_ref[...]
        y = y * _sigmoid(y)
        o_ref[pl.ds(base, CONV_CHUNK), :] = (_bdot(y.astype(BF16), pww_ref[...]) + pwb_ref[...]).astype(BF16)
        return carry

    lax.fori_loop(0, seq // CONV_CHUNK, chunk, 0, unroll=2)


def _conv(uc, dww, dwb, lng, lnb, pww, pwb, *, seq):
    t = uc.shape[0]
    vec = pl.BlockSpec((1, D_CONV), lambda b: (0, 0))
    return pl.pallas_call(
        _conv_kernel,
        grid=(t // seq,),
        in_specs=[pl.BlockSpec((seq, 2 * D_CONV), lambda b: (b, 0)),
                  pl.BlockSpec((CONV_WIDTH, D_CONV), lambda b: (0, 0)),
                  vec, vec, vec,
                  pl.BlockSpec((D_CONV, D_CONV), lambda b: (0, 0)),
                  vec],
        out_specs=pl.BlockSpec((seq, D_CONV), lambda b: (b, 0)),
        out_shape=jax.ShapeDtypeStruct((t, D_CONV), BF16),
        scratch_shapes=[pltpu.VMEM((SUBLANES, seq + 2 * CONV_PAD, D_CONV), F32)],
        compiler_params=_params("parallel"),
        name="conv_module",
    )(uc, dww, dwb, lng, lnb, pww, pwb)


def _head_mask():
    return lax.broadcasted_iota(jnp.int32, (1, LANES), 1) < NA_HEAD_DIM


Q_ROWS = 2
BAND_ROWS = 10
assert BAND_ROWS >= WIN_ROWS + Q_ROWS - 1 and (BAND_ROWS * GRID_W) % LANES == 0


def _na_patterns(rows):
    starts, sigs = [], []
    for g in range(rows // Q_ROWS):
        start = int(np.clip(Q_ROWS * g - WIN_ROWS // 2, 0, rows - BAND_ROWS))
        r = Q_ROWS * g + np.arange(Q_ROWS)
        rs = np.clip(r - WIN_ROWS // 2, 0, rows - WIN_ROWS)
        starts.append(start)
        sigs.append((start - Q_ROWS * g,) + tuple(rs - r))
    run_starts = [g for g in range(len(sigs)) if g == 0 or sigs[g] != sigs[g - 1]]
    assert len(set(sigs)) == len(run_starts)
    return starts, run_starts


def _na_kernel(q_ref, k_ref, va_ref, vb_ref, kc_ref, vca_ref, vcb_ref, pair_ref, o_ref, s_ref, p_ref, bias_ref, *,
               seq, n_ctx):
    rows = seq // GRID_W
    per_batch = rows // Q_ROWS
    n_groups = q_ref.shape[0] // seq * per_batch
    n_q = Q_ROWS * GRID_W
    n_loc = BAND_ROWS * GRID_W
    _, run_starts = _na_patterns(rows)
    first = _head_mask()
    dn = (((1,), (1,)), ((), ()))
    assert n_groups >= 4 and seq == per_batch * n_q

    def split(t):
        t = jnp.asarray(t, jnp.int32)
        return t // per_batch, t % per_batch

    def band(t):
        b, g = split(t)
        start = jnp.clip(Q_ROWS * g - WIN_ROWS // 2, 0, rows - BAND_ROWS)
        return pl.multiple_of(b * seq + start * GRID_W, LANES)

    def q_start(t):
        return pl.multiple_of(jnp.asarray(t, jnp.int32) * n_q, n_q)

    def ctx_start(t):
        return pl.multiple_of(split(t)[0] * n_ctx, n_ctx)

    def scores(t, slot):
        g = split(t)[1]
        pat = sum((g >= s).astype(jnp.int32) for s in run_starts[1:])
        q = q_ref[pl.ds(q_start(t), n_q), :]
        kb = k_ref[pl.ds(band(t), n_loc), :]
        kc = kc_ref[pl.ds(ctx_start(t), n_ctx), :]
        for a in range(2):
            qa = jnp.where(first if a == 0 else ~first, q, jnp.zeros_like(q))
            s_ref[slot, a, :, :n_loc] = lax.dot_general(qa, kb, dn, preferred_element_type=F32) + bias_ref[a, pat]
            s_ref[slot, a, :, n_loc:] = lax.dot_general(qa, kc, dn, preferred_element_type=F32)

    def softmax(slot):
        for a in range(2):
            s = s_ref[slot, a]
            p_ref[slot, a] = jnp.exp((s - jnp.max(s, axis=-1, keepdims=True)).astype(BF16))

    def values(t, slot):
        outs = []
        for a, (v_ref, vc_ref) in enumerate(((va_ref, vca_ref), (vb_ref, vcb_ref))):
            o = (_bdot(p_ref[slot, a, :, :n_loc], v_ref[pl.ds(band(t), n_loc), :])
                 + _bdot(p_ref[slot, a, :, n_loc:], vc_ref[pl.ds(ctx_start(t), n_ctx), :]))
            outs.append(o * (1.0 / pltpu.roll(o, NA_HEAD_DIM, axis=1)))
        o_ref[pl.ds(q_start(t), n_q), :] = jnp.where(first, outs[0], outs[1]).astype(BF16)

    def tick(t, parity):
        values(t - 1, 1 - parity)
        scores(t + 1, 1 - parity)
        softmax(parity)

    def quad(j, carry):
        t = 4 * j + 1
        tick(t, 1)
        tick(t + 1, 0)
        tick(t + 2, 1)
        tick(t + 3, 0)
        return carry

    @pl.when(pl.program_id(1) == 0)
    def _():
        _na_assemble_bias(pair_ref, bias_ref, rows)

    scores(0, 0)
    scores(1, 1)
    softmax(0)
    n_quads = (n_groups - 2) // 4
    lax.fori_loop(0, n_quads, quad, 0)
    for t in range(4 * n_quads + 1, n_groups - 1):
        tick(t, t % 2)
    softmax((n_groups - 1) % 2)
    values(n_groups - 2, n_groups % 2)
    values(n_groups - 1, (n_groups - 1) % 2)


def _na_bias(rpb):
    n_heads, n_dr, n_dc = rpb.shape
    lead = GRID_W - WIN_COLS
    wpad = jnp.pad(rpb.astype(F32), ((0, 0), (0, 0), (lead, 2 * GRID_W - n_dc - lead)))
    flat = jnp.tile(wpad, (1, 1, GRID_W))[..., :GRID_W * (2 * GRID_W - 1)]
    toep = flat.reshape(n_heads, n_dr, GRID_W, 2 * GRID_W - 1)[..., GRID_W - 1:]
    col = np.arange(GRID_W)
    col_start = np.clip(col - WIN_COLS // 2, 0, GRID_W - WIN_COLS)
    col_valid = (col[None, :] >= col_start[:, None]) & (col[None, :] < col_start[:, None] + WIN_COLS)
    assert np.all(np.abs(col[None, :] - col[:, None])[col_valid] <= WIN_COLS - 1)
    toep = jnp.where(jnp.asarray(col_valid), toep, -jnp.inf)
    ext = jnp.pad(toep, ((0, 0), (1, 1), (0, 0), (0, 0)), constant_values=-jnp.inf)
    pairs = jnp.concatenate([ext[:, :-1], ext[:, 1:]], axis=-1)
    return pairs.reshape(HEAD_PAIRS, 2, n_dr + 1, GRID_W, 2 * GRID_W)


def _na_assemble_bias(pair_ref, bias_ref, rows):
    starts, run_starts = _na_patterns(rows)
    low = lax.broadcasted_iota(jnp.int32, (GRID_W, LANES), 1) < GRID_W
    masked = jnp.full((GRID_W, LANES), -jnp.inf, F32)
    for a in range(2):
        for p, g in enumerate(run_starts):
            for qi in range(Q_ROWS):
                r = Q_ROWS * g + qi
                rs = int(np.clip(r - WIN_ROWS // 2, 0, rows - WIN_ROWS))
                below = rs - starts[g]
                dr0 = rs - r + WIN_ROWS - 1
                for t in range(BAND_ROWS // 2):
                    d0 = dr0 + 2 * t - below
                    v0 = below <= 2 * t < below + WIN_ROWS
                    v1 = below <= 2 * t + 1 < below + WIN_ROWS
                    if v0 and v1:
                        tile = pair_ref[0, a, d0 + 1]
                    elif v0:
                        tile = jnp.where(low, pair_ref[0, a, d0 + 1], masked)
                    elif v1:
                        tile = jnp.where(low, masked, pair_ref[0, a, d0 + 1])
                    else:
                        tile = masked
                    bias_ref[a, p, qi * GRID_W:(qi + 1) * GRID_W, t * LANES:(t + 1) * LANES] = tile


NA_BATCHES = 4


def _na(qkv, kv_ctx, pairs, *, seq, n_ctx, kc_off):
    t = qkv.shape[0]
    hp = HEAD_PAIRS
    tb, tbc = NA_BATCHES * seq, NA_BATCHES * n_ctx
    n_pat = len(_na_patterns(seq // GRID_W)[1])
    n_keys = BAND_ROWS * GRID_W + n_ctx
    assert t % tb == 0
    return pl.pallas_call(
        functools.partial(_na_kernel, seq=seq, n_ctx=n_ctx),
        grid=(hp, t // tb),
        in_specs=[pl.BlockSpec((tb, LANES), lambda h, b: (b, h)),
                  pl.BlockSpec((tb, LANES), lambda h, b: (b, hp + h)),
                  pl.BlockSpec((tb, LANES), lambda h, b: (b, 2 * hp + h)),
                  pl.BlockSpec((tb, LANES), lambda h, b: (b, 3 * hp + h)),
                  pl.BlockSpec((tbc, LANES), lambda h, b: (b, kc_off + h)),
                  pl.BlockSpec((tbc, LANES), lambda h, b: (b, kc_off + hp + h)),
                  pl.BlockSpec((tbc, LANES), lambda h, b: (b, kc_off + 2 * hp + h)),
                  pl.BlockSpec((1,) + pairs.shape[1:], lambda h, b: (h, 0, 0, 0, 0))],
        out_specs=pl.BlockSpec((tb, LANES), lambda h, b: (b, h)),
        out_shape=jax.ShapeDtypeStruct((t, D_NA), BF16),
        scratch_shapes=[pltpu.VMEM((2, 2, Q_ROWS * GRID_W, n_keys), F32),
                        pltpu.VMEM((2, 2, Q_ROWS * GRID_W, n_keys), BF16),
                        pltpu.VMEM((2, n_pat, Q_ROWS * GRID_W, BAND_ROWS * GRID_W), F32)],
        compiler_params=_params("parallel", "arbitrary"),
        name="na_attention",
    )(qkv, qkv, qkv, qkv, kv_ctx, kv_ctx, kv_ctx, pairs)


def _ctx_attn_kernel(q_ref, k_ref, va_ref, vb_ref, o_ref):
    first = _head_mask()
    for h in range(HEAD_PAIRS):
        lanes = slice(h * LANES, (h + 1) * LANES)
        q = q_ref[:, lanes]
        k = k_ref[:, lanes]
        outs = []
        for a, v_ref in enumerate((va_ref, vb_ref)):
            qa = jnp.where(first if a == 0 else ~first, q, jnp.zeros_like(q))
            s = lax.dot_general(qa, k, (((1,), (1,)), ((), ())), preferred_element_type=F32)
            p = jnp.exp(s - jnp.max(s, axis=-1, keepdims=True))
            den = jnp.sum(p, axis=-1, keepdims=True)
            outs.append(_bdot(p.astype(BF16), v_ref[:, lanes]) * (1.0 / den))
        o_ref[:, lanes] = jnp.where(first, outs[0], outs[1]).astype(BF16)


def _ctx_attn(qkv, *, n_ctx):
    t = qkv.shape[0]
    return pl.pallas_call(
        _ctx_attn_kernel,
        grid=(t // n_ctx,),
        in_specs=[pl.BlockSpec((n_ctx, D_NA), lambda b: (b, 0)),
                  pl.BlockSpec((n_ctx, D_NA), lambda b: (b, 1)),
                  pl.BlockSpec((n_ctx, D_NA), lambda b: (b, 2)),
                  pl.BlockSpec((n_ctx, D_NA), lambda b: (b, 3))],
        out_specs=pl.BlockSpec((n_ctx, D_NA), lambda b: (b, 0)),
        out_shape=jax.ShapeDtypeStruct((t, D_NA), BF16),
        compiler_params=_params("parallel"),
        name="ctx_attention",
    )(qkv, qkv, qkv, qkv)


def _outmlp_kernel(h_ref, yf_ref, cv_ref, at_ref, mod_ref, g_ref, gf_ref, wo_ref, w1_ref, w2_ref, o_ref, *, final_norm):
    mix = (_bdot(yf_ref[...], wo_ref[:D_FOURIER, :])
           + _bdot(cv_ref[...], wo_ref[D_FOURIER:D_FOURIER + D_CONV, :])
           + _bdot(at_ref[...], wo_ref[D_FOURIER + D_CONV:, :]))
    h1 = h_ref[...] + mod_ref[0, 2:3, :] * mix
    hn = _rms_mod(h1, g_ref[...], mod_ref[0, 3:4, :], mod_ref[0, 4:5, :]).astype(BF16)
    a = jnp.maximum(_bdot(hn, w1_ref[...]), 0.0)
    out = h1 + mod_ref[0, 5:6, :] * _bdot((a * a).astype(BF16), w2_ref[...])
    if final_norm:
        out = out * lax.rsqrt(jnp.mean(out * out, axis=-1, keepdims=True) + RMS_EPS) * gf_ref[...]
    o_ref[...] = out


def _outmlp(h, yf, cv, at, mods, g2, gf, wo, w1, w2, *, seq, tm, ctx, final_norm):
    t, d = h.shape
    nb = t // seq
    per = seq // tm
    row = (lambda i: nb) if ctx else (lambda i: i // per)
    d_ff = w1.shape[1]
    vec = pl.BlockSpec((1, d), lambda i: (0, 0))
    once = dict(pipeline_mode=pl.Buffered(1))
    return pl.pallas_call(
        functools.partial(_outmlp_kernel, final_norm=final_norm),
        grid=(t // tm,),
        in_specs=[pl.BlockSpec((tm, d), lambda i: (i, 0)),
                  pl.BlockSpec((tm, D_FOURIER), lambda i: (i % per, i // per)),
                  pl.BlockSpec((tm, D_CONV), lambda i: (i, 0)),
                  pl.BlockSpec((tm, D_NA), lambda i: (i, 0)),
                  pl.BlockSpec((1, N_MOD, d), lambda i: (row(i), 0, 0)),
                  vec, vec,
                  pl.BlockSpec((D_MIX, d), lambda i: (0, 0), **once),
                  pl.BlockSpec((d, d_ff), lambda i: (0, 0), **once),
                  pl.BlockSpec((d_ff, d), lambda i: (0, 0), **once)],
        out_specs=pl.BlockSpec((tm, d), lambda i: (i, 0)),
        out_shape=jax.ShapeDtypeStruct((t, d), F32),
        compiler_params=_params("parallel"),
        name="outmlp_ctx" if ctx else "outmlp_lat",
    )(h, yf, cv, at, mods, g2, gf, wo, w1, w2)


def kernel(x, c, ctx, c_ctx, ada_w, ada_b, norm1_g, norm2_g, w_in, w_fourier, conv_dw_w, conv_dw_b, conv_norm_g,
           conv_norm_b, conv_pw_w, conv_pw_b, na_rpb, w_out, mlp_w1, mlp_w2, final_norm_g):
    nb, seq, d = x.shape
    n_ctx = ctx.shape[1]
    depth = ada_w.shape[0]
    assert nb < MOD_ROWS and seq % GRID_W == 0 and d == D_MODEL

    cc = jnp.concatenate([c, c_ctx[None], jnp.zeros((MOD_ROWS - nb - 1, d), F32)], axis=0)
    mods = _adaln(cc, ada_w, ada_b).reshape(depth, MOD_ROWS, N_MOD, d)
    ab_lat = _fold_fourier(w_fourier, seq)
    ab_ctx = _fold_fourier(w_fourier, n_ctx)
    dft_lat = _dft_mats(seq)
    dft_ctx = _dft_mats(n_ctx)

    w_in_b = w_in.astype(BF16)
    w_out_b = w_out.astype(BF16)
    w1_b = mlp_w1.astype(BF16)
    w2_b = mlp_w2.astype(BF16)
    pww_b = conv_pw_w.astype(BF16)
    gf = final_norm_g.reshape(1, d)

    h_lat = x.reshape(nb * seq, d)
    h_ctx = ctx.reshape(nb * n_ctx, d)
    for i in range(depth):
        last = i == depth - 1
        g1 = norm1_g[i].reshape(1, d)
        g2 = norm2_g[i].reshape(1, d)
        conv_p = (conv_dw_w[i], conv_dw_b[i].reshape(1, -1), conv_norm_g[i].reshape(1, -1),
                  conv_norm_b[i].reshape(1, -1), pww_b[i], conv_pw_b[i].reshape(1, -1))
        bias = _na_bias(na_rpb[i])

        if last:
            kv_ctx = _kvproj(h_ctx, g1, mods[i], w_in_b[i], nb=nb, tm=n_ctx)
            kc_off = 0
        else:
            zf_c, uc_c, qkv_c = _inproj(h_ctx, g1, mods[i], w_in_b[i], ab_ctx[i], seq=n_ctx, tm=n_ctx, ctx=True)
            kv_ctx = qkv_c
            kc_off = HEAD_PAIRS

        zf, uc, qkv = _inproj(h_lat, g1, mods[i], w_in_b[i], ab_lat[i], seq=seq, tm=1024, ctx=False)
        yf = _fourier(dft_lat, zf)
        cv = _conv(uc, *conv_p, seq=seq)
        at = _na(qkv, kv_ctx, bias, seq=seq, n_ctx=n_ctx, kc_off=kc_off)
        h_lat = _outmlp(h_lat, yf, cv, at, mods[i], g2, gf, w_out_b[i], w1_b[i], w2_b[i],
                        seq=seq, tm=512, ctx=False, final_norm=last)

        if not last:
            yf_c = _fourier(dft_ctx, zf_c)
            cv_c = _conv(uc_c, *conv_p, seq=n_ctx)
            at_c = _ctx_attn(qkv_c, n_ctx=n_ctx)
            h_ctx = _outmlp(h_ctx, yf_c, cv_c, at_c, mods[i], g2, gf, w_out_b[i], w1_b[i], w2_b[i],
                            seq=n_ctx, tm=n_ctx, ctx=True, final_norm=False)
    return h_lat.reshape(nb, seq, d)
```

```python
import functools

import numpy as np
import jax
import jax.numpy as jnp
from jax import lax
from jax.experimental import pallas as pl
from jax.experimental.pallas import tpu as pltpu

D_MODEL = 1024
GRID_W = 64
D_FOURIER = 256
FOURIER_GROUPS = 4
D_CONV = 256
CONV_WIDTH = 31
N_NA_HEADS = 8
NA_HEAD_DIM = 64
D_NA = N_NA_HEADS * NA_HEAD_DIM
WIN_ROWS = 8
WIN_COLS = 16
QKV_START = D_FOURIER + 2 * D_CONV
KV_START = QKV_START + D_NA
D_MIX = D_FOURIER + D_CONV + D_NA
N_MOD = 6
RMS_EPS = 1e-6
LN_EPS = 1e-5

LANES = 128
SUBLANES = 8
HEAD_PAIRS = N_NA_HEADS * NA_HEAD_DIM // LANES
MOD_ROWS = 16
VMEM_LIMIT = 56 * 1024 * 1024

F32 = jnp.float32
BF16 = jnp.bfloat16


def _params(*sem):
    return pltpu.CompilerParams(dimension_semantics=sem, vmem_limit_bytes=VMEM_LIMIT)


def _sigmoid(x):
    return 1.0 / (1.0 + jnp.exp(-x))


def _rms_mod(x, g, shift, scale):
    y = x * lax.rsqrt(jnp.mean(x * x, axis=-1, keepdims=True) + RMS_EPS) * g
    return y * (1.0 + scale) + shift


def _bdot(a, b):
    return jnp.dot(a, b, preferred_element_type=F32)


def _adaln_kernel(c_ref, w_ref, b_ref, o_ref):
    cc = c_ref[...]
    a = cc * _sigmoid(cc)
    w = w_ref[0]
    a_hi = a.astype(BF16)
    a_lo = (a - a_hi.astype(F32)).astype(BF16)
    w_hi = w.astype(BF16)
    w_lo = (w - w_hi.astype(F32)).astype(BF16)
    by_hi = _bdot(jnp.concatenate([a_hi, a_lo], axis=0), w_hi)
    o_ref[0] = by_hi[:MOD_ROWS] + by_hi[MOD_ROWS:] + _bdot(a_hi, w_lo) + b_ref[0]


def _adaln(cc, ada_w, ada_b):
    depth, d, n = ada_w.shape
    tn = 1024
    return pl.pallas_call(
        _adaln_kernel,
        grid=(depth, n // tn),
        in_specs=[pl.BlockSpec((MOD_ROWS, d), lambda l, j: (0, 0)),
                  pl.BlockSpec((1, d, tn), lambda l, j: (l, 0, j)),
                  pl.BlockSpec((1, 1, tn), lambda l, j: (l, 0, j))],
        out_specs=pl.BlockSpec((1, MOD_ROWS, tn), lambda l, j: (l, 0, j)),
        out_shape=jax.ShapeDtypeStruct((depth, MOD_ROWS, n), F32),
        compiler_params=_params("parallel", "parallel"),
        name="adaln",
    )(cc, ada_w, ada_b.reshape(depth, 1, n))


def _fold_kernel(cc_ref, sc_ref, w_ref, o_ref):
    w = w_ref[0]
    hi = lax.Precision.HIGHEST
    o_ref[0, :, :D_FOURIER] = jnp.dot(cc_ref[...], w, preferred_element_type=F32, precision=hi).astype(BF16)
    o_ref[0, :, D_FOURIER:] = jnp.dot(sc_ref[...], w, preferred_element_type=F32, precision=hi).astype(BF16)


def _fold_fourier(w_fourier, seq):
    depth = w_fourier.shape[0]
    gs = D_FOURIER // FOURIER_GROUPS
    idx = np.arange(D_FOURIER)
    same = (idx[:, None] // gs) == (idx[None, :] // gs)
    ang = 2.0 * np.pi * (((idx[:, None] % gs) * (idx[None, :] % gs)) % gs) / gs
    scale = 1.0 / np.sqrt(seq * gs)
    cc = jnp.asarray(np.where(same, np.cos(ang), 0.0) * scale, F32)
    sc = jnp.asarray(np.where(same, np.sin(ang), 0.0) * scale, F32)
    full = pl.BlockSpec((D_FOURIER, D_FOURIER), lambda l: (0, 0))
    return pl.pallas_call(
        _fold_kernel,
        grid=(depth,),
        in_specs=[full, full, pl.BlockSpec((1, D_FOURIER, D_FOURIER), lambda l: (l, 0, 0))],
        out_specs=pl.BlockSpec((1, D_FOURIER, 2 * D_FOURIER), lambda l: (l, 0, 0)),
        out_shape=jax.ShapeDtypeStruct((depth, D_FOURIER, 2 * D_FOURIER), BF16),
        compiler_params=_params("parallel"),
        name="fold_fourier",
    )(cc, sc, w_fourier)


FLIP_BLOCK = 128


def _dft_mats(seq):
    half = seq // 2
    k = np.arange(half)
    ang = 2.0 * np.pi * ((k[:, None] * k[None, :]) % seq) / seq
    r = np.arange(FLIP_BLOCK)
    perm = (np.arange(2 * FLIP_BLOCK)[None, :] == FLIP_BLOCK - r[:, None]).astype(np.float32)
    alt_col = np.where(k % 2 == 0, 1.0, -1.0).astype(np.float32)[:, None]
    alt_row = np.zeros((SUBLANES, seq), np.float32)
    alt_row[0] = np.where(np.arange(seq) % 2 == 0, 1.0, -1.0)
    return (jnp.asarray(np.cos(ang), F32).astype(BF16), jnp.asarray(np.sin(ang), F32).astype(BF16),
            jnp.asarray(perm, F32).astype(BF16), jnp.asarray(alt_col), jnp.asarray(alt_row, F32).astype(BF16))


def _store_kv(hn, w_ref, out_ref, col):
    kv = _bdot(hn, w_ref[:, KV_START:])
    v = kv[:, D_NA:]
    first = lax.broadcasted_iota(jnp.int32, (1, D_NA), 1) % LANES < NA_HEAD_DIM
    out_ref[:, col:col + D_NA] = kv[:, :D_NA].astype(BF16)
    out_ref[:, col + D_NA:col + 2 * D_NA] = jnp.where(first, v, 1.0).astype(BF16)
    out_ref[:, col + 2 * D_NA:col + 3 * D_NA] = jnp.where(first, 1.0, v).astype(BF16)


def _inproj_kernel(x_ref, g_ref, mod_ref, w_ref, ab_ref, zf_ref, uc_ref, qkv_ref):
    hn = _rms_mod(x_ref[...], g_ref[...], mod_ref[0, 0:1, :], mod_ref[0, 1:2, :]).astype(BF16)
    uf = _bdot(hn, w_ref[:, :D_FOURIER])
    uc_ref[...] = _bdot(hn, w_ref[:, D_FOURIER:QKV_START])
    qkv_ref[:, :D_NA] = (_bdot(hn, w_ref[:, QKV_START:KV_START]) * (NA_HEAD_DIM ** -0.5)).astype(BF16)
    _store_kv(hn, w_ref, qkv_ref, D_NA)
    zf_ref[...] = _bdot(uf.astype(BF16), ab_ref[...]).astype(BF16)


def _inproj(h, g, mods, w_in, ab, *, seq, tm, ctx):
    t, d = h.shape
    nb = t // seq
    per = seq // tm
    row = (lambda i: nb) if ctx else (lambda i: i // per)
    d_in = w_in.shape[1]
    return pl.pallas_call(
        _inproj_kernel,
        grid=(t // tm,),
        in_specs=[pl.BlockSpec((tm, d), lambda i: (i, 0)),
                  pl.BlockSpec((1, d), lambda i: (0, 0)),
                  pl.BlockSpec((1, N_MOD, d), lambda i: (row(i), 0, 0)),
                  pl.BlockSpec((d, d_in), lambda i: (0, 0), pipeline_mode=pl.Buffered(1)),
                  pl.BlockSpec((D_FOURIER, 2 * D_FOURIER), lambda i: (0, 0))],
        out_specs=[pl.BlockSpec((tm, 2 * D_FOURIER), lambda i: (i % per, i // per)),
                   pl.BlockSpec((tm, 2 * D_CONV), lambda i: (i, 0)),
                   pl.BlockSpec((tm, 4 * D_NA), lambda i: (i, 0))],
        out_shape=[jax.ShapeDtypeStruct((seq, nb * 2 * D_FOURIER), BF16),
                   jax.ShapeDtypeStruct((t, 2 * D_CONV), F32),
                   jax.ShapeDtypeStruct((t, 4 * D_NA), BF16)],
        compiler_params=_params("parallel"),
        name="inproj_ctx" if ctx else "inproj_lat",
    )(h, g, mods, w_in, ab)


def _kvproj_kernel(x_ref, g_ref, mod_ref, w_ref, kv_ref):
    hn = _rms_mod(x_ref[...], g_ref[...], mod_ref[0, 0:1, :], mod_ref[0, 1:2, :]).astype(BF16)
    _store_kv(hn, w_ref, kv_ref, 0)


def _kvproj(h, g, mods, w_in, *, nb, tm):
    t, d = h.shape
    d_in = w_in.shape[1]
    n = 3 * D_NA
    return pl.pallas_call(
        _kvproj_kernel,
        grid=(t // tm,),
        in_specs=[pl.BlockSpec((tm, d), lambda i: (i, 0)),
                  pl.BlockSpec((1, d), lambda i: (0, 0)),
                  pl.BlockSpec((1, N_MOD, d), lambda i: (nb, 0, 0)),
                  pl.BlockSpec((d, d_in), lambda i: (0, 0))],
        out_specs=pl.BlockSpec((tm, n), lambda i: (i, 0)),
        out_shape=jax.ShapeDtypeStruct((t, n), BF16),
        compiler_params=_params("parallel"),
        name="kvproj_ctx",
    )(h, g, mods, w_in)


def _fourier_kernel(c_ref, s_ref, perm_ref, altc_ref, altr_ref, z_ref, y_ref, ze_ref, yr_ref):
    seq = z_ref.shape[0]
    half = seq // 2
    nblk = half // FLIP_BLOCK
    fb = FLIP_BLOCK

    def reversed_block(ref, i, n):
        if i == 0:
            return _bdot(perm_ref[:, :fb], ref[n - fb:n, :])
        lo = n - (i + 1) * fb
        return _bdot(perm_ref[...], ref[lo:lo + 2 * fb, :])

    for i in range(nblk):
        rows_i = slice(i * fb, (i + 1) * fb)
        zr = reversed_block(z_ref, i, seq)
        ze_ref[rows_i, :D_FOURIER] = (z_ref[rows_i, :D_FOURIER].astype(F32) + zr[:, :D_FOURIER]).astype(BF16)
        ze_ref[rows_i, D_FOURIER:] = (z_ref[rows_i, D_FOURIER:].astype(F32) - zr[:, D_FOURIER:]).astype(BF16)
    nyq = z_ref[half:half + 1, :D_FOURIER].astype(F32)
    yc = _bdot(c_ref[...], ze_ref[:, :D_FOURIER]) + altc_ref[...] * nyq
    ys = _bdot(s_ref[...], ze_ref[:, D_FOURIER:])
    y_ref[:half, :] = (yc - ys).astype(BF16)
    yr_ref[...] = (yc + ys).astype(BF16)
    y_mid = _bdot(altr_ref[...], z_ref[:, :D_FOURIER])[0:1, :]
    first_row = lax.broadcasted_iota(jnp.int32, (fb, 1), 0) == 0
    for i in range(nblk):
        blk = reversed_block(yr_ref, i, half)
        if i == 0:
            blk = jnp.where(first_row, y_mid, blk)
        y_ref[half + i * fb:half + (i + 1) * fb, :] = blk.astype(BF16)


def _fourier(mats, zf):
    cmat, smat, perm, alt_col, alt_row = mats
    seq = alt_row.shape[1]
    half = seq // 2
    nb = zf.shape[1] // (2 * D_FOURIER)
    const = lambda a: pl.BlockSpec(a.shape, lambda b: (0, 0))
    return pl.pallas_call(
        _fourier_kernel,
        grid=(nb,),
        in_specs=[const(cmat), const(smat), const(perm), const(alt_col), const(alt_row),
                  pl.BlockSpec((seq, 2 * D_FOURIER), lambda b: (0, b))],
        out_specs=pl.BlockSpec((seq, D_FOURIER), lambda b: (0, b)),
        out_shape=jax.ShapeDtypeStruct((seq, nb * D_FOURIER), BF16),
        scratch_shapes=[pltpu.VMEM((half, 2 * D_FOURIER), BF16), pltpu.VMEM((half, D_FOURIER), BF16)],
        compiler_params=_params("parallel"),
        name="fourier_pos",
    )(cmat, smat, perm, alt_col, alt_row, zf)


CONV_PAD = 16
CONV_CHUNK = 512


def _conv_kernel(u_ref, dww_ref, dwb_ref, lng_ref, lnb_ref, pww_ref, pwb_ref, o_ref, vs_ref):
    seq = u_ref.shape[0]
    n_pad = seq + 2 * CONV_PAD
    zeros = jnp.zeros((CONV_PAD, D_CONV), F32)
    vs_ref[0, 0:CONV_PAD, :] = zeros
    vs_ref[0, seq + CONV_PAD:n_pad, :] = zeros
    vs_ref[0, CONV_PAD:CONV_PAD + seq, :] = u_ref[:, :D_CONV] * _sigmoid(u_ref[:, D_CONV:])
    for s in range(1, SUBLANES):
        vs_ref[s, 0:n_pad - SUBLANES, :] = vs_ref[0, s:s + n_pad - SUBLANES, :]
    first = CONV_PAD - CONV_WIDTH // 2
    assert (first + CONV_WIDTH - 1) // SUBLANES * SUBLANES + seq <= n_pad - SUBLANES

    rows = min(CONV_CHUNK, seq)

    def chunk(ci, carry):
        base = pl.multiple_of(ci * rows, rows)
        acc = jnp.zeros((rows, D_CONV), F32) + dwb_ref[...]
        for t in range(CONV_WIDTH):
            s, a = (first + t) % SUBLANES, (first + t) // SUBLANES
            acc = acc + vs_ref[s, pl.ds(base + a * SUBLANES, rows), :] * dww_ref[t:t + 1, :]
        mu = jnp.mean(acc, axis=-1, keepdims=True)
        cen = acc - mu
        var = jnp.mean(cen * cen, axis=-1, keepdims=True)
        y = cen * lax.rsqrt(var + LN_EPS) * lng_ref[...] + lnb_ref[...]
        y = y * _sigmoid(y)
        o_ref[pl.ds(base, rows), :] = (_bdot(y.astype(BF16), pww_ref[...]) + pwb_ref[...]).astype(BF16)
        return carry

    lax.fori_loop(0, seq // rows, chunk, 0)


def _conv(uc, dww, dwb, lng, lnb, pww, pwb, *, seq):
    t = uc.shape[0]
    vec = pl.BlockSpec((1, D_CONV), lambda b: (0, 0))
    return pl.pallas_call(
        _conv_kernel,
        grid=(t // seq,),
        in_specs=[pl.BlockSpec((seq, 2 * D_CONV), lambda b: (b, 0)),
                  pl.BlockSpec((CONV_WIDTH, D_CONV), lambda b: (0, 0)),
                  vec, vec, vec,
                  pl.BlockSpec((D_CONV, D_CONV), lambda b: (0, 0)),
                  vec],
        out_specs=pl.BlockSpec((seq, D_CONV), lambda b: (b, 0)),
        out_shape=jax.ShapeDtypeStruct((t, D_CONV), BF16),
        scratch_shapes=[pltpu.VMEM((SUBLANES, seq + 2 * CONV_PAD, D_CONV), F32)],
        compiler_params=_params("parallel"),
        name="conv_module",
    )(uc, dww, dwb, lng, lnb, pww, pwb)


def _head_mask():
    return lax.broadcasted_iota(jnp.int32, (1, LANES), 1) < NA_HEAD_DIM


Q_ROWS = 2
BAND_ROWS = 10
assert BAND_ROWS >= WIN_ROWS + Q_ROWS - 1 and (BAND_ROWS * GRID_W) % LANES == 0


def _na_patterns(rows):
    starts, sigs = [], []
    for g in range(rows // Q_ROWS):
        start = int(np.clip(Q_ROWS * g - WIN_ROWS // 2, 0, rows - BAND_ROWS))
        r = Q_ROWS * g + np.arange(Q_ROWS)
        rs = np.clip(r - WIN_ROWS // 2, 0, rows - WIN_ROWS)
        starts.append(start)
        sigs.append((start - Q_ROWS * g,) + tuple(rs - r))
    run_starts = [g for g in range(len(sigs)) if g == 0 or sigs[g] != sigs[g - 1]]
    assert len(set(sigs)) == len(run_starts)
    return starts, run_starts


def _na_kernel(q_ref, k_ref, va_ref, vb_ref, kc_ref, vca_ref, vcb_ref, rpb_ref, o_ref, s_ref, p_ref, bias_ref, *,
               seq, n_ctx):
    rows = seq // GRID_W
    per_batch = rows // Q_ROWS
    n_groups = q_ref.shape[0] // seq * per_batch
    n_q = Q_ROWS * GRID_W
    n_loc = BAND_ROWS * GRID_W
    _, run_starts = _na_patterns(rows)
    first = _head_mask()
    dn = (((1,), (1,)), ((), ()))
    assert n_groups >= 4 and seq == per_batch * n_q

    def split(t):
        t = jnp.asarray(t, jnp.int32)
        return t // per_batch, t % per_batch

    def band(t):
        b, g = split(t)
        start = jnp.clip(Q_ROWS * g - WIN_ROWS // 2, 0, rows - BAND_ROWS)
        return pl.multiple_of(b * seq + start * GRID_W, LANES)

    def q_start(t):
        return pl.multiple_of(jnp.asarray(t, jnp.int32) * n_q, n_q)

    def ctx_start(t):
        return pl.multiple_of(split(t)[0] * n_ctx, n_ctx)

    def scores(t, slot):
        g = split(t)[1]
        pat = sum((g >= s).astype(jnp.int32) for s in run_starts[1:])
        q = q_ref[pl.ds(q_start(t), n_q), :]
        kb = k_ref[pl.ds(band(t), n_loc), :]
        kc = kc_ref[pl.ds(ctx_start(t), n_ctx), :]
        for a in range(2):
            qa = jnp.where(first if a == 0 else ~first, q, jnp.zeros_like(q))
            s_ref[slot, a, :, :n_loc] = lax.dot_general(qa, kb, dn, preferred_element_type=F32) + bias_ref[a, pat]
            s_ref[slot, a, :, n_loc:] = lax.dot_general(qa, kc, dn, preferred_element_type=F32)

    def softmax(slot):
        for a in range(2):
            s = s_ref[slot, a]
            p_ref[slot, a] = jnp.exp((s - jnp.max(s, axis=-1, keepdims=True)).astype(BF16))

    def values(t, slot):
        outs = []
        for a, (v_ref, vc_ref) in enumerate(((va_ref, vca_ref), (vb_ref, vcb_ref))):
            o = (_bdot(p_ref[slot, a, :, :n_loc], v_ref[pl.ds(band(t), n_loc), :])
                 + _bdot(p_ref[slot, a, :, n_loc:], vc_ref[pl.ds(ctx_start(t), n_ctx), :]))
            outs.append(o * (1.0 / pltpu.roll(o, NA_HEAD_DIM, axis=1)))
        o_ref[pl.ds(q_start(t), n_q), :] = jnp.where(first, outs[0], outs[1]).astype(BF16)

    def tick(t, parity):
        values(t - 1, 1 - parity)
        scores(t + 1, 1 - parity)
        softmax(parity)

    def quad(j, carry):
        t = 4 * j + 1
        tick(t, 1)
        tick(t + 1, 0)
        tick(t + 2, 1)
        tick(t + 3, 0)
        return carry

    @pl.when(pl.program_id(1) == 0)
    def _():
        _na_assemble_bias(rpb_ref, bias_ref, rows)

    scores(0, 0)
    scores(1, 1)
    softmax(0)
    n_quads = (n_groups - 2) // 4
    lax.fori_loop(0, n_quads, quad, 0)
    for t in range(4 * n_quads + 1, n_groups - 1):
        tick(t, t % 2)
    softmax((n_groups - 1) % 2)
    values(n_groups - 2, n_groups % 2)
    values(n_groups - 1, (n_groups - 1) % 2)


def _na_bias(rpb):
    n_heads, n_dr, n_dc = rpb.shape
    r = rpb.astype(F32)
    w = jnp.concatenate([r[..., WIN_COLS - 1:], jnp.zeros((n_heads, n_dr, LANES - n_dc), F32), r[..., :WIN_COLS - 1]],
                        axis=-1)
    w = jnp.pad(w, ((0, 0), (0, 1), (0, 0)))
    return w.reshape(HEAD_PAIRS, 2, n_dr + 1, LANES)


def _na_assemble_bias(w_ref, bias_ref, rows):
    starts, run_starts = _na_patterns(rows)
    cq = lax.broadcasted_iota(jnp.int32, (GRID_W, LANES), 0)
    lane = lax.broadcasted_iota(jnp.int32, (GRID_W, LANES), 1)
    low = lane < GRID_W
    ck = jnp.where(low, lane, lane - GRID_W)
    col_start = jnp.clip(cq - WIN_COLS // 2, 0, GRID_W - WIN_COLS)
    in_window = (ck >= col_start) & (ck < col_start + WIN_COLS)
    masked = jnp.full((GRID_W, LANES), -jnp.inf, F32)
    rolled = {}

    def half(a, d, upper):
        if (a, d, upper) not in rolled:
            row = jnp.broadcast_to(w_ref[0, a, d:d + 1, :], (GRID_W, LANES))
            rolled[a, d, upper] = pltpu.roll(row, GRID_W if upper else 0, axis=1, stride=1, stride_axis=0)
        return rolled[a, d, upper]

    for a in range(2):
        for p, g in enumerate(run_starts):
            for qi in range(Q_ROWS):
                r = Q_ROWS * g + qi
                rs = int(np.clip(r - WIN_ROWS // 2, 0, rows - WIN_ROWS))
                below = rs - starts[g]
                dr0 = rs - r + WIN_ROWS - 1
                for t in range(BAND_ROWS // 2):
                    d0 = dr0 + 2 * t - below
                    v0 = below <= 2 * t < below + WIN_ROWS
                    v1 = below <= 2 * t + 1 < below + WIN_ROWS
                    if v0 and v1:
                        tile = jnp.where(low, half(a, d0, False), half(a, d0 + 1, True))
                    elif v0:
                        tile = jnp.where(low, half(a, d0, False), masked)
                    elif v1:
                        tile = jnp.where(low, masked, half(a, d0 + 1, True))
                    else:
                        tile = masked
                    if v0 or v1:
                        tile = jnp.where(in_window, tile, masked)
                    bias_ref[a, p, qi * GRID_W:(qi + 1) * GRID_W, t * LANES:(t + 1) * LANES] = tile


NA_BATCHES = 4


def _na(qkv, kv_ctx, rpb_rows, *, seq, n_ctx, kc_off):
    t = qkv.shape[0]
    hp = HEAD_PAIRS
    tb, tbc = NA_BATCHES * seq, NA_BATCHES * n_ctx
    n_pat = len(_na_patterns(seq // GRID_W)[1])
    n_keys = BAND_ROWS * GRID_W + n_ctx
    assert t % tb == 0
    return pl.pallas_call(
        functools.partial(_na_kernel, seq=seq, n_ctx=n_ctx),
        grid=(hp, t // tb),
        in_specs=[pl.BlockSpec((tb, LANES), lambda h, b: (b, h)),
                  pl.BlockSpec((tb, LANES), lambda h, b: (b, hp + h)),
                  pl.BlockSpec((tb, LANES), lambda h, b: (b, 2 * hp + h)),
                  pl.BlockSpec((tb, LANES), lambda h, b: (b, 3 * hp + h)),
                  pl.BlockSpec((tbc, LANES), lambda h, b: (b, kc_off + h)),
                  pl.BlockSpec((tbc, LANES), lambda h, b: (b, kc_off + hp + h)),
                  pl.BlockSpec((tbc, LANES), lambda h, b: (b, kc_off + 2 * hp + h)),
                  pl.BlockSpec((1,) + rpb_rows.shape[1:], lambda h, b: (h, 0, 0, 0))],
        out_specs=pl.BlockSpec((tb, LANES), lambda h, b: (b, h)),
        out_shape=jax.ShapeDtypeStruct((t, D_NA), BF16),
        scratch_shapes=[pltpu.VMEM((2, 2, Q_ROWS * GRID_W, n_keys), F32),
                        pltpu.VMEM((2, 2, Q_ROWS * GRID_W, n_keys), BF16),
                        pltpu.VMEM((2, n_pat, Q_ROWS * GRID_W, BAND_ROWS * GRID_W), F32)],
        compiler_params=_params("parallel", "arbitrary"),
        name="na_attention",
    )(qkv, qkv, qkv, qkv, kv_ctx, kv_ctx, kv_ctx, rpb_rows)


def _ctx_attn_kernel(q_ref, k_ref, va_ref, vb_ref, o_ref):
    first = _head_mask()
    for h in range(HEAD_PAIRS):
        lanes = slice(h * LANES, (h + 1) * LANES)
        q = q_ref[:, lanes]
        k = k_ref[:, lanes]
        outs = []
        for a, v_ref in enumerate((va_ref, vb_ref)):
            qa = jnp.where(first if a == 0 else ~first, q, jnp.zeros_like(q))
            s = lax.dot_general(qa, k, (((1,), (1,)), ((), ())), preferred_element_type=F32)
            p = jnp.exp(s - jnp.max(s, axis=-1, keepdims=True))
            den = jnp.sum(p, axis=-1, keepdims=True)
            outs.append(_bdot(p.astype(BF16), v_ref[:, lanes]) * (1.0 / den))
        o_ref[:, lanes] = jnp.where(first, outs[0], outs[1]).astype(BF16)


def _ctx_attn(qkv, *, n_ctx):
    t = qkv.shape[0]
    return pl.pallas_call(
        _ctx_attn_kernel,
        grid=(t // n_ctx,),
        in_specs=[pl.BlockSpec((n_ctx, D_NA), lambda b: (b, 0)),
                  pl.BlockSpec((n_ctx, D_NA), lambda b: (b, 1)),
                  pl.BlockSpec((n_ctx, D_NA), lambda b: (b, 2)),
                  pl.BlockSpec((n_ctx, D_NA), lambda b: (b, 3))],
        out_specs=pl.BlockSpec((n_ctx, D_NA), lambda b: (b, 0)),
        out_shape=jax.ShapeDtypeStruct((t, D_NA), BF16),
        compiler_params=_params("parallel"),
        name="ctx_attention",
    )(qkv, qkv, qkv, qkv)


def _outmlp_kernel(h_ref, yf_ref, cv_ref, at_ref, mod_ref, g_ref, gf_ref, wo_ref, w1_ref, w2_ref, o_ref, *, final_norm):
    mix = (_bdot(yf_ref[...], wo_ref[:D_FOURIER, :])
           + _bdot(cv_ref[...], wo_ref[D_FOURIER:D_FOURIER + D_CONV, :])
           + _bdot(at_ref[...], wo_ref[D_FOURIER + D_CONV:, :]))
    h1 = h_ref[...] + mod_ref[0, 2:3, :] * mix
    hn = _rms_mod(h1, g_ref[...], mod_ref[0, 3:4, :], mod_ref[0, 4:5, :]).astype(BF16)
    a = jnp.maximum(_bdot(hn, w1_ref[...]), 0.0)
    out = h1 + mod_ref[0, 5:6, :] * _bdot((a * a).astype(BF16), w2_ref[...])
    if final_norm:
        out = out * lax.rsqrt(jnp.mean(out * out, axis=-1, keepdims=True) + RMS_EPS) * gf_ref[...]
    o_ref[...] = out


def _outmlp(h, yf, cv, at, mods, g2, gf, wo, w1, w2, *, seq, tm, ctx, final_norm):
    t, d = h.shape
    nb = t // seq
    per = seq // tm
    row = (lambda i: nb) if ctx else (lambda i: i // per)
    d_ff = w1.shape[1]
    vec = pl.BlockSpec((1, d), lambda i: (0, 0))
    once = dict(pipeline_mode=pl.Buffered(1))
    return pl.pallas_call(
        functools.partial(_outmlp_kernel, final_norm=final_norm),
        grid=(t // tm,),
        in_specs=[pl.BlockSpec((tm, d), lambda i: (i, 0)),
                  pl.BlockSpec((tm, D_FOURIER), lambda i: (i % per, i // per)),
                  pl.BlockSpec((tm, D_CONV), lambda i: (i, 0)),
                  pl.BlockSpec((tm, D_NA), lambda i: (i, 0)),
                  pl.BlockSpec((1, N_MOD, d), lambda i: (row(i), 0, 0)),
                  vec, vec,
                  pl.BlockSpec((D_MIX, d), lambda i: (0, 0), **once),
                  pl.BlockSpec((d, d_ff), lambda i: (0, 0), **once),
                  pl.BlockSpec((d_ff, d), lambda i: (0, 0), **once)],
        out_specs=pl.BlockSpec((tm, d), lambda i: (i, 0)),
        out_shape=jax.ShapeDtypeStruct((t, d), F32),
        compiler_params=_params("parallel"),
        name="outmlp_ctx" if ctx else "outmlp_lat",
    )(h, yf, cv, at, mods, g2, gf, wo, w1, w2)


def kernel(x, c, ctx, c_ctx, ada_w, ada_b, norm1_g, norm2_g, w_in, w_fourier, conv_dw_w, conv_dw_b, conv_norm_g,
           conv_norm_b, conv_pw_w, conv_pw_b, na_rpb, w_out, mlp_w1, mlp_w2, final_norm_g):
    nb, seq, d = x.shape
    n_ctx = ctx.shape[1]
    depth = ada_w.shape[0]
    assert nb < MOD_ROWS and seq % GRID_W == 0 and d == D_MODEL

    cc = jnp.concatenate([c, c_ctx[None], jnp.zeros((MOD_ROWS - nb - 1, d), F32)], axis=0)
    mods = _adaln(cc, ada_w, ada_b).reshape(depth, MOD_ROWS, N_MOD, d)
    ab_lat = _fold_fourier(w_fourier, seq)
    ab_ctx = _fold_fourier(w_fourier, n_ctx)
    dft_lat = _dft_mats(seq)
    dft_ctx = _dft_mats(n_ctx)

    w_in_b = w_in.astype(BF16)
    w_out_b = w_out.astype(BF16)
    w1_b = mlp_w1.astype(BF16)
    w2_b = mlp_w2.astype(BF16)
    pww_b = conv_pw_w.astype(BF16)
    gf = final_norm_g.reshape(1, d)

    h_lat = x.reshape(nb * seq, d)
    h_ctx = ctx.reshape(nb * n_ctx, d)
    for i in range(depth):
        last = i == depth - 1
        g1 = norm1_g[i].reshape(1, d)
        g2 = norm2_g[i].reshape(1, d)
        conv_p = (conv_dw_w[i], conv_dw_b[i].reshape(1, -1), conv_norm_g[i].reshape(1, -1),
                  conv_norm_b[i].reshape(1, -1), pww_b[i], conv_pw_b[i].reshape(1, -1))
        bias = _na_bias(na_rpb[i])

        if last:
            kv_ctx = _kvproj(h_ctx, g1, mods[i], w_in_b[i], nb=nb, tm=n_ctx)
            kc_off = 0
        else:
            zf_c, uc_c, qkv_c = _inproj(h_ctx, g1, mods[i], w_in_b[i], ab_ctx[i], seq=n_ctx, tm=n_ctx, ctx=True)
            kv_ctx = qkv_c
            kc_off = HEAD_PAIRS

        zf, uc, qkv = _inproj(h_lat, g1, mods[i], w_in_b[i], ab_lat[i], seq=seq, tm=1024, ctx=False)
        yf = _fourier(dft_lat, zf)
        cv = _conv(uc, *conv_p, seq=seq)
        at = _na(qkv, kv_ctx, bias, seq=seq, n_ctx=n_ctx, kc_off=kc_off)
        h_lat = _outmlp(h_lat, yf, cv, at, mods[i], g2, gf, w_out_b[i], w1_b[i], w2_b[i],
                        seq=seq, tm=512, ctx=False, final_norm=last)

        if not last:
            yf_c = _fourier(dft_ctx, zf_c)
            cv_c = _conv(uc_c, *conv_p, seq=n_ctx)
            at_c = _ctx_attn(qkv_c, n_ctx=n_ctx)
            h_ctx = _outmlp(h_ctx, yf_c, cv_c, at_c, mods[i], g2, gf, w_out_b[i], w1_b[i], w2_b[i],
                            seq=n_ctx, tm=n_ctx, ctx=True, final_norm=False)
    return h_lat.reshape(nb, seq, d)
```

```python
import functools

import numpy as np
import jax
import jax.numpy as jnp
from jax import lax
from jax.experimental import pallas as pl
from jax.experimental.pallas import tpu as pltpu

D_MODEL = 1024
GRID_W = 64
D_FOURIER = 256
FOURIER_GROUPS = 4
D_CONV = 256
CONV_WIDTH = 31
N_NA_HEADS = 8
NA_HEAD_DIM = 64
D_NA = N_NA_HEADS * NA_HEAD_DIM
WIN_ROWS = 8
WIN_COLS = 16
QKV_START = D_FOURIER + 2 * D_CONV
KV_START = QKV_START + D_NA
D_MIX = D_FOURIER + D_CONV + D_NA
N_MOD = 6
RMS_EPS = 1e-6
LN_EPS = 1e-5

LANES = 128
SUBLANES = 8
HEAD_PAIRS = N_NA_HEADS * NA_HEAD_DIM // LANES
MOD_ROWS = 16
VMEM_LIMIT = 56 * 1024 * 1024

TM_INPROJ = 1024
TM_MLP = 512
TN_ADALN = 2048

F32 = jnp.float32
BF16 = jnp.bfloat16


def _params(*sem):
    return pltpu.CompilerParams(dimension_semantics=sem, vmem_limit_bytes=VMEM_LIMIT)


def _sigmoid(x):
    return 1.0 / (1.0 + jnp.exp(-x))


def _rms_mod(x, g, shift, scale):
    return x * lax.rsqrt(jnp.mean(x * x, axis=-1, keepdims=True) + RMS_EPS) * (g * (1.0 + scale)) + shift


def _bdot(a, b):
    return jnp.dot(a, b, preferred_element_type=F32)


def _adaln_kernel(c_ref, w_ref, b_ref, o_ref):
    cc = c_ref[...]
    a = cc * _sigmoid(cc)
    w = w_ref[0]
    a_hi = a.astype(BF16)
    a_lo = (a - a_hi.astype(F32)).astype(BF16)
    w_hi = w.astype(BF16)
    w_lo = (w - w_hi.astype(F32)).astype(BF16)
    by_hi = _bdot(jnp.concatenate([a_hi, a_lo], axis=0), w_hi)
    o_ref[0] = by_hi[:MOD_ROWS] + by_hi[MOD_ROWS:] + _bdot(a_hi, w_lo) + b_ref[0]


def _adaln(cc, ada_w, ada_b):
    depth, d, n = ada_w.shape
    tn = TN_ADALN
    return pl.pallas_call(
        _adaln_kernel,
        grid=(depth, n // tn),
        in_specs=[pl.BlockSpec((MOD_ROWS, d), lambda l, j: (0, 0)),
                  pl.BlockSpec((1, d, tn), lambda l, j: (l, 0, j)),
                  pl.BlockSpec((1, 1, tn), lambda l, j: (l, 0, j))],
        out_specs=pl.BlockSpec((1, MOD_ROWS, tn), lambda l, j: (l, 0, j)),
        out_shape=jax.ShapeDtypeStruct((depth, MOD_ROWS, n), F32),
        compiler_params=_params("parallel", "parallel"),
        name="adaln",
    )(cc, ada_w, ada_b.reshape(depth, 1, n))


def _fold_kernel(cc_ref, sc_ref, w_ref, o_ref):
    w = w_ref[0]
    hi = lax.Precision.HIGHEST
    o_ref[0, :, :D_FOURIER] = jnp.dot(cc_ref[...], w, preferred_element_type=F32, precision=hi).astype(BF16)
    o_ref[0, :, D_FOURIER:] = jnp.dot(sc_ref[...], w, preferred_element_type=F32, precision=hi).astype(BF16)


def _fold_fourier(w_fourier, seq):
    depth = w_fourier.shape[0]
    gs = D_FOURIER // FOURIER_GROUPS
    idx = np.arange(D_FOURIER)
    same = (idx[:, None] // gs) == (idx[None, :] // gs)
    ang = 2.0 * np.pi * (((idx[:, None] % gs) * (idx[None, :] % gs)) % gs) / gs
    scale = 1.0 / np.sqrt(seq * gs)
    cc = jnp.asarray(np.where(same, np.cos(ang), 0.0) * scale, F32)
    sc = jnp.asarray(np.where(same, np.sin(ang), 0.0) * scale, F32)
    full = pl.BlockSpec((D_FOURIER, D_FOURIER), lambda l: (0, 0))
    return pl.pallas_call(
        _fold_kernel,
        grid=(depth,),
        in_specs=[full, full, pl.BlockSpec((1, D_FOURIER, D_FOURIER), lambda l: (l, 0, 0))],
        out_specs=pl.BlockSpec((1, D_FOURIER, 2 * D_FOURIER), lambda l: (l, 0, 0)),
        out_shape=jax.ShapeDtypeStruct((depth, D_FOURIER, 2 * D_FOURIER), BF16),
        compiler_params=_params("parallel"),
        name="fold_fourier",
    )(cc, sc, w_fourier)


FLIP_BLOCK = 128


def _dft_mats(seq):
    half = seq // 2
    k = np.arange(half)
    ang = 2.0 * np.pi * ((k[:, None] * k[None, :]) % seq) / seq
    r = np.arange(FLIP_BLOCK)
    perm = (np.arange(2 * FLIP_BLOCK)[None, :] == FLIP_BLOCK - r[:, None]).astype(np.float32)
    alt_col = np.where(k % 2 == 0, 1.0, -1.0).astype(np.float32)[:, None]
    alt_row = np.zeros((SUBLANES, seq), np.float32)
    alt_row[0] = np.where(np.arange(seq) % 2 == 0, 1.0, -1.0)
    return (jnp.asarray(np.cos(ang), F32).astype(BF16), jnp.asarray(np.sin(ang), F32).astype(BF16),
            jnp.asarray(perm, F32).astype(BF16), jnp.asarray(alt_col), jnp.asarray(alt_row, F32).astype(BF16))


def _store_kv(hn, w_ref, out_ref, col):
    kv = _bdot(hn, w_ref[:, KV_START:])
    v = kv[:, D_NA:]
    first = lax.broadcasted_iota(jnp.int32, (1, D_NA), 1) % LANES < NA_HEAD_DIM
    out_ref[:, col:col + D_NA] = kv[:, :D_NA].astype(BF16)
    out_ref[:, col + D_NA:col + 2 * D_NA] = jnp.where(first, v, 1.0).astype(BF16)
    out_ref[:, col + 2 * D_NA:col + 3 * D_NA] = jnp.where(first, 1.0, v).astype(BF16)


def _inproj_kernel(x_ref, g_ref, mod_ref, w_ref, ab_ref, zf_ref, uc_ref, qkv_ref):
    hn = _rms_mod(x_ref[...], g_ref[...], mod_ref[0, 0:1, :], mod_ref[0, 1:2, :]).astype(BF16)
    uf = _bdot(hn, w_ref[:, :D_FOURIER])
    uc_ref[...] = _bdot(hn, w_ref[:, D_FOURIER:QKV_START])
    qkv_ref[:, :D_NA] = (_bdot(hn, w_ref[:, QKV_START:KV_START]) * (NA_HEAD_DIM ** -0.5)).astype(BF16)
    _store_kv(hn, w_ref, qkv_ref, D_NA)
    zf_ref[...] = _bdot(uf.astype(BF16), ab_ref[...]).astype(BF16)


def _inproj(h, g, mods, w_in, ab, *, seq, tm, ctx):
    t, d = h.shape
    nb = t // seq
    per = seq // tm
    row = (lambda i: nb) if ctx else (lambda i: i // per)
    d_in = w_in.shape[1]
    return pl.pallas_call(
        _inproj_kernel,
        grid=(t // tm,),
        in_specs=[pl.BlockSpec((tm, d), lambda i: (i, 0)),
                  pl.BlockSpec((1, d), lambda i: (0, 0)),
                  pl.BlockSpec((1, N_MOD, d), lambda i: (row(i), 0, 0)),
                  pl.BlockSpec((d, d_in), lambda i: (0, 0), pipeline_mode=pl.Buffered(1)),
                  pl.BlockSpec((D_FOURIER, 2 * D_FOURIER), lambda i: (0, 0))],
        out_specs=[pl.BlockSpec((tm, 2 * D_FOURIER), lambda i: (i % per, i // per)),
                   pl.BlockSpec((tm, 2 * D_CONV), lambda i: (i, 0)),
                   pl.BlockSpec((tm, 4 * D_NA), lambda i: (i, 0))],
        out_shape=[jax.ShapeDtypeStruct((seq, nb * 2 * D_FOURIER), BF16),
                   jax.ShapeDtypeStruct((t, 2 * D_CONV), F32),
                   jax.ShapeDtypeStruct((t, 4 * D_NA), BF16)],
        compiler_params=_params("parallel"),
        name="inproj_ctx" if ctx else "inproj_lat",
    )(h, g, mods, w_in, ab)


def _kvproj_kernel(x_ref, g_ref, mod_ref, w_ref, kv_ref):
    hn = _rms_mod(x_ref[...], g_ref[...], mod_ref[0, 0:1, :], mod_ref[0, 1:2, :]).astype(BF16)
    _store_kv(hn, w_ref, kv_ref, 0)


def _kvproj(h, g, mods, w_in, *, nb, tm):
    t, d = h.shape
    d_in = w_in.shape[1]
    n = 3 * D_NA
    return pl.pallas_call(
        _kvproj_kernel,
        grid=(t // tm,),
        in_specs=[pl.BlockSpec((tm, d), lambda i: (i, 0)),
                  pl.BlockSpec((1, d), lambda i: (0, 0)),
                  pl.BlockSpec((1, N_MOD, d), lambda i: (nb, 0, 0)),
                  pl.BlockSpec((d, d_in), lambda i: (0, 0))],
        out_specs=pl.BlockSpec((tm, n), lambda i: (i, 0)),
        out_shape=jax.ShapeDtypeStruct((t, n), BF16),
        compiler_params=_params("parallel"),
        name="kvproj_ctx",
    )(h, g, mods, w_in)


def _fourier_kernel(c_ref, s_ref, perm_ref, altc_ref, altr_ref, z_ref, y_ref, ze_ref, yr_ref):
    seq = z_ref.shape[0]
    half = seq // 2
    nblk = half // FLIP_BLOCK
    fb = FLIP_BLOCK

    def reversed_block(ref, i, n):
        if i == 0:
            return _bdot(perm_ref[:, :fb], ref[n - fb:n, :])
        lo = n - (i + 1) * fb
        return _bdot(perm_ref[...], ref[lo:lo + 2 * fb, :])

    for i in range(nblk):
        rows_i = slice(i * fb, (i + 1) * fb)
        zr = reversed_block(z_ref, i, seq)
        ze_ref[rows_i, :D_FOURIER] = (z_ref[rows_i, :D_FOURIER].astype(F32) + zr[:, :D_FOURIER]).astype(BF16)
        ze_ref[rows_i, D_FOURIER:] = (z_ref[rows_i, D_FOURIER:].astype(F32) - zr[:, D_FOURIER:]).astype(BF16)
    nyq = z_ref[half:half + 1, :D_FOURIER].astype(F32)
    yc = _bdot(c_ref[...], ze_ref[:, :D_FOURIER]) + altc_ref[...] * nyq
    ys = _bdot(s_ref[...], ze_ref[:, D_FOURIER:])
    y_ref[:half, :] = (yc - ys).astype(BF16)
    yr_ref[...] = (yc + ys).astype(BF16)
    y_mid = _bdot(altr_ref[...], z_ref[:, :D_FOURIER])[0:1, :]
    first_row = lax.broadcasted_iota(jnp.int32, (fb, 1), 0) == 0
    for i in range(nblk):
        blk = reversed_block(yr_ref, i, half)
        if i == 0:
            blk = jnp.where(first_row, y_mid, blk)
        y_ref[half + i * fb:half + (i + 1) * fb, :] = blk.astype(BF16)


def _fourier(mats, zf):
    cmat, smat, perm, alt_col, alt_row = mats
    seq = alt_row.shape[1]
    half = seq // 2
    nb = zf.shape[1] // (2 * D_FOURIER)
    const = lambda a: pl.BlockSpec(a.shape, lambda b: (0, 0))
    return pl.pallas_call(
        _fourier_kernel,
        grid=(nb,),
        in_specs=[const(cmat), const(smat), const(perm), const(alt_col), const(alt_row),
                  pl.BlockSpec((seq, 2 * D_FOURIER), lambda b: (0, b))],
        out_specs=pl.BlockSpec((seq, D_FOURIER), lambda b: (0, b)),
        out_shape=jax.ShapeDtypeStruct((seq, nb * D_FOURIER), BF16),
        scratch_shapes=[pltpu.VMEM((half, 2 * D_FOURIER), BF16), pltpu.VMEM((half, D_FOURIER), BF16)],
        compiler_params=_params("parallel"),
        name="fourier_pos",
    )(cmat, smat, perm, alt_col, alt_row, zf)


CONV_PAD = 16
CONV_CHUNK = 512


def _conv_kernel(u_ref, dww_ref, dwb_ref, lng_ref, lnb_ref, pww_ref, pwb_ref, o_ref, vs_ref):
    seq = u_ref.shape[0]
    n_pad = seq + 2 * CONV_PAD
    zeros = jnp.zeros((CONV_PAD, D_CONV), F32)
    vs_ref[0, 0:CONV_PAD, :] = zeros
    vs_ref[0, seq + CONV_PAD:n_pad, :] = zeros
    vs_ref[0, CONV_PAD:CONV_PAD + seq, :] = u_ref[:, :D_CONV] * _sigmoid(u_ref[:, D_CONV:])
    for s in range(1, SUBLANES):
        vs_ref[s, 0:n_pad - SUBLANES, :] = vs_ref[0, s:s + n_pad - SUBLANES, :]
    first = CONV_PAD - CONV_WIDTH // 2
    assert (first + CONV_WIDTH - 1) // SUBLANES * SUBLANES + seq <= n_pad - SUBLANES

    rows = min(CONV_CHUNK, seq)

    def chunk(ci, carry):
        base = pl.multiple_of(ci * rows, rows)
        acc = jnp.zeros((rows, D_CONV), F32) + dwb_ref[...]
        for t in range(CONV_WIDTH):
            s, a = (first + t) % SUBLANES, (first + t) // SUBLANES
            acc = acc + vs_ref[s, pl.ds(base + a * SUBLANES, rows), :] * dww_ref[t:t + 1, :]
        mu = jnp.mean(acc, axis=-1, keepdims=True)
        cen = acc - mu
        var = jnp.mean(cen * cen, axis=-1, keepdims=True)
        y = cen * lax.rsqrt(var + LN_EPS) * lng_ref[...] + lnb_ref[...]
        y = y * _sigmoid(y)
        o_ref[pl.ds(base, rows), :] = (_bdot(y.astype(BF16), pww_ref[...]) + pwb_ref[...]).astype(BF16)
        return carry

    lax.fori_loop(0, seq // rows, chunk, 0)


def _conv(uc, dww, dwb, lng, lnb, pww, pwb, *, seq):
    t = uc.shape[0]
    vec = pl.BlockSpec((1, D_CONV), lambda b: (0, 0))
    return pl.pallas_call(
        _conv_kernel,
        grid=(t // seq,),
        in_specs=[pl.BlockSpec((seq, 2 * D_CONV), lambda b: (b, 0)),
                  pl.BlockSpec((CONV_WIDTH, D_CONV), lambda b: (0, 0)),
                  vec, vec, vec,
                  pl.BlockSpec((D_CONV, D_CONV), lambda b: (0, 0)),
                  vec],
        out_specs=pl.BlockSpec((seq, D_CONV), lambda b: (b, 0)),
        out_shape=jax.ShapeDtypeStruct((t, D_CONV), BF16),
        scratch_shapes=[pltpu.VMEM((SUBLANES, seq + 2 * CONV_PAD, D_CONV), F32)],
        compiler_params=_params("parallel"),
        name="conv_module",
    )(uc, dww, dwb, lng, lnb, pww, pwb)


def _head_mask():
    return lax.broadcasted_iota(jnp.int32, (1, LANES), 1) < NA_HEAD_DIM


Q_ROWS = 2
BAND_ROWS = 10
assert BAND_ROWS >= WIN_ROWS + Q_ROWS - 1 and (BAND_ROWS * GRID_W) % LANES == 0


def _na_patterns(rows):
    starts, sigs = [], []
    for g in range(rows // Q_ROWS):
        start = int(np.clip(Q_ROWS * g - WIN_ROWS // 2, 0, rows - BAND_ROWS))
        r = Q_ROWS * g + np.arange(Q_ROWS)
        rs = np.clip(r - WIN_ROWS // 2, 0, rows - WIN_ROWS)
        starts.append(start)
        sigs.append((start - Q_ROWS * g,) + tuple(rs - r))
    run_starts = [g for g in range(len(sigs)) if g == 0 or sigs[g] != sigs[g - 1]]
    assert len(set(sigs)) == len(run_starts)
    return starts, run_starts


def _na_kernel(q_ref, k_ref, va_ref, vb_ref, kc_ref, vca_ref, vcb_ref, rpb_ref, o_ref, s_ref, p_ref, bias_ref, *,
               seq, n_ctx):
    rows = seq // GRID_W
    per_batch = rows // Q_ROWS
    n_groups = q_ref.shape[0] // seq * per_batch
    n_q = Q_ROWS * GRID_W
    n_loc = BAND_ROWS * GRID_W
    _, run_starts = _na_patterns(rows)
    first = _head_mask()
    dn = (((1,), (1,)), ((), ()))
    assert n_groups >= 4 and seq == per_batch * n_q

    def split(t):
        t = jnp.asarray(t, jnp.int32)
        return t // per_batch, t % per_batch

    def band(t):
        b, g = split(t)
        start = jnp.clip(Q_ROWS * g - WIN_ROWS // 2, 0, rows - BAND_ROWS)
        return pl.multiple_of(b * seq + start * GRID_W, LANES)

    def q_start(t):
        return pl.multiple_of(jnp.asarray(t, jnp.int32) * n_q, n_q)

    def ctx_start(t):
        return pl.multiple_of(split(t)[0] * n_ctx, n_ctx)

    def scores(t, slot):
        g = split(t)[1]
        pat = sum((g >= s).astype(jnp.int32) for s in run_starts[1:])
        q = q_ref[pl.ds(q_start(t), n_q), :]
        kb = k_ref[pl.ds(band(t), n_loc), :]
        kc = kc_ref[pl.ds(ctx_start(t), n_ctx), :]
        for a in range(2):
            qa = jnp.where(first if a == 0 else ~first, q, jnp.zeros_like(q))
            s_ref[slot, a, :, :n_loc] = lax.dot_general(qa, kb, dn, preferred_element_type=F32) + bias_ref[a, pat]
            s_ref[slot, a, :, n_loc:] = lax.dot_general(qa, kc, dn, preferred_element_type=F32)

    def softmax(slot):
        for a in range(2):
            s = s_ref[slot, a]
            p_ref[slot, a] = jnp.exp((s - jnp.max(s, axis=-1, keepdims=True)).astype(BF16))

    def values(t, slot):
        outs = []
        for a, (v_ref, vc_ref) in enumerate(((va_ref, vca_ref), (vb_ref, vcb_ref))):
            o = (_bdot(p_ref[slot, a, :, :n_loc], v_ref[pl.ds(band(t), n_loc), :])
                 + _bdot(p_ref[slot, a, :, n_loc:], vc_ref[pl.ds(ctx_start(t), n_ctx), :]))
            outs.append(o * (1.0 / pltpu.roll(o, NA_HEAD_DIM, axis=1)))
        o_ref[pl.ds(q_start(t), n_q), :] = jnp.where(first, outs[0], outs[1]).astype(BF16)

    def tick(t, parity):
        values(t - 1, 1 - parity)
        scores(t + 1, 1 - parity)
        softmax(parity)

    def quad(j, carry):
        t = 4 * j + 1
        tick(t, 1)
        tick(t + 1, 0)
        tick(t + 2, 1)
        tick(t + 3, 0)
        return carry

    @pl.when(pl.program_id(1) == 0)
    def _():
        _na_assemble_bias(rpb_ref, bias_ref, rows)

    scores(0, 0)
    scores(1, 1)
    softmax(0)
    n_quads = (n_groups - 2) // 4
    lax.fori_loop(0, n_quads, quad, 0)
    for t in range(4 * n_quads + 1, n_groups - 1):
        tick(t, t % 2)
    softmax((n_groups - 1) % 2)
    values(n_groups - 2, n_groups % 2)
    values(n_groups - 1, (n_groups - 1) % 2)


def _na_bias(rpb):
    n_heads, n_dr, n_dc = rpb.shape
    r = rpb.astype(F32)
    w = jnp.concatenate([r[..., WIN_COLS - 1:], jnp.zeros((n_heads, n_dr, LANES - n_dc), F32), r[..., :WIN_COLS - 1]],
                        axis=-1)
    w = jnp.pad(w, ((0, 0), (0, 1), (0, 0)))
    return w.reshape(HEAD_PAIRS, 2, n_dr + 1, LANES)


def _na_assemble_bias(w_ref, bias_ref, rows):
    starts, run_starts = _na_patterns(rows)
    cq = lax.broadcasted_iota(jnp.int32, (GRID_W, LANES), 0)
    lane = lax.broadcasted_iota(jnp.int32, (GRID_W, LANES), 1)
    low = lane < GRID_W
    ck = jnp.where(low, lane, lane - GRID_W)
    col_start = jnp.clip(cq - WIN_COLS // 2, 0, GRID_W - WIN_COLS)
    in_window = (ck >= col_start) & (ck < col_start + WIN_COLS)
    masked = jnp.full((GRID_W, LANES), -jnp.inf, F32)
    rolled = {}

    def half(a, d, upper):
        if (a, d, upper) not in rolled:
            row = jnp.broadcast_to(w_ref[0, a, d:d + 1, :], (GRID_W, LANES))
            rolled[a, d, upper] = pltpu.roll(row, GRID_W if upper else 0, axis=1, stride=1, stride_axis=0)
        return rolled[a, d, upper]

    for a in range(2):
        for p, g in enumerate(run_starts):
            for qi in range(Q_ROWS):
                r = Q_ROWS * g + qi
                rs = int(np.clip(r - WIN_ROWS // 2, 0, rows - WIN_ROWS))
                below = rs - starts[g]
                dr0 = rs - r + WIN_ROWS - 1
                for t in range(BAND_ROWS // 2):
                    d0 = dr0 + 2 * t - below
                    v0 = below <= 2 * t < below + WIN_ROWS
                    v1 = below <= 2 * t + 1 < below + WIN_ROWS
                    if v0 and v1:
                        tile = jnp.where(low, half(a, d0, False), half(a, d0 + 1, True))
                    elif v0:
                        tile = jnp.where(low, half(a, d0, False), masked)
                    elif v1:
                        tile = jnp.where(low, masked, half(a, d0 + 1, True))
                    else:
                        tile = masked
                    if v0 or v1:
                        tile = jnp.where(in_window, tile, masked)
                    bias_ref[a, p, qi * GRID_W:(qi + 1) * GRID_W, t * LANES:(t + 1) * LANES] = tile


NA_BATCHES = 4


def _na(qkv, kv_ctx, rpb_rows, *, seq, n_ctx, kc_off):
    t = qkv.shape[0]
    hp = HEAD_PAIRS
    tb, tbc = NA_BATCHES * seq, NA_BATCHES * n_ctx
    n_pat = len(_na_patterns(seq // GRID_W)[1])
    n_keys = BAND_ROWS * GRID_W + n_ctx
    assert t % tb == 0
    return pl.pallas_call(
        functools.partial(_na_kernel, seq=seq, n_ctx=n_ctx),
        grid=(hp, t // tb),
        in_specs=[pl.BlockSpec((tb, LANES), lambda h, b: (b, h)),
                  pl.BlockSpec((tb, LANES), lambda h, b: (b, hp + h)),
                  pl.BlockSpec((tb, LANES), lambda h, b: (b, 2 * hp + h)),
                  pl.BlockSpec((tb, LANES), lambda h, b: (b, 3 * hp + h)),
                  pl.BlockSpec((tbc, LANES), lambda h, b: (b, kc_off + h)),
                  pl.BlockSpec((tbc, LANES), lambda h, b: (b, kc_off + hp + h)),
                  pl.BlockSpec((tbc, LANES), lambda h, b: (b, kc_off + 2 * hp + h)),
                  pl.BlockSpec((1,) + rpb_rows.shape[1:], lambda h, b: (h, 0, 0, 0))],
        out_specs=pl.BlockSpec((tb, LANES), lambda h, b: (b, h)),
        out_shape=jax.ShapeDtypeStruct((t, D_NA), BF16),
        scratch_shapes=[pltpu.VMEM((2, 2, Q_ROWS * GRID_W, n_keys), F32),
                        pltpu.VMEM((2, 2, Q_ROWS * GRID_W, n_keys), BF16),
                        pltpu.VMEM((2, n_pat, Q_ROWS * GRID_W, BAND_ROWS * GRID_W), F32)],
        compiler_params=_params("parallel", "arbitrary"),
        name="na_attention",
    )(qkv, qkv, qkv, qkv, kv_ctx, kv_ctx, kv_ctx, rpb_rows)


def _ctx_attn_kernel(q_ref, k_ref, va_ref, vb_ref, o_ref):
    first = _head_mask()
    for h in range(HEAD_PAIRS):
        lanes = slice(h * LANES, (h + 1) * LANES)
        q = q_ref[:, lanes]
        k = k_ref[:, lanes]
        outs = []
        for a, v_ref in enumerate((va_ref, vb_ref)):
            qa = jnp.where(first if a == 0 else ~first, q, jnp.zeros_like(q))
            s = lax.dot_general(qa, k, (((1,), (1,)), ((), ())), preferred_element_type=F32)
            p = jnp.exp(s - jnp.max(s, axis=-1, keepdims=True))
            den = jnp.sum(p, axis=-1, keepdims=True)
            outs.append(_bdot(p.astype(BF16), v_ref[:, lanes]) * (1.0 / den))
        o_ref[:, lanes] = jnp.where(first, outs[0], outs[1]).astype(BF16)


def _ctx_attn(qkv, *, n_ctx):
    t = qkv.shape[0]
    return pl.pallas_call(
        _ctx_attn_kernel,
        grid=(t // n_ctx,),
        in_specs=[pl.BlockSpec((n_ctx, D_NA), lambda b: (b, 0)),
                  pl.BlockSpec((n_ctx, D_NA), lambda b: (b, 1)),
                  pl.BlockSpec((n_ctx, D_NA), lambda b: (b, 2)),
                  pl.BlockSpec((n_ctx, D_NA), lambda b: (b, 3))],
        out_specs=pl.BlockSpec((n_ctx, D_NA), lambda b: (b, 0)),
        out_shape=jax.ShapeDtypeStruct((t, D_NA), BF16),
        compiler_params=_params("parallel"),
        name="ctx_attention",
    )(qkv, qkv, qkv, qkv)


def _outmlp_kernel(h_ref, yf_ref, cv_ref, at_ref, mod_ref, g_ref, gf_ref, wo_ref, w1_ref, w2_ref, o_ref, *, final_norm):
    mix = (_bdot(yf_ref[...], wo_ref[:D_FOURIER, :])
           + _bdot(cv_ref[...], wo_ref[D_FOURIER:D_FOURIER + D_CONV, :])
           + _bdot(at_ref[...], wo_ref[D_FOURIER + D_CONV:, :]))
    h1 = h_ref[...] + mod_ref[0, 2:3, :] * mix
    hn = _rms_mod(h1, g_ref[...], mod_ref[0, 3:4, :], mod_ref[0, 4:5, :]).astype(BF16)
    a = jnp.maximum(_bdot(hn, w1_ref[...]), 0.0)
    out = h1 + mod_ref[0, 5:6, :] * _bdot((a * a).astype(BF16), w2_ref[...])
    if final_norm:
        out = out * lax.rsqrt(jnp.mean(out * out, axis=-1, keepdims=True) + RMS_EPS) * gf_ref[...]
    o_ref[...] = out


def _outmlp(h, yf, cv, at, mods, g2, gf, wo, w1, w2, *, seq, tm, ctx, final_norm):
    t, d = h.shape
    nb = t // seq
    per = seq // tm
    row = (lambda i: nb) if ctx else (lambda i: i // per)
    d_ff = w1.shape[1]
    vec = pl.BlockSpec((1, d), lambda i: (0, 0))
    once = dict(pipeline_mode=pl.Buffered(1))
    return pl.pallas_call(
        functools.partial(_outmlp_kernel, final_norm=final_norm),
        grid=(t // tm,),
        in_specs=[pl.BlockSpec((tm, d), lambda i: (i, 0)),
                  pl.BlockSpec((tm, D_FOURIER), lambda i: (i % per, i // per)),
                  pl.BlockSpec((tm, D_CONV), lambda i: (i, 0)),
                  pl.BlockSpec((tm, D_NA), lambda i: (i, 0)),
                  pl.BlockSpec((1, N_MOD, d), lambda i: (row(i), 0, 0)),
                  vec, vec,
                  pl.BlockSpec((D_MIX, d), lambda i: (0, 0), **once),
                  pl.BlockSpec((d, d_ff), lambda i: (0, 0), **once),
                  pl.BlockSpec((d_ff, d), lambda i: (0, 0), **once)],
        out_specs=pl.BlockSpec((tm, d), lambda i: (i, 0)),
        out_shape=jax.ShapeDtypeStruct((t, d), F32),
        compiler_params=_params("parallel"),
        name="outmlp_ctx" if ctx else "outmlp_lat",
    )(h, yf, cv, at, mods, g2, gf, wo, w1, w2)


def kernel(x, c, ctx, c_ctx, ada_w, ada_b, norm1_g, norm2_g, w_in, w_fourier, conv_dw_w, conv_dw_b, conv_norm_g,
           conv_norm_b, conv_pw_w, conv_pw_b, na_rpb, w_out, mlp_w1, mlp_w2, final_norm_g):
    nb, seq, d = x.shape
    n_ctx = ctx.shape[1]
    depth = ada_w.shape[0]
    assert nb < MOD_ROWS and seq % GRID_W == 0 and d == D_MODEL

    cc = jnp.concatenate([c, c_ctx[None], jnp.zeros((MOD_ROWS - nb - 1, d), F32)], axis=0)
    mods = _adaln(cc, ada_w, ada_b).reshape(depth, MOD_ROWS, N_MOD, d)
    ab_lat = _fold_fourier(w_fourier, seq)
    ab_ctx = _fold_fourier(w_fourier, n_ctx)
    dft_lat = _dft_mats(seq)
    dft_ctx = _dft_mats(n_ctx)

    w_in_b = w_in.astype(BF16)
    w_out_b = w_out.astype(BF16)
    w1_b = mlp_w1.astype(BF16)
    w2_b = mlp_w2.astype(BF16)
    pww_b = conv_pw_w.astype(BF16)
    gf = final_norm_g.reshape(1, d)

    h_lat = x.reshape(nb * seq, d)
    h_ctx = ctx.reshape(nb * n_ctx, d)
    for i in range(depth):
        last = i == depth - 1
        g1 = norm1_g[i].reshape(1, d)
        g2 = norm2_g[i].reshape(1, d)
        conv_p = (conv_dw_w[i], conv_dw_b[i].reshape(1, -1), conv_norm_g[i].reshape(1, -1),
                  conv_norm_b[i].reshape(1, -1), pww_b[i], conv_pw_b[i].reshape(1, -1))
        bias = _na_bias(na_rpb[i])

        if last:
            kv_ctx = _kvproj(h_ctx, g1, mods[i], w_in_b[i], nb=nb, tm=n_ctx)
            kc_off = 0
        else:
            zf_c, uc_c, qkv_c = _inproj(h_ctx, g1, mods[i], w_in_b[i], ab_ctx[i], seq=n_ctx, tm=n_ctx, ctx=True)
            kv_ctx = qkv_c
            kc_off = HEAD_PAIRS

        zf, uc, qkv = _inproj(h_lat, g1, mods[i], w_in_b[i], ab_lat[i], seq=seq, tm=TM_INPROJ, ctx=False)
        yf = _fourier(dft_lat, zf)
        cv = _conv(uc, *conv_p, seq=seq)
        at = _na(qkv, kv_ctx, bias, seq=seq, n_ctx=n_ctx, kc_off=kc_off)
        h_lat = _outmlp(h_lat, yf, cv, at, mods[i], g2, gf, w_out_b[i], w1_b[i], w2_b[i],
                        seq=seq, tm=TM_MLP, ctx=False, final_norm=last)

        if not last:
            yf_c = _fourier(dft_ctx, zf_c)
            cv_c = _conv(uc_c, *conv_p, seq=n_ctx)
            at_c = _ctx_attn(qkv_c, n_ctx=n_ctx)
            h_ctx = _outmlp(h_ctx, yf_c, cv_c, at_c, mods[i], g2, gf, w_out_b[i], w1_b[i], w2_b[i],
                            seq=n_ctx, tm=n_ctx, ctx=True, final_norm=False)
    return h_lat.reshape(nb, seq, d)
```

```python
import functools

import numpy as np
import jax
import jax.numpy as jnp
from jax import lax
from jax.experimental import pallas as pl
from jax.experimental.pallas import tpu as pltpu

D_MODEL = 1024
GRID_W = 64
D_FOURIER = 256
FOURIER_GROUPS = 4
D_CONV = 256
CONV_WIDTH = 31
N_NA_HEADS = 8
NA_HEAD_DIM = 64
D_NA = N_NA_HEADS * NA_HEAD_DIM
WIN_ROWS = 8
WIN_COLS = 16
QKV_START = D_FOURIER + 2 * D_CONV
KV_START = QKV_START + D_NA
D_MIX = D_FOURIER + D_CONV + D_NA
N_MOD = 6
RMS_EPS = 1e-6
LN_EPS = 1e-5

LANES = 128
SUBLANES = 8
HEAD_PAIRS = N_NA_HEADS * NA_HEAD_DIM // LANES
MOD_ROWS = 16
VMEM_LIMIT = 56 * 1024 * 1024

TM_INPROJ = 1024
TM_MLP = 512
TN_ADALN = 2048

F32 = jnp.float32
BF16 = jnp.bfloat16


def _params(*sem):
    return pltpu.CompilerParams(dimension_semantics=sem, vmem_limit_bytes=VMEM_LIMIT)


def _sigmoid(x):
    return 1.0 / (1.0 + jnp.exp(-x))


def _rms_mod(x, g, shift, scale):
    return x * lax.rsqrt(jnp.mean(x * x, axis=-1, keepdims=True) + RMS_EPS) * (g * (1.0 + scale)) + shift


def _bdot(a, b):
    return jnp.dot(a, b, preferred_element_type=F32)


def _adaln_kernel(c_ref, w_ref, b_ref, o_ref):
    cc = c_ref[...]
    a = cc * _sigmoid(cc)
    w = w_ref[0]
    a_hi = a.astype(BF16)
    a_lo = (a - a_hi.astype(F32)).astype(BF16)
    w_hi = w.astype(BF16)
    w_lo = (w - w_hi.astype(F32)).astype(BF16)
    by_hi = _bdot(jnp.concatenate([a_hi, a_lo], axis=0), w_hi)
    o_ref[0] = by_hi[:MOD_ROWS] + by_hi[MOD_ROWS:] + _bdot(a_hi, w_lo) + b_ref[0]


def _adaln(cc, ada_w, ada_b):
    depth, d, n = ada_w.shape
    tn = TN_ADALN
    return pl.pallas_call(
        _adaln_kernel,
        grid=(depth, n // tn),
        in_specs=[pl.BlockSpec((MOD_ROWS, d), lambda l, j: (0, 0)),
                  pl.BlockSpec((1, d, tn), lambda l, j: (l, 0, j)),
                  pl.BlockSpec((1, 1, tn), lambda l, j: (l, 0, j))],
        out_specs=pl.BlockSpec((1, MOD_ROWS, tn), lambda l, j: (l, 0, j)),
        out_shape=jax.ShapeDtypeStruct((depth, MOD_ROWS, n), F32),
        compiler_params=_params("parallel", "parallel"),
        name="adaln",
    )(cc, ada_w, ada_b.reshape(depth, 1, n))


def _fold_kernel(cc_ref, sc_ref, w_ref, o_ref):
    w = w_ref[0]
    hi = lax.Precision.HIGHEST
    o_ref[0, :, :D_FOURIER] = jnp.dot(cc_ref[...], w, preferred_element_type=F32, precision=hi).astype(BF16)
    o_ref[0, :, D_FOURIER:] = jnp.dot(sc_ref[...], w, preferred_element_type=F32, precision=hi).astype(BF16)


def _fold_fourier(w_fourier, seq):
    depth = w_fourier.shape[0]
    gs = D_FOURIER // FOURIER_GROUPS
    idx = np.arange(D_FOURIER)
    same = (idx[:, None] // gs) == (idx[None, :] // gs)
    ang = 2.0 * np.pi * (((idx[:, None] % gs) * (idx[None, :] % gs)) % gs) / gs
    scale = 1.0 / np.sqrt(seq * gs)
    cc = jnp.asarray(np.where(same, np.cos(ang), 0.0) * scale, F32)
    sc = jnp.asarray(np.where(same, np.sin(ang), 0.0) * scale, F32)
    full = pl.BlockSpec((D_FOURIER, D_FOURIER), lambda l: (0, 0))
    return pl.pallas_call(
        _fold_kernel,
        grid=(depth,),
        in_specs=[full, full, pl.BlockSpec((1, D_FOURIER, D_FOURIER), lambda l: (l, 0, 0))],
        out_specs=pl.BlockSpec((1, D_FOURIER, 2 * D_FOURIER), lambda l: (l, 0, 0)),
        out_shape=jax.ShapeDtypeStruct((depth, D_FOURIER, 2 * D_FOURIER), BF16),
        compiler_params=_params("parallel"),
        name="fold_fourier",
    )(cc, sc, w_fourier)


FLIP_BLOCK = 128


def _dft_mats(seq):
    half = seq // 2
    k = np.arange(half)
    ang = 2.0 * np.pi * ((k[:, None] * k[None, :]) % seq) / seq
    r = np.arange(FLIP_BLOCK)
    perm = (np.arange(2 * FLIP_BLOCK)[None, :] == FLIP_BLOCK - r[:, None]).astype(np.float32)
    alt_col = np.where(k % 2 == 0, 1.0, -1.0).astype(np.float32)[:, None]
    alt_row = np.zeros((SUBLANES, seq), np.float32)
    alt_row[0] = np.where(np.arange(seq) % 2 == 0, 1.0, -1.0)
    return (jnp.asarray(np.cos(ang), F32).astype(BF16), jnp.asarray(np.sin(ang), F32).astype(BF16),
            jnp.asarray(perm, F32).astype(BF16), jnp.asarray(alt_col), jnp.asarray(alt_row, F32).astype(BF16))


NT = (((1,), (1,)), ((), ()))


def _store_kv(hn, w_ref, wkt_ref, kt_ref, out_ref, col):
    kt_ref[...] = lax.dot_general(wkt_ref[...], hn, NT, preferred_element_type=F32).astype(BF16)
    v = _bdot(hn, w_ref[:, KV_START + D_NA:])
    first = lax.broadcasted_iota(jnp.int32, (1, D_NA), 1) % LANES < NA_HEAD_DIM
    out_ref[:, col:col + D_NA] = jnp.where(first, v, 1.0).astype(BF16)
    out_ref[:, col + D_NA:col + 2 * D_NA] = jnp.where(first, 1.0, v).astype(BF16)


def _inproj_kernel(x_ref, g_ref, mod_ref, w_ref, wkt_ref, ab_ref, zf_ref, uc_ref, qv_ref, kt_ref):
    hn = _rms_mod(x_ref[...], g_ref[...], mod_ref[0, 0:1, :], mod_ref[0, 1:2, :]).astype(BF16)
    uf = _bdot(hn, w_ref[:, :D_FOURIER])
    uc_ref[...] = _bdot(hn, w_ref[:, D_FOURIER:QKV_START])
    qv_ref[:, :D_NA] = (_bdot(hn, w_ref[:, QKV_START:KV_START]) * (NA_HEAD_DIM ** -0.5)).astype(BF16)
    _store_kv(hn, w_ref, wkt_ref, kt_ref, qv_ref, D_NA)
    zf_ref[...] = _bdot(uf.astype(BF16), ab_ref[...]).astype(BF16)


def _inproj(h, g, mods, w_in, wkt, ab, *, seq, tm, ctx):
    t, d = h.shape
    nb = t // seq
    per = seq // tm
    row = (lambda i: nb) if ctx else (lambda i: i // per)
    d_in = w_in.shape[1]
    once = dict(pipeline_mode=pl.Buffered(1))
    return pl.pallas_call(
        _inproj_kernel,
        grid=(t // tm,),
        in_specs=[pl.BlockSpec((tm, d), lambda i: (i, 0)),
                  pl.BlockSpec((1, d), lambda i: (0, 0)),
                  pl.BlockSpec((1, N_MOD, d), lambda i: (row(i), 0, 0)),
                  pl.BlockSpec((d, d_in), lambda i: (0, 0), **once),
                  pl.BlockSpec((D_NA, d), lambda i: (0, 0), **once),
                  pl.BlockSpec((D_FOURIER, 2 * D_FOURIER), lambda i: (0, 0))],
        out_specs=[pl.BlockSpec((tm, 2 * D_FOURIER), lambda i: (i % per, i // per)),
                   pl.BlockSpec((tm, 2 * D_CONV), lambda i: (i, 0)),
                   pl.BlockSpec((tm, 3 * D_NA), lambda i: (i, 0)),
                   pl.BlockSpec((D_NA, tm), lambda i: (0, i))],
        out_shape=[jax.ShapeDtypeStruct((seq, nb * 2 * D_FOURIER), BF16),
                   jax.ShapeDtypeStruct((t, 2 * D_CONV), F32),
                   jax.ShapeDtypeStruct((t, 3 * D_NA), BF16),
                   jax.ShapeDtypeStruct((D_NA, t), BF16)],
        compiler_params=_params("parallel"),
        name="inproj_ctx" if ctx else "inproj_lat",
    )(h, g, mods, w_in, wkt, ab)


def _kvproj_kernel(x_ref, g_ref, mod_ref, w_ref, wkt_ref, v_ref, kt_ref):
    hn = _rms_mod(x_ref[...], g_ref[...], mod_ref[0, 0:1, :], mod_ref[0, 1:2, :]).astype(BF16)
    _store_kv(hn, w_ref, wkt_ref, kt_ref, v_ref, 0)


def _kvproj(h, g, mods, w_in, wkt, *, nb, tm):
    t, d = h.shape
    d_in = w_in.shape[1]
    return pl.pallas_call(
        _kvproj_kernel,
        grid=(t // tm,),
        in_specs=[pl.BlockSpec((tm, d), lambda i: (i, 0)),
                  pl.BlockSpec((1, d), lambda i: (0, 0)),
                  pl.BlockSpec((1, N_MOD, d), lambda i: (nb, 0, 0)),
                  pl.BlockSpec((d, d_in), lambda i: (0, 0)),
                  pl.BlockSpec((D_NA, d), lambda i: (0, 0))],
        out_specs=[pl.BlockSpec((tm, 2 * D_NA), lambda i: (i, 0)),
                   pl.BlockSpec((D_NA, tm), lambda i: (0, i))],
        out_shape=[jax.ShapeDtypeStruct((t, 2 * D_NA), BF16),
                   jax.ShapeDtypeStruct((D_NA, t), BF16)],
        compiler_params=_params("parallel"),
        name="kvproj_ctx",
    )(h, g, mods, w_in, wkt)


def _fourier_kernel(c_ref, s_ref, perm_ref, altc_ref, altr_ref, z_ref, y_ref, ze_ref, yr_ref):
    seq = z_ref.shape[0]
    half = seq // 2
    nblk = half // FLIP_BLOCK
    fb = FLIP_BLOCK

    def reversed_block(ref, i, n):
        if i == 0:
            return _bdot(perm_ref[:, :fb], ref[n - fb:n, :])
        lo = n - (i + 1) * fb
        return _bdot(perm_ref[...], ref[lo:lo + 2 * fb, :])

    for i in range(nblk):
        rows_i = slice(i * fb, (i + 1) * fb)
        zr = reversed_block(z_ref, i, seq)
        ze_ref[rows_i, :D_FOURIER] = (z_ref[rows_i, :D_FOURIER].astype(F32) + zr[:, :D_FOURIER]).astype(BF16)
        ze_ref[rows_i, D_FOURIER:] = (z_ref[rows_i, D_FOURIER:].astype(F32) - zr[:, D_FOURIER:]).astype(BF16)
    nyq = z_ref[half:half + 1, :D_FOURIER].astype(F32)
    yc = _bdot(c_ref[...], ze_ref[:, :D_FOURIER]) + altc_ref[...] * nyq
    ys = _bdot(s_ref[...], ze_ref[:, D_FOURIER:])
    y_ref[:half, :] = (yc - ys).astype(BF16)
    yr_ref[...] = (yc + ys).astype(BF16)
    y_mid = _bdot(altr_ref[...], z_ref[:, :D_FOURIER])[0:1, :]
    first_row = lax.broadcasted_iota(jnp.int32, (fb, 1), 0) == 0
    for i in range(nblk):
        blk = reversed_block(yr_ref, i, half)
        if i == 0:
            blk = jnp.where(first_row, y_mid, blk)
        y_ref[half + i * fb:half + (i + 1) * fb, :] = blk.astype(BF16)


def _fourier(mats, zf):
    cmat, smat, perm, alt_col, alt_row = mats
    seq = alt_row.shape[1]
    half = seq // 2
    nb = zf.shape[1] // (2 * D_FOURIER)
    const = lambda a: pl.BlockSpec(a.shape, lambda b: (0, 0))
    return pl.pallas_call(
        _fourier_kernel,
        grid=(nb,),
        in_specs=[const(cmat), const(smat), const(perm), const(alt_col), const(alt_row),
                  pl.BlockSpec((seq, 2 * D_FOURIER), lambda b: (0, b))],
        out_specs=pl.BlockSpec((seq, D_FOURIER), lambda b: (0, b)),
        out_shape=jax.ShapeDtypeStruct((seq, nb * D_FOURIER), BF16),
        scratch_shapes=[pltpu.VMEM((half, 2 * D_FOURIER), BF16), pltpu.VMEM((half, D_FOURIER), BF16)],
        compiler_params=_params("parallel"),
        name="fourier_pos",
    )(cmat, smat, perm, alt_col, alt_row, zf)


CONV_PAD = 16
CONV_CHUNK = 512


def _conv_kernel(u_ref, dww_ref, dwb_ref, lng_ref, lnb_ref, pww_ref, pwb_ref, o_ref, vs_ref):
    seq = u_ref.shape[0]
    n_pad = seq + 2 * CONV_PAD
    zeros = jnp.zeros((CONV_PAD, D_CONV), F32)
    vs_ref[0, 0:CONV_PAD, :] = zeros
    vs_ref[0, seq + CONV_PAD:n_pad, :] = zeros
    vs_ref[0, CONV_PAD:CONV_PAD + seq, :] = u_ref[:, :D_CONV] * _sigmoid(u_ref[:, D_CONV:])
    for s in range(1, SUBLANES):
        vs_ref[s, 0:n_pad - SUBLANES, :] = vs_ref[0, s:s + n_pad - SUBLANES, :]
    first = CONV_PAD - CONV_WIDTH // 2
    assert (first + CONV_WIDTH - 1) // SUBLANES * SUBLANES + seq <= n_pad - SUBLANES

    rows = min(CONV_CHUNK, seq)

    def chunk(ci, carry):
        base = pl.multiple_of(ci * rows, rows)
        acc = jnp.zeros((rows, D_CONV), F32) + dwb_ref[...]
        for t in range(CONV_WIDTH):
            s, a = (first + t) % SUBLANES, (first + t) // SUBLANES
            acc = acc + vs_ref[s, pl.ds(base + a * SUBLANES, rows), :] * dww_ref[t:t + 1, :]
        mu = jnp.mean(acc, axis=-1, keepdims=True)
        cen = acc - mu
        var = jnp.mean(cen * cen, axis=-1, keepdims=True)
        y = cen * lax.rsqrt(var + LN_EPS) * lng_ref[...] + lnb_ref[...]
        y = y * _sigmoid(y)
        o_ref[pl.ds(base, rows), :] = (_bdot(y.astype(BF16), pww_ref[...]) + pwb_ref[...]).astype(BF16)
        return carry

    lax.fori_loop(0, seq // rows, chunk, 0)


def _conv(uc, dww, dwb, lng, lnb, pww, pwb, *, seq):
    t = uc.shape[0]
    vec = pl.BlockSpec((1, D_CONV), lambda b: (0, 0))
    return pl.pallas_call(
        _conv_kernel,
        grid=(t // seq,),
        in_specs=[pl.BlockSpec((seq, 2 * D_CONV), lambda b: (b, 0)),
                  pl.BlockSpec((CONV_WIDTH, D_CONV), lambda b: (0, 0)),
                  vec, vec, vec,
                  pl.BlockSpec((D_CONV, D_CONV), lambda b: (0, 0)),
                  vec],
        out_specs=pl.BlockSpec((seq, D_CONV), lambda b: (b, 0)),
        out_shape=jax.ShapeDtypeStruct((t, D_CONV), BF16),
        scratch_shapes=[pltpu.VMEM((SUBLANES, seq + 2 * CONV_PAD, D_CONV), F32)],
        compiler_params=_params("parallel"),
        name="conv_module",
    )(uc, dww, dwb, lng, lnb, pww, pwb)


def _head_mask():
    return lax.broadcasted_iota(jnp.int32, (1, LANES), 1) < NA_HEAD_DIM


Q_ROWS = 2
BAND_ROWS = 10
assert BAND_ROWS >= WIN_ROWS + Q_ROWS - 1 and (BAND_ROWS * GRID_W) % LANES == 0


def _na_patterns(rows):
    starts, sigs = [], []
    for g in range(rows // Q_ROWS):
        start = int(np.clip(Q_ROWS * g - WIN_ROWS // 2, 0, rows - BAND_ROWS))
        r = Q_ROWS * g + np.arange(Q_ROWS)
        rs = np.clip(r - WIN_ROWS // 2, 0, rows - WIN_ROWS)
        starts.append(start)
        sigs.append((start - Q_ROWS * g,) + tuple(rs - r))
    run_starts = [g for g in range(len(sigs)) if g == 0 or sigs[g] != sigs[g - 1]]
    assert len(set(sigs)) == len(run_starts)
    return starts, run_starts


def _na_kernel(q_ref, kt_ref, va_ref, vb_ref, kct_ref, vca_ref, vcb_ref, rpb_ref, o_ref, s_ref, p_ref, bias_ref, *,
               seq, n_ctx):
    rows = seq // GRID_W
    per_batch = rows // Q_ROWS
    n_groups = q_ref.shape[0] // seq * per_batch
    n_q = Q_ROWS * GRID_W
    n_loc = BAND_ROWS * GRID_W
    _, run_starts = _na_patterns(rows)
    first = _head_mask()
    assert n_groups >= 4 and seq == per_batch * n_q

    def split(t):
        t = jnp.asarray(t, jnp.int32)
        return t // per_batch, t % per_batch

    def band(t):
        b, g = split(t)
        start = jnp.clip(Q_ROWS * g - WIN_ROWS // 2, 0, rows - BAND_ROWS)
        return pl.multiple_of(b * seq + start * GRID_W, LANES)

    def q_start(t):
        return pl.multiple_of(jnp.asarray(t, jnp.int32) * n_q, n_q)

    def ctx_start(t):
        return pl.multiple_of(split(t)[0] * n_ctx, n_ctx)

    def scores(t, slot):
        g = split(t)[1]
        pat = sum((g >= s).astype(jnp.int32) for s in run_starts[1:])
        q = q_ref[pl.ds(q_start(t), n_q), :]
        kb = kt_ref[:, pl.ds(band(t), n_loc)]
        kc = kct_ref[:, pl.ds(ctx_start(t), n_ctx)]
        for a in range(2):
            qa = jnp.where(first if a == 0 else ~first, q, jnp.zeros_like(q))
            s_ref[slot, a, :, :n_loc] = _bdot(qa, kb) + bias_ref[a, pat]
            s_ref[slot, a, :, n_loc:] = _bdot(qa, kc)

    def softmax(slot):
        for a in range(2):
            s = s_ref[slot, a]
            p_ref[slot, a] = jnp.exp((s - jnp.max(s, axis=-1, keepdims=True)).astype(BF16))

    def values(t, slot):
        outs = []
        for a, (v_ref, vc_ref) in enumerate(((va_ref, vca_ref), (vb_ref, vcb_ref))):
            o = (_bdot(p_ref[slot, a, :, :n_loc], v_ref[pl.ds(band(t), n_loc), :])
                 + _bdot(p_ref[slot, a, :, n_loc:], vc_ref[pl.ds(ctx_start(t), n_ctx), :]))
            outs.append(o * (1.0 / pltpu.roll(o, NA_HEAD_DIM, axis=1)))
        o_ref[pl.ds(q_start(t), n_q), :] = jnp.where(first, outs[0], outs[1]).astype(BF16)

    def tick(t, parity):
        values(t - 1, 1 - parity)
        scores(t + 1, 1 - parity)
        softmax(parity)

    def quad(j, carry):
        t = 4 * j + 1
        tick(t, 1)
        tick(t + 1, 0)
        tick(t + 2, 1)
        tick(t + 3, 0)
        return carry

    @pl.when(pl.program_id(1) == 0)
    def _():
        _na_assemble_bias(rpb_ref, bias_ref, rows)

    scores(0, 0)
    scores(1, 1)
    softmax(0)
    n_quads = (n_groups - 2) // 4
    lax.fori_loop(0, n_quads, quad, 0)
    for t in range(4 * n_quads + 1, n_groups - 1):
        tick(t, t % 2)
    softmax((n_groups - 1) % 2)
    values(n_groups - 2, n_groups % 2)
    values(n_groups - 1, (n_groups - 1) % 2)


def _na_bias(rpb):
    n_heads, n_dr, n_dc = rpb.shape
    r = rpb.astype(F32)
    w = jnp.concatenate([r[..., WIN_COLS - 1:], jnp.zeros((n_heads, n_dr, LANES - n_dc), F32), r[..., :WIN_COLS - 1]],
                        axis=-1)
    w = jnp.pad(w, ((0, 0), (0, 1), (0, 0)))
    return w.reshape(HEAD_PAIRS, 2, n_dr + 1, LANES)


def _na_assemble_bias(w_ref, bias_ref, rows):
    starts, run_starts = _na_patterns(rows)
    cq = lax.broadcasted_iota(jnp.int32, (GRID_W, LANES), 0)
    lane = lax.broadcasted_iota(jnp.int32, (GRID_W, LANES), 1)
    low = lane < GRID_W
    ck = jnp.where(low, lane, lane - GRID_W)
    col_start = jnp.clip(cq - WIN_COLS // 2, 0, GRID_W - WIN_COLS)
    in_window = (ck >= col_start) & (ck < col_start + WIN_COLS)
    masked = jnp.full((GRID_W, LANES), -jnp.inf, F32)
    rolled = {}

    def half(a, d, upper):
        if (a, d, upper) not in rolled:
            row = jnp.broadcast_to(w_ref[0, a, d:d + 1, :], (GRID_W, LANES))
            rolled[a, d, upper] = pltpu.roll(row, GRID_W if upper else 0, axis=1, stride=1, stride_axis=0)
        return rolled[a, d, upper]

    for a in range(2):
        for p, g in enumerate(run_starts):
            for qi in range(Q_ROWS):
                r = Q_ROWS * g + qi
                rs = int(np.clip(r - WIN_ROWS // 2, 0, rows - WIN_ROWS))
                below = rs - starts[g]
                dr0 = rs - r + WIN_ROWS - 1
                for t in range(BAND_ROWS // 2):
                    d0 = dr0 + 2 * t - below
                    v0 = below <= 2 * t < below + WIN_ROWS
                    v1 = below <= 2 * t + 1 < below + WIN_ROWS
                    if v0 and v1:
                        tile = jnp.where(low, half(a, d0, False), half(a, d0 + 1, True))
                    elif v0:
                        tile = jnp.where(low, half(a, d0, False), masked)
                    elif v1:
                        tile = jnp.where(low, masked, half(a, d0 + 1, True))
                    else:
                        tile = masked
                    if v0 or v1:
                        tile = jnp.where(in_window, tile, masked)
                    bias_ref[a, p, qi * GRID_W:(qi + 1) * GRID_W, t * LANES:(t + 1) * LANES] = tile


NA_BATCHES = 4


def _na(qv, kt, v_ctx, kt_ctx, rpb_rows, *, seq, n_ctx, vc_off):
    t = qv.shape[0]
    hp = HEAD_PAIRS
    tb, tbc = NA_BATCHES * seq, NA_BATCHES * n_ctx
    n_pat = len(_na_patterns(seq // GRID_W)[1])
    n_keys = BAND_ROWS * GRID_W + n_ctx
    assert t % tb == 0
    return pl.pallas_call(
        functools.partial(_na_kernel, seq=seq, n_ctx=n_ctx),
        grid=(hp, t // tb),
        in_specs=[pl.BlockSpec((tb, LANES), lambda h, b: (b, h)),
                  pl.BlockSpec((LANES, tb), lambda h, b: (h, b)),
                  pl.BlockSpec((tb, LANES), lambda h, b: (b, hp + h)),
                  pl.BlockSpec((tb, LANES), lambda h, b: (b, 2 * hp + h)),
                  pl.BlockSpec((LANES, tbc), lambda h, b: (h, b)),
                  pl.BlockSpec((tbc, LANES), lambda h, b: (b, vc_off + h)),
                  pl.BlockSpec((tbc, LANES), lambda h, b: (b, vc_off + hp + h)),
                  pl.BlockSpec((1,) + rpb_rows.shape[1:], lambda h, b: (h, 0, 0, 0))],
        out_specs=pl.BlockSpec((tb, LANES), lambda h, b: (b, h)),
        out_shape=jax.ShapeDtypeStruct((t, D_NA), BF16),
        scratch_shapes=[pltpu.VMEM((2, 2, Q_ROWS * GRID_W, n_keys), F32),
                        pltpu.VMEM((2, 2, Q_ROWS * GRID_W, n_keys), BF16),
                        pltpu.VMEM((2, n_pat, Q_ROWS * GRID_W, BAND_ROWS * GRID_W), F32)],
        compiler_params=_params("parallel", "arbitrary"),
        name="na_attention",
    )(qv, kt, qv, qv, kt_ctx, v_ctx, v_ctx, rpb_rows)


def _ctx_attn_kernel(q_ref, kt_ref, va_ref, vb_ref, o_ref):
    first = _head_mask()
    for h in range(HEAD_PAIRS):
        lanes = slice(h * LANES, (h + 1) * LANES)
        q = q_ref[:, lanes]
        kt = kt_ref[lanes, :]
        outs = []
        for a, v_ref in enumerate((va_ref, vb_ref)):
            qa = jnp.where(first if a == 0 else ~first, q, jnp.zeros_like(q))
            s = _bdot(qa, kt)
            p = jnp.exp(s - jnp.max(s, axis=-1, keepdims=True))
            den = jnp.sum(p, axis=-1, keepdims=True)
            outs.append(_bdot(p.astype(BF16), v_ref[:, lanes]) * (1.0 / den))
        o_ref[:, lanes] = jnp.where(first, outs[0], outs[1]).astype(BF16)


def _ctx_attn(qv, kt, *, n_ctx):
    t = qv.shape[0]
    return pl.pallas_call(
        _ctx_attn_kernel,
        grid=(t // n_ctx,),
        in_specs=[pl.BlockSpec((n_ctx, D_NA), lambda b: (b, 0)),
                  pl.BlockSpec((D_NA, n_ctx), lambda b: (0, b)),
                  pl.BlockSpec((n_ctx, D_NA), lambda b: (b, 1)),
                  pl.BlockSpec((n_ctx, D_NA), lambda b: (b, 2))],
        out_specs=pl.BlockSpec((n_ctx, D_NA), lambda b: (b, 0)),
        out_shape=jax.ShapeDtypeStruct((t, D_NA), BF16),
        compiler_params=_params("parallel"),
        name="ctx_attention",
    )(qv, kt, qv, qv)


def _outmlp_kernel(h_ref, yf_ref, cv_ref, at_ref, mod_ref, g_ref, gf_ref, wo_ref, w1_ref, w2_ref, o_ref, *, final_norm):
    mix = (_bdot(yf_ref[...], wo_ref[:D_FOURIER, :])
           + _bdot(cv_ref[...], wo_ref[D_FOURIER:D_FOURIER + D_CONV, :])
           + _bdot(at_ref[...], wo_ref[D_FOURIER + D_CONV:, :]))
    h1 = h_ref[...] + mod_ref[0, 2:3, :] * mix
    hn = _rms_mod(h1, g_ref[...], mod_ref[0, 3:4, :], mod_ref[0, 4:5, :]).astype(BF16)
    a = jnp.maximum(_bdot(hn, w1_ref[...]), 0.0)
    out = h1 + mod_ref[0, 5:6, :] * _bdot((a * a).astype(BF16), w2_ref[...])
    if final_norm:
        out = out * lax.rsqrt(jnp.mean(out * out, axis=-1, keepdims=True) + RMS_EPS) * gf_ref[...]
    o_ref[...] = out


def _outmlp(h, yf, cv, at, mods, g2, gf, wo, w1, w2, *, seq, tm, ctx, final_norm):
    t, d = h.shape
    nb = t // seq
    per = seq // tm
    row = (lambda i: nb) if ctx else (lambda i: i // per)
    d_ff = w1.shape[1]
    vec = pl.BlockSpec((1, d), lambda i: (0, 0))
    once = dict(pipeline_mode=pl.Buffered(1))
    return pl.pallas_call(
        functools.partial(_outmlp_kernel, final_norm=final_norm),
        grid=(t // tm,),
        in_specs=[pl.BlockSpec((tm, d), lambda i: (i, 0)),
                  pl.BlockSpec((tm, D_FOURIER), lambda i: (i % per, i // per)),
                  pl.BlockSpec((tm, D_CONV), lambda i: (i, 0)),
                  pl.BlockSpec((tm, D_NA), lambda i: (i, 0)),
                  pl.BlockSpec((1, N_MOD, d), lambda i: (row(i), 0, 0)),
                  vec, vec,
                  pl.BlockSpec((D_MIX, d), lambda i: (0, 0), **once),
                  pl.BlockSpec((d, d_ff), lambda i: (0, 0), **once),
                  pl.BlockSpec((d_ff, d), lambda i: (0, 0), **once)],
        out_specs=pl.BlockSpec((tm, d), lambda i: (i, 0)),
        out_shape=jax.ShapeDtypeStruct((t, d), F32),
        compiler_params=_params("parallel"),
        name="outmlp_ctx" if ctx else "outmlp_lat",
    )(h, yf, cv, at, mods, g2, gf, wo, w1, w2)


def kernel(x, c, ctx, c_ctx, ada_w, ada_b, norm1_g, norm2_g, w_in, w_fourier, conv_dw_w, conv_dw_b, conv_norm_g,
           conv_norm_b, conv_pw_w, conv_pw_b, na_rpb, w_out, mlp_w1, mlp_w2, final_norm_g):
    nb, seq, d = x.shape
    n_ctx = ctx.shape[1]
    depth = ada_w.shape[0]
    assert nb < MOD_ROWS and seq % GRID_W == 0 and d == D_MODEL

    cc = jnp.concatenate([c, c_ctx[None], jnp.zeros((MOD_ROWS - nb - 1, d), F32)], axis=0)
    mods = _adaln(cc, ada_w, ada_b).reshape(depth, MOD_ROWS, N_MOD, d)
    ab_lat = _fold_fourier(w_fourier, seq)
    ab_ctx = _fold_fourier(w_fourier, n_ctx)
    dft_lat = _dft_mats(seq)
    dft_ctx = _dft_mats(n_ctx)

    w_in_b = w_in.astype(BF16)
    w_out_b = w_out.astype(BF16)
    w1_b = mlp_w1.astype(BF16)
    w2_b = mlp_w2.astype(BF16)
    pww_b = conv_pw_w.astype(BF16)
    gf = final_norm_g.reshape(1, d)

    h_lat = x.reshape(nb * seq, d)
    h_ctx = ctx.reshape(nb * n_ctx, d)
    for i in range(depth):
        last = i == depth - 1
        g1 = norm1_g[i].reshape(1, d)
        g2 = norm2_g[i].reshape(1, d)
        conv_p = (conv_dw_w[i], conv_dw_b[i].reshape(1, -1), conv_norm_g[i].reshape(1, -1),
                  conv_norm_b[i].reshape(1, -1), pww_b[i], conv_pw_b[i].reshape(1, -1))
        bias = _na_bias(na_rpb[i])
        wkt = w_in_b[i][:, KV_START:KV_START + D_NA].T

        if last:
            v_ctx, kt_ctx = _kvproj(h_ctx, g1, mods[i], w_in_b[i], wkt, nb=nb, tm=n_ctx)
            vc_off = 0
        else:
            zf_c, uc_c, qv_c, kt_ctx = _inproj(h_ctx, g1, mods[i], w_in_b[i], wkt, ab_ctx[i],
                                               seq=n_ctx, tm=n_ctx, ctx=True)
            v_ctx = qv_c
            vc_off = HEAD_PAIRS

        zf, uc, qv, kt = _inproj(h_lat, g1, mods[i], w_in_b[i], wkt, ab_lat[i], seq=seq, tm=TM_INPROJ, ctx=False)
        yf = _fourier(dft_lat, zf)
        cv = _conv(uc, *conv_p, seq=seq)
        at = _na(qv, kt, v_ctx, kt_ctx, bias, seq=seq, n_ctx=n_ctx, vc_off=vc_off)
        h_lat = _outmlp(h_lat, yf, cv, at, mods[i], g2, gf, w_out_b[i], w1_b[i], w2_b[i],
                        seq=seq, tm=TM_MLP, ctx=False, final_norm=last)

        if not last:
            yf_c = _fourier(dft_ctx, zf_c)
            cv_c = _conv(uc_c, *conv_p, seq=n_ctx)
            at_c = _ctx_attn(qv_c, kt_ctx, n_ctx=n_ctx)
            h_ctx = _outmlp(h_ctx, yf_c, cv_c, at_c, mods[i], g2, gf, w_out_b[i], w1_b[i], w2_b[i],
                            seq=n_ctx, tm=n_ctx, ctx=True, final_norm=False)
    return h_lat.reshape(nb, seq, d)
```

```python
import functools

import numpy as np
import jax
import jax.numpy as jnp
from jax import lax
from jax.experimental import pallas as pl
from jax.experimental.pallas import tpu as pltpu

D_MODEL = 1024
GRID_W = 64
D_FOURIER = 256
FOURIER_GROUPS = 4
D_CONV = 256
CONV_WIDTH = 31
N_NA_HEADS = 8
NA_HEAD_DIM = 64
D_NA = N_NA_HEADS * NA_HEAD_DIM
WIN_ROWS = 8
WIN_COLS = 16
QKV_START = D_FOURIER + 2 * D_CONV
KV_START = QKV_START + D_NA
D_MIX = D_FOURIER + D_CONV + D_NA
N_MOD = 6
RMS_EPS = 1e-6
LN_EPS = 1e-5

LANES = 128
SUBLANES = 8
HEAD_PAIRS = N_NA_HEADS * NA_HEAD_DIM // LANES
MOD_ROWS = 16
VMEM_LIMIT = 56 * 1024 * 1024

TM_INPROJ = 1024
TM_MLP = 512
TN_ADALN = 2048

F32 = jnp.float32
BF16 = jnp.bfloat16


def _params(*sem):
    return pltpu.CompilerParams(dimension_semantics=sem, vmem_limit_bytes=VMEM_LIMIT)


def _sigmoid(x):
    return 1.0 / (1.0 + jnp.exp(-x))


def _rms_mod(x, g, shift, scale):
    return x * lax.rsqrt(jnp.mean(x * x, axis=-1, keepdims=True) + RMS_EPS) * (g * (1.0 + scale)) + shift


def _bdot(a, b):
    return jnp.dot(a, b, preferred_element_type=F32)


def _adaln_kernel(c_ref, w_ref, b_ref, o_ref):
    cc = c_ref[...]
    a = cc * _sigmoid(cc)
    w = w_ref[0]
    a_hi = a.astype(BF16)
    a_lo = (a - a_hi.astype(F32)).astype(BF16)
    w_hi = w.astype(BF16)
    w_lo = (w - w_hi.astype(F32)).astype(BF16)
    by_hi = _bdot(jnp.concatenate([a_hi, a_lo], axis=0), w_hi)
    o_ref[0] = by_hi[:MOD_ROWS] + by_hi[MOD_ROWS:] + _bdot(a_hi, w_lo) + b_ref[0]


def _adaln(cc, ada_w, ada_b):
    depth, d, n = ada_w.shape
    tn = TN_ADALN
    return pl.pallas_call(
        _adaln_kernel,
        grid=(depth, n // tn),
        in_specs=[pl.BlockSpec((MOD_ROWS, d), lambda l, j: (0, 0)),
                  pl.BlockSpec((1, d, tn), lambda l, j: (l, 0, j)),
                  pl.BlockSpec((1, 1, tn), lambda l, j: (l, 0, j))],
        out_specs=pl.BlockSpec((1, MOD_ROWS, tn), lambda l, j: (l, 0, j)),
        out_shape=jax.ShapeDtypeStruct((depth, MOD_ROWS, n), F32),
        compiler_params=_params("parallel", "parallel"),
        name="adaln",
    )(cc, ada_w, ada_b.reshape(depth, 1, n))


def _fold_kernel(cc_ref, sc_ref, w_ref, o_ref):
    w = w_ref[0]
    hi = lax.Precision.HIGHEST
    o_ref[0, :, :D_FOURIER] = jnp.dot(cc_ref[...], w, preferred_element_type=F32, precision=hi).astype(BF16)
    o_ref[0, :, D_FOURIER:] = jnp.dot(sc_ref[...], w, preferred_element_type=F32, precision=hi).astype(BF16)


def _fold_fourier(w_fourier, seq):
    depth = w_fourier.shape[0]
    gs = D_FOURIER // FOURIER_GROUPS
    idx = np.arange(D_FOURIER)
    same = (idx[:, None] // gs) == (idx[None, :] // gs)
    ang = 2.0 * np.pi * (((idx[:, None] % gs) * (idx[None, :] % gs)) % gs) / gs
    scale = 1.0 / np.sqrt(seq * gs)
    cc = jnp.asarray(np.where(same, np.cos(ang), 0.0) * scale, F32)
    sc = jnp.asarray(np.where(same, np.sin(ang), 0.0) * scale, F32)
    full = pl.BlockSpec((D_FOURIER, D_FOURIER), lambda l: (0, 0))
    return pl.pallas_call(
        _fold_kernel,
        grid=(depth,),
        in_specs=[full, full, pl.BlockSpec((1, D_FOURIER, D_FOURIER), lambda l: (l, 0, 0))],
        out_specs=pl.BlockSpec((1, D_FOURIER, 2 * D_FOURIER), lambda l: (l, 0, 0)),
        out_shape=jax.ShapeDtypeStruct((depth, D_FOURIER, 2 * D_FOURIER), BF16),
        compiler_params=_params("parallel"),
        name="fold_fourier",
    )(cc, sc, w_fourier)


FLIP_BLOCK = 128


def _dft_mats(seq):
    half = seq // 2
    k = np.arange(half)
    ang = 2.0 * np.pi * ((k[:, None] * k[None, :]) % seq) / seq
    r = np.arange(FLIP_BLOCK)
    perm = (np.arange(2 * FLIP_BLOCK)[None, :] == FLIP_BLOCK - r[:, None]).astype(np.float32)
    alt_col = np.where(k % 2 == 0, 1.0, -1.0).astype(np.float32)[:, None]
    alt_row = np.zeros((SUBLANES, seq), np.float32)
    alt_row[0] = np.where(np.arange(seq) % 2 == 0, 1.0, -1.0)
    return (jnp.asarray(np.cos(ang), F32).astype(BF16), jnp.asarray(np.sin(ang), F32).astype(BF16),
            jnp.asarray(perm, F32).astype(BF16), jnp.asarray(alt_col), jnp.asarray(alt_row, F32).astype(BF16))


def _store_kv(hn, w_ref, out_ref, col):
    kv = _bdot(hn, w_ref[:, KV_START:])
    v = kv[:, D_NA:]
    first = lax.broadcasted_iota(jnp.int32, (1, D_NA), 1) % LANES < NA_HEAD_DIM
    out_ref[:, col:col + D_NA] = kv[:, :D_NA].astype(BF16)
    out_ref[:, col + D_NA:col + 2 * D_NA] = jnp.where(first, v, 1.0).astype(BF16)
    out_ref[:, col + 2 * D_NA:col + 3 * D_NA] = jnp.where(first, 1.0, v).astype(BF16)


def _inproj_kernel(x_ref, g_ref, mod_ref, w_ref, ab_ref, zf_ref, uc_ref, qkv_ref):
    hn = _rms_mod(x_ref[...], g_ref[...], mod_ref[0, 0:1, :], mod_ref[0, 1:2, :]).astype(BF16)
    uf = _bdot(hn, w_ref[:, :D_FOURIER])
    uc_ref[...] = _bdot(hn, w_ref[:, D_FOURIER:QKV_START])
    qkv_ref[:, :D_NA] = (_bdot(hn, w_ref[:, QKV_START:KV_START]) * (NA_HEAD_DIM ** -0.5)).astype(BF16)
    _store_kv(hn, w_ref, qkv_ref, D_NA)
    zf_ref[...] = _bdot(uf.astype(BF16), ab_ref[...]).astype(BF16)


def _inproj(h, g, mods, w_in, ab, *, layer, seq, tm, ctx):
    t, d = h.shape
    nb = t // seq
    per = seq // tm
    row = (lambda i: nb) if ctx else (lambda i: i // per)
    d_in = w_in.shape[2]
    return pl.pallas_call(
        _inproj_kernel,
        grid=(t // tm,),
        in_specs=[pl.BlockSpec((tm, d), lambda i: (i, 0)),
                  pl.BlockSpec((1, d), lambda i: (0, 0)),
                  pl.BlockSpec((1, N_MOD, d), lambda i: (row(i), 0, 0)),
                  pl.BlockSpec((None, d, d_in), lambda i: (layer, 0, 0), pipeline_mode=pl.Buffered(1)),
                  pl.BlockSpec((D_FOURIER, 2 * D_FOURIER), lambda i: (0, 0))],
        out_specs=[pl.BlockSpec((tm, 2 * D_FOURIER), lambda i: (i % per, i // per)),
                   pl.BlockSpec((tm, 2 * D_CONV), lambda i: (i, 0)),
                   pl.BlockSpec((tm, 4 * D_NA), lambda i: (i, 0))],
        out_shape=[jax.ShapeDtypeStruct((seq, nb * 2 * D_FOURIER), BF16),
                   jax.ShapeDtypeStruct((t, 2 * D_CONV), F32),
                   jax.ShapeDtypeStruct((t, 4 * D_NA), BF16)],
        compiler_params=_params("parallel"),
        name="inproj_ctx" if ctx else "inproj_lat",
    )(h, g, mods, w_in, ab)


def _kvproj_kernel(x_ref, g_ref, mod_ref, w_ref, kv_ref):
    hn = _rms_mod(x_ref[...], g_ref[...], mod_ref[0, 0:1, :], mod_ref[0, 1:2, :]).astype(BF16)
    _store_kv(hn, w_ref, kv_ref, 0)


def _kvproj(h, g, mods, w_in, *, layer, nb, tm):
    t, d = h.shape
    d_in = w_in.shape[2]
    n = 3 * D_NA
    return pl.pallas_call(
        _kvproj_kernel,
        grid=(t // tm,),
        in_specs=[pl.BlockSpec((tm, d), lambda i: (i, 0)),
                  pl.BlockSpec((1, d), lambda i: (0, 0)),
                  pl.BlockSpec((1, N_MOD, d), lambda i: (nb, 0, 0)),
                  pl.BlockSpec((None, d, d_in), lambda i: (layer, 0, 0))],
        out_specs=pl.BlockSpec((tm, n), lambda i: (i, 0)),
        out_shape=jax.ShapeDtypeStruct((t, n), BF16),
        compiler_params=_params("parallel"),
        name="kvproj_ctx",
    )(h, g, mods, w_in)


def _fourier_kernel(c_ref, s_ref, perm_ref, altc_ref, altr_ref, z_ref, y_ref, ze_ref, yr_ref):
    seq = z_ref.shape[0]
    half = seq // 2
    nblk = half // FLIP_BLOCK
    fb = FLIP_BLOCK

    def reversed_block(ref, i, n):
        if i == 0:
            return _bdot(perm_ref[:, :fb], ref[n - fb:n, :])
        lo = n - (i + 1) * fb
        return _bdot(perm_ref[...], ref[lo:lo + 2 * fb, :])

    for i in range(nblk):
        rows_i = slice(i * fb, (i + 1) * fb)
        zr = reversed_block(z_ref, i, seq)
        ze_ref[rows_i, :D_FOURIER] = (z_ref[rows_i, :D_FOURIER].astype(F32) + zr[:, :D_FOURIER]).astype(BF16)
        ze_ref[rows_i, D_FOURIER:] = (z_ref[rows_i, D_FOURIER:].astype(F32) - zr[:, D_FOURIER:]).astype(BF16)
    nyq = z_ref[half:half + 1, :D_FOURIER].astype(F32)
    yc = _bdot(c_ref[...], ze_ref[:, :D_FOURIER]) + altc_ref[...] * nyq
    ys = _bdot(s_ref[...], ze_ref[:, D_FOURIER:])
    y_ref[:half, :] = (yc - ys).astype(BF16)
    yr_ref[...] = (yc + ys).astype(BF16)
    y_mid = _bdot(altr_ref[...], z_ref[:, :D_FOURIER])[0:1, :]
    first_row = lax.broadcasted_iota(jnp.int32, (fb, 1), 0) == 0
    for i in range(nblk):
        blk = reversed_block(yr_ref, i, half)
        if i == 0:
            blk = jnp.where(first_row, y_mid, blk)
        y_ref[half + i * fb:half + (i + 1) * fb, :] = blk.astype(BF16)


def _fourier(mats, zf):
    cmat, smat, perm, alt_col, alt_row = mats
    seq = alt_row.shape[1]
    half = seq // 2
    nb = zf.shape[1] // (2 * D_FOURIER)
    const = lambda a: pl.BlockSpec(a.shape, lambda b: (0, 0))
    return pl.pallas_call(
        _fourier_kernel,
        grid=(nb,),
        in_specs=[const(cmat), const(smat), const(perm), const(alt_col), const(alt_row),
                  pl.BlockSpec((seq, 2 * D_FOURIER), lambda b: (0, b))],
        out_specs=pl.BlockSpec((seq, D_FOURIER), lambda b: (0, b)),
        out_shape=jax.ShapeDtypeStruct((seq, nb * D_FOURIER), BF16),
        scratch_shapes=[pltpu.VMEM((half, 2 * D_FOURIER), BF16), pltpu.VMEM((half, D_FOURIER), BF16)],
        compiler_params=_params("parallel"),
        name="fourier_pos",
    )(cmat, smat, perm, alt_col, alt_row, zf)


CONV_PAD = 16
CONV_CHUNK = 512


def _conv_kernel(u_ref, dww_ref, dwb_ref, lng_ref, lnb_ref, pww_ref, pwb_ref, o_ref, vs_ref):
    seq = u_ref.shape[0]
    n_pad = seq + 2 * CONV_PAD
    zeros = jnp.zeros((CONV_PAD, D_CONV), F32)
    vs_ref[0, 0:CONV_PAD, :] = zeros
    vs_ref[0, seq + CONV_PAD:n_pad, :] = zeros
    vs_ref[0, CONV_PAD:CONV_PAD + seq, :] = u_ref[:, :D_CONV] * _sigmoid(u_ref[:, D_CONV:])
    for s in range(1, SUBLANES):
        vs_ref[s, 0:n_pad - SUBLANES, :] = vs_ref[0, s:s + n_pad - SUBLANES, :]
    first = CONV_PAD - CONV_WIDTH // 2
    assert (first + CONV_WIDTH - 1) // SUBLANES * SUBLANES + seq <= n_pad - SUBLANES

    rows = min(CONV_CHUNK, seq)

    def chunk(ci, carry):
        base = pl.multiple_of(ci * rows, rows)
        acc = jnp.zeros((rows, D_CONV), F32) + dwb_ref[...]
        for t in range(CONV_WIDTH):
            s, a = (first + t) % SUBLANES, (first + t) // SUBLANES
            acc = acc + vs_ref[s, pl.ds(base + a * SUBLANES, rows), :] * dww_ref[t:t + 1, :]
        mu = jnp.mean(acc, axis=-1, keepdims=True)
        cen = acc - mu
        var = jnp.mean(cen * cen, axis=-1, keepdims=True)
        y = cen * lax.rsqrt(var + LN_EPS) * lng_ref[...] + lnb_ref[...]
        y = y * _sigmoid(y)
        o_ref[pl.ds(base, rows), :] = (_bdot(y.astype(BF16), pww_ref[...]) + pwb_ref[...]).astype(BF16)
        return carry

    lax.fori_loop(0, seq // rows, chunk, 0)


def _conv(uc, dww, dwb, lng, lnb, pww, pwb, *, seq):
    t = uc.shape[0]
    vec = pl.BlockSpec((1, D_CONV), lambda b: (0, 0))
    return pl.pallas_call(
        _conv_kernel,
        grid=(t // seq,),
        in_specs=[pl.BlockSpec((seq, 2 * D_CONV), lambda b: (b, 0)),
                  pl.BlockSpec((CONV_WIDTH, D_CONV), lambda b: (0, 0)),
                  vec, vec, vec,
                  pl.BlockSpec((D_CONV, D_CONV), lambda b: (0, 0)),
                  vec],
        out_specs=pl.BlockSpec((seq, D_CONV), lambda b: (b, 0)),
        out_shape=jax.ShapeDtypeStruct((t, D_CONV), BF16),
        scratch_shapes=[pltpu.VMEM((SUBLANES, seq + 2 * CONV_PAD, D_CONV), F32)],
        compiler_params=_params("parallel"),
        name="conv_module",
    )(uc, dww, dwb, lng, lnb, pww, pwb)


def _head_mask():
    return lax.broadcasted_iota(jnp.int32, (1, LANES), 1) < NA_HEAD_DIM


Q_ROWS = 2
BAND_ROWS = 10
assert BAND_ROWS >= WIN_ROWS + Q_ROWS - 1 and (BAND_ROWS * GRID_W) % LANES == 0


def _na_patterns(rows):
    starts, sigs = [], []
    for g in range(rows // Q_ROWS):
        start = int(np.clip(Q_ROWS * g - WIN_ROWS // 2, 0, rows - BAND_ROWS))
        r = Q_ROWS * g + np.arange(Q_ROWS)
        rs = np.clip(r - WIN_ROWS // 2, 0, rows - WIN_ROWS)
        starts.append(start)
        sigs.append((start - Q_ROWS * g,) + tuple(rs - r))
    run_starts = [g for g in range(len(sigs)) if g == 0 or sigs[g] != sigs[g - 1]]
    assert len(set(sigs)) == len(run_starts)
    return starts, run_starts


def _na_kernel(q_ref, k_ref, va_ref, vb_ref, kc_ref, vca_ref, vcb_ref, rpb_ref, o_ref, s_ref, p_ref, bias_ref, *,
               seq, n_ctx):
    rows = seq // GRID_W
    per_batch = rows // Q_ROWS
    n_groups = q_ref.shape[0] // seq * per_batch
    n_q = Q_ROWS * GRID_W
    n_loc = BAND_ROWS * GRID_W
    _, run_starts = _na_patterns(rows)
    first = _head_mask()
    dn = (((1,), (1,)), ((), ()))
    assert n_groups >= 4 and seq == per_batch * n_q

    def split(t):
        t = jnp.asarray(t, jnp.int32)
        return t // per_batch, t % per_batch

    def band(t):
        b, g = split(t)
        start = jnp.clip(Q_ROWS * g - WIN_ROWS // 2, 0, rows - BAND_ROWS)
        return pl.multiple_of(b * seq + start * GRID_W, LANES)

    def q_start(t):
        return pl.multiple_of(jnp.asarray(t, jnp.int32) * n_q, n_q)

    def ctx_start(t):
        return pl.multiple_of(split(t)[0] * n_ctx, n_ctx)

    def scores(t, slot):
        g = split(t)[1]
        pat = sum((g >= s).astype(jnp.int32) for s in run_starts[1:])
        q = q_ref[pl.ds(q_start(t), n_q), :]
        kb = k_ref[pl.ds(band(t), n_loc), :]
        kc = kc_ref[pl.ds(ctx_start(t), n_ctx), :]
        for a in range(2):
            qa = jnp.where(first if a == 0 else ~first, q, jnp.zeros_like(q))
            s_ref[slot, a, :, :n_loc] = lax.dot_general(qa, kb, dn, preferred_element_type=F32) + bias_ref[a, pat]
            s_ref[slot, a, :, n_loc:] = lax.dot_general(qa, kc, dn, preferred_element_type=F32)

    def softmax(slot):
        for a in range(2):
            s = s_ref[slot, a]
            p_ref[slot, a] = jnp.exp((s - jnp.max(s, axis=-1, keepdims=True)).astype(BF16))

    def values(t, slot):
        outs = []
        for a, (v_ref, vc_ref) in enumerate(((va_ref, vca_ref), (vb_ref, vcb_ref))):
            o = (_bdot(p_ref[slot, a, :, :n_loc], v_ref[pl.ds(band(t), n_loc), :])
                 + _bdot(p_ref[slot, a, :, n_loc:], vc_ref[pl.ds(ctx_start(t), n_ctx), :]))
            outs.append(o * (1.0 / pltpu.roll(o, NA_HEAD_DIM, axis=1)))
        o_ref[pl.ds(q_start(t), n_q), :] = jnp.where(first, outs[0], outs[1]).astype(BF16)

    def tick(t, parity):
        values(t - 1, 1 - parity)
        scores(t + 1, 1 - parity)
        softmax(parity)

    def quad(j, carry):
        t = 4 * j + 1
        tick(t, 1)
        tick(t + 1, 0)
        tick(t + 2, 1)
        tick(t + 3, 0)
        return carry

    @pl.when(pl.program_id(1) == 0)
    def _():
        _na_assemble_bias(rpb_ref, bias_ref, rows)

    scores(0, 0)
    scores(1, 1)
    softmax(0)
    n_quads = (n_groups - 2) // 4
    lax.fori_loop(0, n_quads, quad, 0)
    for t in range(4 * n_quads + 1, n_groups - 1):
        tick(t, t % 2)
    softmax((n_groups - 1) % 2)
    values(n_groups - 2, n_groups % 2)
    values(n_groups - 1, (n_groups - 1) % 2)


def _na_bias(rpb):
    n_heads, n_dr, n_dc = rpb.shape
    r = rpb.astype(F32)
    w = jnp.concatenate([r[..., WIN_COLS - 1:], jnp.zeros((n_heads, n_dr, LANES - n_dc), F32), r[..., :WIN_COLS - 1]],
                        axis=-1)
    w = jnp.pad(w, ((0, 0), (0, 1), (0, 0)))
    return w.reshape(HEAD_PAIRS, 2, n_dr + 1, LANES)


def _na_assemble_bias(w_ref, bias_ref, rows):
    starts, run_starts = _na_patterns(rows)
    cq = lax.broadcasted_iota(jnp.int32, (GRID_W, LANES), 0)
    lane = lax.broadcasted_iota(jnp.int32, (GRID_W, LANES), 1)
    low = lane < GRID_W
    ck = jnp.where(low, lane, lane - GRID_W)
    col_start = jnp.clip(cq - WIN_COLS // 2, 0, GRID_W - WIN_COLS)
    in_window = (ck >= col_start) & (ck < col_start + WIN_COLS)
    masked = jnp.full((GRID_W, LANES), -jnp.inf, F32)
    rolled = {}

    def half(a, d, upper):
        if (a, d, upper) not in rolled:
            row = jnp.broadcast_to(w_ref[0, a, d:d + 1, :], (GRID_W, LANES))
            rolled[a, d, upper] = pltpu.roll(row, GRID_W if upper else 0, axis=1, stride=1, stride_axis=0)
        return rolled[a, d, upper]

    for a in range(2):
        for p, g in enumerate(run_starts):
            for qi in range(Q_ROWS):
                r = Q_ROWS * g + qi
                rs = int(np.clip(r - WIN_ROWS // 2, 0, rows - WIN_ROWS))
                below = rs - starts[g]
                dr0 = rs - r + WIN_ROWS - 1
                for t in range(BAND_ROWS // 2):
                    d0 = dr0 + 2 * t - below
                    v0 = below <= 2 * t < below + WIN_ROWS
                    v1 = below <= 2 * t + 1 < below + WIN_ROWS
                    if v0 and v1:
                        tile = jnp.where(low, half(a, d0, False), half(a, d0 + 1, True))
                    elif v0:
                        tile = jnp.where(low, half(a, d0, False), masked)
                    elif v1:
                        tile = jnp.where(low, masked, half(a, d0 + 1, True))
                    else:
                        tile = masked
                    if v0 or v1:
                        tile = jnp.where(in_window, tile, masked)
                    bias_ref[a, p, qi * GRID_W:(qi + 1) * GRID_W, t * LANES:(t + 1) * LANES] = tile


NA_BATCHES = 4


def _na(qkv, kv_ctx, rpb_rows, *, seq, n_ctx, kc_off):
    t = qkv.shape[0]
    hp = HEAD_PAIRS
    tb, tbc = NA_BATCHES * seq, NA_BATCHES * n_ctx
    n_pat = len(_na_patterns(seq // GRID_W)[1])
    n_keys = BAND_ROWS * GRID_W + n_ctx
    assert t % tb == 0
    return pl.pallas_call(
        functools.partial(_na_kernel, seq=seq, n_ctx=n_ctx),
        grid=(hp, t // tb),
        in_specs=[pl.BlockSpec((tb, LANES), lambda h, b: (b, h)),
                  pl.BlockSpec((tb, LANES), lambda h, b: (b, hp + h)),
                  pl.BlockSpec((tb, LANES), lambda h, b: (b, 2 * hp + h)),
                  pl.BlockSpec((tb, LANES), lambda h, b: (b, 3 * hp + h)),
                  pl.BlockSpec((tbc, LANES), lambda h, b: (b, kc_off + h)),
                  pl.BlockSpec((tbc, LANES), lambda h, b: (b, kc_off + hp + h)),
                  pl.BlockSpec((tbc, LANES), lambda h, b: (b, kc_off + 2 * hp + h)),
                  pl.BlockSpec((1,) + rpb_rows.shape[1:], lambda h, b: (h, 0, 0, 0))],
        out_specs=pl.BlockSpec((tb, LANES), lambda h, b: (b, h)),
        out_shape=jax.ShapeDtypeStruct((t, D_NA), BF16),
        scratch_shapes=[pltpu.VMEM((2, 2, Q_ROWS * GRID_W, n_keys), F32),
                        pltpu.VMEM((2, 2, Q_ROWS * GRID_W, n_keys), BF16),
                        pltpu.VMEM((2, n_pat, Q_ROWS * GRID_W, BAND_ROWS * GRID_W), F32)],
        compiler_params=_params("parallel", "arbitrary"),
        name="na_attention",
    )(qkv, qkv, qkv, qkv, kv_ctx, kv_ctx, kv_ctx, rpb_rows)


def _ctx_attn_kernel(q_ref, k_ref, va_ref, vb_ref, o_ref):
    first = _head_mask()
    for h in range(HEAD_PAIRS):
        lanes = slice(h * LANES, (h + 1) * LANES)
        q = q_ref[:, lanes]
        k = k_ref[:, lanes]
        outs = []
        for a, v_ref in enumerate((va_ref, vb_ref)):
            qa = jnp.where(first if a == 0 else ~first, q, jnp.zeros_like(q))
            s = lax.dot_general(qa, k, (((1,), (1,)), ((), ())), preferred_element_type=F32)
            p = jnp.exp(s - jnp.max(s, axis=-1, keepdims=True))
            den = jnp.sum(p, axis=-1, keepdims=True)
            outs.append(_bdot(p.astype(BF16), v_ref[:, lanes]) * (1.0 / den))
        o_ref[:, lanes] = jnp.where(first, outs[0], outs[1]).astype(BF16)


def _ctx_attn(qkv, *, n_ctx):
    t = qkv.shape[0]
    return pl.pallas_call(
        _ctx_attn_kernel,
        grid=(t // n_ctx,),
        in_specs=[pl.BlockSpec((n_ctx, D_NA), lambda b: (b, 0)),
                  pl.BlockSpec((n_ctx, D_NA), lambda b: (b, 1)),
                  pl.BlockSpec((n_ctx, D_NA), lambda b: (b, 2)),
                  pl.BlockSpec((n_ctx, D_NA), lambda b: (b, 3))],
        out_specs=pl.BlockSpec((n_ctx, D_NA), lambda b: (b, 0)),
        out_shape=jax.ShapeDtypeStruct((t, D_NA), BF16),
        compiler_params=_params("parallel"),
        name="ctx_attention",
    )(qkv, qkv, qkv, qkv)


def _outmlp_kernel(h_ref, yf_ref, cv_ref, at_ref, mod_ref, g_ref, gf_ref, wo_ref, w1_ref, w2_ref, o_ref, *, final_norm):
    mix = (_bdot(yf_ref[...], wo_ref[:D_FOURIER, :])
           + _bdot(cv_ref[...], wo_ref[D_FOURIER:D_FOURIER + D_CONV, :])
           + _bdot(at_ref[...], wo_ref[D_FOURIER + D_CONV:, :]))
    h1 = h_ref[...] + mod_ref[0, 2:3, :] * mix
    hn = _rms_mod(h1, g_ref[...], mod_ref[0, 3:4, :], mod_ref[0, 4:5, :]).astype(BF16)
    a = jnp.maximum(_bdot(hn, w1_ref[...]), 0.0)
    out = h1 + mod_ref[0, 5:6, :] * _bdot((a * a).astype(BF16), w2_ref[...])
    if final_norm:
        out = out * lax.rsqrt(jnp.mean(out * out, axis=-1, keepdims=True) + RMS_EPS) * gf_ref[...]
    o_ref[...] = out


def _outmlp(h, yf, cv, at, mods, g2, gf, wo, w1, w2, *, layer, seq, tm, ctx, final_norm):
    t, d = h.shape
    nb = t // seq
    per = seq // tm
    row = (lambda i: nb) if ctx else (lambda i: i // per)
    d_ff = w1.shape[2]
    vec = pl.BlockSpec((1, d), lambda i: (0, 0))
    once = dict(pipeline_mode=pl.Buffered(1))
    return pl.pallas_call(
        functools.partial(_outmlp_kernel, final_norm=final_norm),
        grid=(t // tm,),
        in_specs=[pl.BlockSpec((tm, d), lambda i: (i, 0)),
                  pl.BlockSpec((tm, D_FOURIER), lambda i: (i % per, i // per)),
                  pl.BlockSpec((tm, D_CONV), lambda i: (i, 0)),
                  pl.BlockSpec((tm, D_NA), lambda i: (i, 0)),
                  pl.BlockSpec((1, N_MOD, d), lambda i: (row(i), 0, 0)),
                  vec, vec,
                  pl.BlockSpec((None, D_MIX, d), lambda i: (layer, 0, 0), **once),
                  pl.BlockSpec((None, d, d_ff), lambda i: (layer, 0, 0), **once),
                  pl.BlockSpec((None, d_ff, d), lambda i: (layer, 0, 0), **once)],
        out_specs=pl.BlockSpec((tm, d), lambda i: (i, 0)),
        out_shape=jax.ShapeDtypeStruct((t, d), F32),
        compiler_params=_params("parallel"),
        name="outmlp_ctx" if ctx else "outmlp_lat",
    )(h, yf, cv, at, mods, g2, gf, wo, w1, w2)


def kernel(x, c, ctx, c_ctx, ada_w, ada_b, norm1_g, norm2_g, w_in, w_fourier, conv_dw_w, conv_dw_b, conv_norm_g,
           conv_norm_b, conv_pw_w, conv_pw_b, na_rpb, w_out, mlp_w1, mlp_w2, final_norm_g):
    nb, seq, d = x.shape
    n_ctx = ctx.shape[1]
    depth = ada_w.shape[0]
    assert nb < MOD_ROWS and seq % GRID_W == 0 and d == D_MODEL

    cc = jnp.concatenate([c, c_ctx[None], jnp.zeros((MOD_ROWS - nb - 1, d), F32)], axis=0)
    mods = _adaln(cc, ada_w, ada_b).reshape(depth, MOD_ROWS, N_MOD, d)
    ab_lat = _fold_fourier(w_fourier, seq)
    ab_ctx = _fold_fourier(w_fourier, n_ctx)
    dft_lat = _dft_mats(seq)
    dft_ctx = _dft_mats(n_ctx)

    w_in_b = w_in.astype(BF16)
    w_out_b = w_out.astype(BF16)
    w1_b = mlp_w1.astype(BF16)
    w2_b = mlp_w2.astype(BF16)
    pww_b = conv_pw_w.astype(BF16)
    gf = final_norm_g.reshape(1, d)

    h_lat = x.reshape(nb * seq, d)
    h_ctx = ctx.reshape(nb * n_ctx, d)
    for i in range(depth):
        last = i == depth - 1
        g1 = norm1_g[i].reshape(1, d)
        g2 = norm2_g[i].reshape(1, d)
        conv_p = (conv_dw_w[i], conv_dw_b[i].reshape(1, -1), conv_norm_g[i].reshape(1, -1),
                  conv_norm_b[i].reshape(1, -1), pww_b[i], conv_pw_b[i].reshape(1, -1))
        bias = _na_bias(na_rpb[i])

        if last:
            kv_ctx = _kvproj(h_ctx, g1, mods[i], w_in_b, layer=i, nb=nb, tm=n_ctx)
            kc_off = 0
        else:
            zf_c, uc_c, qkv_c = _inproj(h_ctx, g1, mods[i], w_in_b, ab_ctx[i], layer=i, seq=n_ctx, tm=n_ctx, ctx=True)
            kv_ctx = qkv_c
            kc_off = HEAD_PAIRS

        zf, uc, qkv = _inproj(h_lat, g1, mods[i], w_in_b, ab_lat[i], layer=i, seq=seq, tm=TM_INPROJ, ctx=False)
        yf = _fourier(dft_lat, zf)
        cv = _conv(uc, *conv_p, seq=seq)
        at = _na(qkv, kv_ctx, bias, seq=seq, n_ctx=n_ctx, kc_off=kc_off)
        h_lat = _outmlp(h_lat, yf, cv, at, mods[i], g2, gf, w_out_b, w1_b, w2_b,
                        layer=i, seq=seq, tm=TM_MLP, ctx=False, final_norm=last)

        if not last:
            yf_c = _fourier(dft_ctx, zf_c)
            cv_c = _conv(uc_c, *conv_p, seq=n_ctx)
            at_c = _ctx_attn(qkv_c, n_ctx=n_ctx)
            h_ctx = _outmlp(h_ctx, yf_c, cv_c, at_c, mods[i], g2, gf, w_out_b, w1_b, w2_b,
                            layer=i, seq=n_ctx, tm=n_ctx, ctx=True, final_norm=False)
    return h_lat.reshape(nb, seq, d)
```

```python
import functools

import numpy as np
import jax
import jax.numpy as jnp
from jax import lax
from jax.experimental import pallas as pl
from jax.experimental.pallas import tpu as pltpu

D_MODEL = 1024
GRID_W = 64
D_FOURIER = 256
FOURIER_GROUPS = 4
D_CONV = 256
CONV_WIDTH = 31
N_NA_HEADS = 8
NA_HEAD_DIM = 64
D_NA = N_NA_HEADS * NA_HEAD_DIM
WIN_ROWS = 8
WIN_COLS = 16
QKV_START = D_FOURIER + 2 * D_CONV
KV_START = QKV_START + D_NA
D_MIX = D_FOURIER + D_CONV + D_NA
N_MOD = 6
RMS_EPS = 1e-6
LN_EPS = 1e-5

LANES = 128
SUBLANES = 8
HEAD_PAIRS = N_NA_HEADS * NA_HEAD_DIM // LANES
MOD_ROWS = 16
VMEM_LIMIT = 56 * 1024 * 1024

TM_INPROJ = 1024
TM_MLP = 512
TN_ADALN = 2048

F32 = jnp.float32
BF16 = jnp.bfloat16


def _params(*sem):
    return pltpu.CompilerParams(dimension_semantics=sem, vmem_limit_bytes=VMEM_LIMIT)


def _sigmoid(x):
    return 1.0 / (1.0 + jnp.exp(-x))


def _rms_mod(x, g, shift, scale):
    return x * lax.rsqrt(jnp.mean(x * x, axis=-1, keepdims=True) + RMS_EPS) * (g * (1.0 + scale)) + shift


def _bdot(a, b):
    return jnp.dot(a, b, preferred_element_type=F32)


def _adaln_kernel(c_ref, w_ref, b_ref, o_ref):
    cc = c_ref[...]
    a = cc * _sigmoid(cc)
    w = w_ref[0]
    a_hi = a.astype(BF16)
    a_lo = (a - a_hi.astype(F32)).astype(BF16)
    w_hi = w.astype(BF16)
    w_lo = (w - w_hi.astype(F32)).astype(BF16)
    by_hi = _bdot(jnp.concatenate([a_hi, a_lo], axis=0), w_hi)
    o_ref[0] = by_hi[:MOD_ROWS] + by_hi[MOD_ROWS:] + _bdot(a_hi, w_lo) + b_ref[0]


def _adaln(cc, ada_w, ada_b):
    depth, d, n = ada_w.shape
    tn = TN_ADALN
    return pl.pallas_call(
        _adaln_kernel,
        grid=(depth, n // tn),
        in_specs=[pl.BlockSpec((MOD_ROWS, d), lambda l, j: (0, 0)),
                  pl.BlockSpec((1, d, tn), lambda l, j: (l, 0, j)),
                  pl.BlockSpec((1, 1, tn), lambda l, j: (l, 0, j))],
        out_specs=pl.BlockSpec((1, MOD_ROWS, tn), lambda l, j: (l, 0, j)),
        out_shape=jax.ShapeDtypeStruct((depth, MOD_ROWS, n), F32),
        compiler_params=_params("parallel", "parallel"),
        name="adaln",
    )(cc, ada_w, ada_b.reshape(depth, 1, n))


def _fold_kernel(cc_ref, sc_ref, w_ref, o_ref):
    w = w_ref[0]
    hi = lax.Precision.HIGHEST
    o_ref[0, :, :D_FOURIER] = jnp.dot(cc_ref[...], w, preferred_element_type=F32, precision=hi).astype(BF16)
    o_ref[0, :, D_FOURIER:] = jnp.dot(sc_ref[...], w, preferred_element_type=F32, precision=hi).astype(BF16)


def _fold_fourier(w_fourier, seq):
    depth = w_fourier.shape[0]
    gs = D_FOURIER // FOURIER_GROUPS
    idx = np.arange(D_FOURIER)
    same = (idx[:, None] // gs) == (idx[None, :] // gs)
    ang = 2.0 * np.pi * (((idx[:, None] % gs) * (idx[None, :] % gs)) % gs) / gs
    scale = 1.0 / np.sqrt(seq * gs)
    cc = jnp.asarray(np.where(same, np.cos(ang), 0.0) * scale, F32)
    sc = jnp.asarray(np.where(same, np.sin(ang), 0.0) * scale, F32)
    full = pl.BlockSpec((D_FOURIER, D_FOURIER), lambda l: (0, 0))
    return pl.pallas_call(
        _fold_kernel,
        grid=(depth,),
        in_specs=[full, full, pl.BlockSpec((1, D_FOURIER, D_FOURIER), lambda l: (l, 0, 0))],
        out_specs=pl.BlockSpec((1, D_FOURIER, 2 * D_FOURIER), lambda l: (l, 0, 0)),
        out_shape=jax.ShapeDtypeStruct((depth, D_FOURIER, 2 * D_FOURIER), BF16),
        compiler_params=_params("parallel"),
        name="fold_fourier",
    )(cc, sc, w_fourier)


FLIP_BLOCK = 128


def _dft_mats(seq):
    half = seq // 2
    k = np.arange(half)
    ang = 2.0 * np.pi * ((k[:, None] * k[None, :]) % seq) / seq
    r = np.arange(FLIP_BLOCK)
    perm = (np.arange(2 * FLIP_BLOCK)[None, :] == FLIP_BLOCK - r[:, None]).astype(np.float32)
    alt_col = np.where(k % 2 == 0, 1.0, -1.0).astype(np.float32)[:, None]
    alt_row = np.zeros((SUBLANES, seq), np.float32)
    alt_row[0] = np.where(np.arange(seq) % 2 == 0, 1.0, -1.0)
    return (jnp.asarray(np.cos(ang), F32).astype(BF16), jnp.asarray(np.sin(ang), F32).astype(BF16),
            jnp.asarray(perm, F32).astype(BF16), jnp.asarray(alt_col), jnp.asarray(alt_row, F32).astype(BF16))


def _store_kv(hn, w_ref, out_ref, rows, col):
    kv = _bdot(hn, w_ref[:, KV_START:])
    v = kv[:, D_NA:]
    first = lax.broadcasted_iota(jnp.int32, (1, D_NA), 1) % LANES < NA_HEAD_DIM
    out_ref[rows, col:col + D_NA] = kv[:, :D_NA].astype(BF16)
    out_ref[rows, col + D_NA:col + 2 * D_NA] = jnp.where(first, v, 1.0).astype(BF16)
    out_ref[rows, col + 2 * D_NA:col + 3 * D_NA] = jnp.where(first, 1.0, v).astype(BF16)


INPROJ_SPLIT_ROWS = 512


def _inproj_kernel(x_ref, g_ref, mod_ref, w_ref, ab_ref, zf_ref, uc_ref, qkv_ref):
    tm = x_ref.shape[0]
    part = min(tm, INPROJ_SPLIT_ROWS)
    for r in range(tm // part):
        rows = slice(r * part, (r + 1) * part)
        hn = _rms_mod(x_ref[rows, :], g_ref[...], mod_ref[0, 0:1, :], mod_ref[0, 1:2, :]).astype(BF16)
        uf = _bdot(hn, w_ref[:, :D_FOURIER])
        uc_ref[rows, :] = _bdot(hn, w_ref[:, D_FOURIER:QKV_START])
        qkv_ref[rows, :D_NA] = (_bdot(hn, w_ref[:, QKV_START:KV_START]) * (NA_HEAD_DIM ** -0.5)).astype(BF16)
        _store_kv(hn, w_ref, qkv_ref, rows, D_NA)
        zf_ref[rows, :] = _bdot(uf.astype(BF16), ab_ref[...]).astype(BF16)


def _inproj(h, g, mods, w_in, ab, *, layer, seq, tm, ctx):
    t, d = h.shape
    nb = t // seq
    per = seq // tm
    row = (lambda i: nb) if ctx else (lambda i: i // per)
    d_in = w_in.shape[2]
    return pl.pallas_call(
        _inproj_kernel,
        grid=(t // tm,),
        in_specs=[pl.BlockSpec((tm, d), lambda i: (i, 0)),
                  pl.BlockSpec((1, d), lambda i: (0, 0)),
                  pl.BlockSpec((1, N_MOD, d), lambda i: (row(i), 0, 0)),
                  pl.BlockSpec((None, d, d_in), lambda i: (layer, 0, 0), pipeline_mode=pl.Buffered(1)),
                  pl.BlockSpec((D_FOURIER, 2 * D_FOURIER), lambda i: (0, 0))],
        out_specs=[pl.BlockSpec((tm, 2 * D_FOURIER), lambda i: (i % per, i // per)),
                   pl.BlockSpec((tm, 2 * D_CONV), lambda i: (i, 0)),
                   pl.BlockSpec((tm, 4 * D_NA), lambda i: (i, 0))],
        out_shape=[jax.ShapeDtypeStruct((seq, nb * 2 * D_FOURIER), BF16),
                   jax.ShapeDtypeStruct((t, 2 * D_CONV), F32),
                   jax.ShapeDtypeStruct((t, 4 * D_NA), BF16)],
        compiler_params=_params("parallel"),
        name="inproj_ctx" if ctx else "inproj_lat",
    )(h, g, mods, w_in, ab)


def _kvproj_kernel(x_ref, g_ref, mod_ref, w_ref, kv_ref):
    hn = _rms_mod(x_ref[...], g_ref[...], mod_ref[0, 0:1, :], mod_ref[0, 1:2, :]).astype(BF16)
    _store_kv(hn, w_ref, kv_ref, slice(None), 0)


def _kvproj(h, g, mods, w_in, *, layer, nb, tm):
    t, d = h.shape
    d_in = w_in.shape[2]
    n = 3 * D_NA
    return pl.pallas_call(
        _kvproj_kernel,
        grid=(t // tm,),
        in_specs=[pl.BlockSpec((tm, d), lambda i: (i, 0)),
                  pl.BlockSpec((1, d), lambda i: (0, 0)),
                  pl.BlockSpec((1, N_MOD, d), lambda i: (nb, 0, 0)),
                  pl.BlockSpec((None, d, d_in), lambda i: (layer, 0, 0))],
        out_specs=pl.BlockSpec((tm, n), lambda i: (i, 0)),
        out_shape=jax.ShapeDtypeStruct((t, n), BF16),
        compiler_params=_params("parallel"),
        name="kvproj_ctx",
    )(h, g, mods, w_in)


def _fourier_kernel(c_ref, s_ref, perm_ref, altc_ref, altr_ref, z_ref, y_ref, ze_ref, yr_ref):
    seq = z_ref.shape[0]
    half = seq // 2
    nblk = half // FLIP_BLOCK
    fb = FLIP_BLOCK

    def reversed_block(ref, i, n):
        if i == 0:
            return _bdot(perm_ref[:, :fb], ref[n - fb:n, :])
        lo = n - (i + 1) * fb
        return _bdot(perm_ref[...], ref[lo:lo + 2 * fb, :])

    for i in range(nblk):
        rows_i = slice(i * fb, (i + 1) * fb)
        zr = reversed_block(z_ref, i, seq)
        ze_ref[rows_i, :D_FOURIER] = (z_ref[rows_i, :D_FOURIER].astype(F32) + zr[:, :D_FOURIER]).astype(BF16)
        ze_ref[rows_i, D_FOURIER:] = (z_ref[rows_i, D_FOURIER:].astype(F32) - zr[:, D_FOURIER:]).astype(BF16)
    nyq = z_ref[half:half + 1, :D_FOURIER].astype(F32)
    yc = _bdot(c_ref[...], ze_ref[:, :D_FOURIER]) + altc_ref[...] * nyq
    ys = _bdot(s_ref[...], ze_ref[:, D_FOURIER:])
    y_ref[:half, :] = (yc - ys).astype(BF16)
    yr_ref[...] = (yc + ys).astype(BF16)
    y_mid = _bdot(altr_ref[...], z_ref[:, :D_FOURIER])[0:1, :]
    first_row = lax.broadcasted_iota(jnp.int32, (fb, 1), 0) == 0
    for i in range(nblk):
        blk = reversed_block(yr_ref, i, half)
        if i == 0:
            blk = jnp.where(first_row, y_mid, blk)
        y_ref[half + i * fb:half + (i + 1) * fb, :] = blk.astype(BF16)


def _fourier(mats, zf):
    cmat, smat, perm, alt_col, alt_row = mats
    seq = alt_row.shape[1]
    half = seq // 2
    nb = zf.shape[1] // (2 * D_FOURIER)
    const = lambda a: pl.BlockSpec(a.shape, lambda b: (0, 0))
    return pl.pallas_call(
        _fourier_kernel,
        grid=(nb,),
        in_specs=[const(cmat), const(smat), const(perm), const(alt_col), const(alt_row),
                  pl.BlockSpec((seq, 2 * D_FOURIER), lambda b: (0, b))],
        out_specs=pl.BlockSpec((seq, D_FOURIER), lambda b: (0, b)),
        out_shape=jax.ShapeDtypeStruct((seq, nb * D_FOURIER), BF16),
        scratch_shapes=[pltpu.VMEM((half, 2 * D_FOURIER), BF16), pltpu.VMEM((half, D_FOURIER), BF16)],
        compiler_params=_params("parallel"),
        name="fourier_pos",
    )(cmat, smat, perm, alt_col, alt_row, zf)


CONV_PAD = 16
CONV_CHUNK = 512


def _conv_kernel(u_ref, dww_ref, dwb_ref, lng_ref, lnb_ref, pww_ref, pwb_ref, o_ref, vs_ref):
    seq = u_ref.shape[0]
    n_pad = seq + 2 * CONV_PAD
    zeros = jnp.zeros((CONV_PAD, D_CONV), F32)
    vs_ref[0, 0:CONV_PAD, :] = zeros
    vs_ref[0, seq + CONV_PAD:n_pad, :] = zeros
    vs_ref[0, CONV_PAD:CONV_PAD + seq, :] = u_ref[:, :D_CONV] * _sigmoid(u_ref[:, D_CONV:])
    for s in range(1, SUBLANES):
        vs_ref[s, 0:n_pad - SUBLANES, :] = vs_ref[0, s:s + n_pad - SUBLANES, :]
    first = CONV_PAD - CONV_WIDTH // 2
    assert (first + CONV_WIDTH - 1) // SUBLANES * SUBLANES + seq <= n_pad - SUBLANES

    rows = min(CONV_CHUNK, seq)

    def chunk(ci, carry):
        base = pl.multiple_of(ci * rows, rows)
        acc = jnp.zeros((rows, D_CONV), F32) + dwb_ref[...]
        for t in range(CONV_WIDTH):
            s, a = (first + t) % SUBLANES, (first + t) // SUBLANES
            acc = acc + vs_ref[s, pl.ds(base + a * SUBLANES, rows), :] * dww_ref[t:t + 1, :]
        mu = jnp.mean(acc, axis=-1, keepdims=True)
        cen = acc - mu
        var = jnp.mean(cen * cen, axis=-1, keepdims=True)
        y = cen * lax.rsqrt(var + LN_EPS) * lng_ref[...] + lnb_ref[...]
        y = y * _sigmoid(y)
        o_ref[pl.ds(base, rows), :] = (_bdot(y.astype(BF16), pww_ref[...]) + pwb_ref[...]).astype(BF16)
        return carry

    lax.fori_loop(0, seq // rows, chunk, 0)


def _conv(uc, dww, dwb, lng, lnb, pww, pwb, *, seq):
    t = uc.shape[0]
    vec = pl.BlockSpec((1, D_CONV), lambda b: (0, 0))
    return pl.pallas_call(
        _conv_kernel,
        grid=(t // seq,),
        in_specs=[pl.BlockSpec((seq, 2 * D_CONV), lambda b: (b, 0)),
                  pl.BlockSpec((CONV_WIDTH, D_CONV), lambda b: (0, 0)),
                  vec, vec, vec,
                  pl.BlockSpec((D_CONV, D_CONV), lambda b: (0, 0)),
                  vec],
        out_specs=pl.BlockSpec((seq, D_CONV), lambda b: (b, 0)),
        out_shape=jax.ShapeDtypeStruct((t, D_CONV), BF16),
        scratch_shapes=[pltpu.VMEM((SUBLANES, seq + 2 * CONV_PAD, D_CONV), F32)],
        compiler_params=_params("parallel"),
        name="conv_module",
    )(uc, dww, dwb, lng, lnb, pww, pwb)


def _head_mask():
    return lax.broadcasted_iota(jnp.int32, (1, LANES), 1) < NA_HEAD_DIM


Q_ROWS = 2
BAND_ROWS = 10
assert BAND_ROWS >= WIN_ROWS + Q_ROWS - 1 and (BAND_ROWS * GRID_W) % LANES == 0


def _na_patterns(rows):
    starts, sigs = [], []
    for g in range(rows // Q_ROWS):
        start = int(np.clip(Q_ROWS * g - WIN_ROWS // 2, 0, rows - BAND_ROWS))
        r = Q_ROWS * g + np.arange(Q_ROWS)
        rs = np.clip(r - WIN_ROWS // 2, 0, rows - WIN_ROWS)
        starts.append(start)
        sigs.append((start - Q_ROWS * g,) + tuple(rs - r))
    run_starts = [g for g in range(len(sigs)) if g == 0 or sigs[g] != sigs[g - 1]]
    assert len(set(sigs)) == len(run_starts)
    return starts, run_starts


def _na_kernel(q_ref, k_ref, va_ref, vb_ref, kc_ref, vca_ref, vcb_ref, rpb_ref, o_ref, s_ref, p_ref, bias_ref, *,
               seq, n_ctx):
    rows = seq // GRID_W
    per_batch = rows // Q_ROWS
    n_groups = q_ref.shape[0] // seq * per_batch
    n_q = Q_ROWS * GRID_W
    n_loc = BAND_ROWS * GRID_W
    _, run_starts = _na_patterns(rows)
    first = _head_mask()
    dn = (((1,), (1,)), ((), ()))
    assert n_groups >= 4 and seq == per_batch * n_q

    def split(t):
        t = jnp.asarray(t, jnp.int32)
        return t // per_batch, t % per_batch

    def band(t):
        b, g = split(t)
        start = jnp.clip(Q_ROWS * g - WIN_ROWS // 2, 0, rows - BAND_ROWS)
        return pl.multiple_of(b * seq + start * GRID_W, LANES)

    def q_start(t):
        return pl.multiple_of(jnp.asarray(t, jnp.int32) * n_q, n_q)

    def ctx_start(t):
        return pl.multiple_of(split(t)[0] * n_ctx, n_ctx)

    def scores(t, slot):
        g = split(t)[1]
        pat = sum((g >= s).astype(jnp.int32) for s in run_starts[1:])
        q = q_ref[pl.ds(q_start(t), n_q), :]
        kb = k_ref[pl.ds(band(t), n_loc), :]
        kc = kc_ref[pl.ds(ctx_start(t), n_ctx), :]
        for a in range(2):
            qa = jnp.where(first if a == 0 else ~first, q, jnp.zeros_like(q))
            s_ref[slot, a, :, :n_loc] = lax.dot_general(qa, kb, dn, preferred_element_type=F32) + bias_ref[a, pat]
            s_ref[slot, a, :, n_loc:] = lax.dot_general(qa, kc, dn, preferred_element_type=F32)

    def softmax(slot):
        for a in range(2):
            s = s_ref[slot, a]
            p_ref[slot, a] = jnp.exp((s - jnp.max(s, axis=-1, keepdims=True)).astype(BF16))

    def values(t, slot):
        outs = []
        for a, (v_ref, vc_ref) in enumerate(((va_ref, vca_ref), (vb_ref, vcb_ref))):
            o = (_bdot(p_ref[slot, a, :, :n_loc], v_ref[pl.ds(band(t), n_loc), :])
                 + _bdot(p_ref[slot, a, :, n_loc:], vc_ref[pl.ds(ctx_start(t), n_ctx), :]))
            outs.append(o * (1.0 / pltpu.roll(o, NA_HEAD_DIM, axis=1)))
        o_ref[pl.ds(q_start(t), n_q), :] = jnp.where(first, outs[0], outs[1]).astype(BF16)

    def tick(t, parity):
        values(t - 1, 1 - parity)
        scores(t + 1, 1 - parity)
        softmax(parity)

    def quad(j, carry):
        t = 4 * j + 1
        tick(t, 1)
        tick(t + 1, 0)
        tick(t + 2, 1)
        tick(t + 3, 0)
        return carry

    @pl.when(pl.program_id(1) == 0)
    def _():
        _na_assemble_bias(rpb_ref, bias_ref, rows)

    scores(0, 0)
    scores(1, 1)
    softmax(0)
    n_quads = (n_groups - 2) // 4
    lax.fori_loop(0, n_quads, quad, 0)
    for t in range(4 * n_quads + 1, n_groups - 1):
        tick(t, t % 2)
    softmax((n_groups - 1) % 2)
    values(n_groups - 2, n_groups % 2)
    values(n_groups - 1, (n_groups - 1) % 2)


def _na_bias(rpb):
    n_heads, n_dr, n_dc = rpb.shape
    r = rpb.astype(F32)
    w = jnp.concatenate([r[..., WIN_COLS - 1:], jnp.zeros((n_heads, n_dr, LANES - n_dc), F32), r[..., :WIN_COLS - 1]],
                        axis=-1)
    w = jnp.pad(w, ((0, 0), (0, 1), (0, 0)))
    return w.reshape(HEAD_PAIRS, 2, n_dr + 1, LANES)


def _na_assemble_bias(w_ref, bias_ref, rows):
    starts, run_starts = _na_patterns(rows)
    cq = lax.broadcasted_iota(jnp.int32, (GRID_W, LANES), 0)
    lane = lax.broadcasted_iota(jnp.int32, (GRID_W, LANES), 1)
    low = lane < GRID_W
    ck = jnp.where(low, lane, lane - GRID_W)
    col_start = jnp.clip(cq - WIN_COLS // 2, 0, GRID_W - WIN_COLS)
    in_window = (ck >= col_start) & (ck < col_start + WIN_COLS)
    masked = jnp.full((GRID_W, LANES), -jnp.inf, F32)
    rolled = {}

    def half(a, d, upper):
        if (a, d, upper) not in rolled:
            row = jnp.broadcast_to(w_ref[0, a, d:d + 1, :], (GRID_W, LANES))
            rolled[a, d, upper] = pltpu.roll(row, GRID_W if upper else 0, axis=1, stride=1, stride_axis=0)
        return rolled[a, d, upper]

    for a in range(2):
        for p, g in enumerate(run_starts):
            for qi in range(Q_ROWS):
                r = Q_ROWS * g + qi
                rs = int(np.clip(r - WIN_ROWS // 2, 0, rows - WIN_ROWS))
                below = rs - starts[g]
                dr0 = rs - r + WIN_ROWS - 1
                for t in range(BAND_ROWS // 2):
                    d0 = dr0 + 2 * t - below
                    v0 = below <= 2 * t < below + WIN_ROWS
                    v1 = below <= 2 * t + 1 < below + WIN_ROWS
                    if v0 and v1:
                        tile = jnp.where(low, half(a, d0, False), half(a, d0 + 1, True))
                    elif v0:
                        tile = jnp.where(low, half(a, d0, False), masked)
                    elif v1:
                        tile = jnp.where(low, masked, half(a, d0 + 1, True))
                    else:
                        tile = masked
                    if v0 or v1:
                        tile = jnp.where(in_window, tile, masked)
                    bias_ref[a, p, qi * GRID_W:(qi + 1) * GRID_W, t * LANES:(t + 1) * LANES] = tile


NA_BATCHES = 4


def _na(qkv, kv_ctx, rpb_rows, *, seq, n_ctx, kc_off):
    t = qkv.shape[0]
    hp = HEAD_PAIRS
    tb, tbc = NA_BATCHES * seq, NA_BATCHES * n_ctx
    n_pat = len(_na_patterns(seq // GRID_W)[1])
    n_keys = BAND_ROWS * GRID_W + n_ctx
    assert t % tb == 0
    return pl.pallas_call(
        functools.partial(_na_kernel, seq=seq, n_ctx=n_ctx),
        grid=(hp, t // tb),
        in_specs=[pl.BlockSpec((tb, LANES), lambda h, b: (b, h)),
                  pl.BlockSpec((tb, LANES), lambda h, b: (b, hp + h)),
                  pl.BlockSpec((tb, LANES), lambda h, b: (b, 2 * hp + h)),
                  pl.BlockSpec((tb, LANES), lambda h, b: (b, 3 * hp + h)),
                  pl.BlockSpec((tbc, LANES), lambda h, b: (b, kc_off + h)),
                  pl.BlockSpec((tbc, LANES), lambda h, b: (b, kc_off + hp + h)),
                  pl.BlockSpec((tbc, LANES), lambda h, b: (b, kc_off + 2 * hp + h)),
                  pl.BlockSpec((1,) + rpb_rows.shape[1:], lambda h, b: (h, 0, 0, 0))],
        out_specs=pl.BlockSpec((tb, LANES), lambda h, b: (b, h)),
        out_shape=jax.ShapeDtypeStruct((t, D_NA), BF16),
        scratch_shapes=[pltpu.VMEM((2, 2, Q_ROWS * GRID_W, n_keys), F32),
                        pltpu.VMEM((2, 2, Q_ROWS * GRID_W, n_keys), BF16),
                        pltpu.VMEM((2, n_pat, Q_ROWS * GRID_W, BAND_ROWS * GRID_W), F32)],
        compiler_params=_params("parallel", "arbitrary"),
        name="na_attention",
    )(qkv, qkv, qkv, qkv, kv_ctx, kv_ctx, kv_ctx, rpb_rows)


def _ctx_attn_kernel(q_ref, k_ref, va_ref, vb_ref, o_ref):
    first = _head_mask()
    for h in range(HEAD_PAIRS):
        lanes = slice(h * LANES, (h + 1) * LANES)
        q = q_ref[:, lanes]
        k = k_ref[:, lanes]
        outs = []
        for a, v_ref in enumerate((va_ref, vb_ref)):
            qa = jnp.where(first if a == 0 else ~first, q, jnp.zeros_like(q))
            s = lax.dot_general(qa, k, (((1,), (1,)), ((), ())), preferred_element_type=F32)
            p = jnp.exp(s - jnp.max(s, axis=-1, keepdims=True))
            den = jnp.sum(p, axis=-1, keepdims=True)
            outs.append(_bdot(p.astype(BF16), v_ref[:, lanes]) * (1.0 / den))
        o_ref[:, lanes] = jnp.where(first, outs[0], outs[1]).astype(BF16)


def _ctx_attn(qkv, *, n_ctx):
    t = qkv.shape[0]
    return pl.pallas_call(
        _ctx_attn_kernel,
        grid=(t // n_ctx,),
        in_specs=[pl.BlockSpec((n_ctx, D_NA), lambda b: (b, 0)),
                  pl.BlockSpec((n_ctx, D_NA), lambda b: (b, 1)),
                  pl.BlockSpec((n_ctx, D_NA), lambda b: (b, 2)),
                  pl.BlockSpec((n_ctx, D_NA), lambda b: (b, 3))],
        out_specs=pl.BlockSpec((n_ctx, D_NA), lambda b: (b, 0)),
        out_shape=jax.ShapeDtypeStruct((t, D_NA), BF16),
        compiler_params=_params("parallel"),
        name="ctx_attention",
    )(qkv, qkv, qkv, qkv)


def _outmlp_kernel(h_ref, yf_ref, cv_ref, at_ref, mod_ref, g_ref, gf_ref, wo_ref, w1_ref, w2_ref, o_ref, *, final_norm):
    mix = (_bdot(yf_ref[...], wo_ref[:D_FOURIER, :])
           + _bdot(cv_ref[...], wo_ref[D_FOURIER:D_FOURIER + D_CONV, :])
           + _bdot(at_ref[...], wo_ref[D_FOURIER + D_CONV:, :]))
    h1 = h_ref[...] + mod_ref[0, 2:3, :] * mix
    hn = _rms_mod(h1, g_ref[...], mod_ref[0, 3:4, :], mod_ref[0, 4:5, :]).astype(BF16)
    a = jnp.maximum(_bdot(hn, w1_ref[...]), 0.0)
    out = h1 + mod_ref[0, 5:6, :] * _bdot((a * a).astype(BF16), w2_ref[...])
    if final_norm:
        out = out * lax.rsqrt(jnp.mean(out * out, axis=-1, keepdims=True) + RMS_EPS) * gf_ref[...]
    o_ref[...] = out


def _outmlp(h, yf, cv, at, mods, g2, gf, wo, w1, w2, *, layer, seq, tm, ctx, final_norm):
    t, d = h.shape
    nb = t // seq
    per = seq // tm
    row = (lambda i: nb) if ctx else (lambda i: i // per)
    d_ff = w1.shape[2]
    vec = pl.BlockSpec((1, d), lambda i: (0, 0))
    once = dict(pipeline_mode=pl.Buffered(1))
    return pl.pallas_call(
        functools.partial(_outmlp_kernel, final_norm=final_norm),
        grid=(t // tm,),
        in_specs=[pl.BlockSpec((tm, d), lambda i: (i, 0)),
                  pl.BlockSpec((tm, D_FOURIER), lambda i: (i % per, i // per)),
                  pl.BlockSpec((tm, D_CONV), lambda i: (i, 0)),
                  pl.BlockSpec((tm, D_NA), lambda i: (i, 0)),
                  pl.BlockSpec((1, N_MOD, d), lambda i: (row(i), 0, 0)),
                  vec, vec,
                  pl.BlockSpec((None, D_MIX, d), lambda i: (layer, 0, 0), **once),
                  pl.BlockSpec((None, d, d_ff), lambda i: (layer, 0, 0), **once),
                  pl.BlockSpec((None, d_ff, d), lambda i: (layer, 0, 0), **once)],
        out_specs=pl.BlockSpec((tm, d), lambda i: (i, 0)),
        out_shape=jax.ShapeDtypeStruct((t, d), F32),
        compiler_params=_params("parallel"),
        name="outmlp_ctx" if ctx else "outmlp_lat",
    )(h, yf, cv, at, mods, g2, gf, wo, w1, w2)


def kernel(x, c, ctx, c_ctx, ada_w, ada_b, norm1_g, norm2_g, w_in, w_fourier, conv_dw_w, conv_dw_b, conv_norm_g,
           conv_norm_b, conv_pw_w, conv_pw_b, na_rpb, w_out, mlp_w1, mlp_w2, final_norm_g):
    nb, seq, d = x.shape
    n_ctx = ctx.shape[1]
    depth = ada_w.shape[0]
    assert nb < MOD_ROWS and seq % GRID_W == 0 and d == D_MODEL

    cc = jnp.concatenate([c, c_ctx[None], jnp.zeros((MOD_ROWS - nb - 1, d), F32)], axis=0)
    mods = _adaln(cc, ada_w, ada_b).reshape(depth, MOD_ROWS, N_MOD, d)
    ab_lat = _fold_fourier(w_fourier, seq)
    ab_ctx = _fold_fourier(w_fourier, n_ctx)
    dft_lat = _dft_mats(seq)
    dft_ctx = _dft_mats(n_ctx)

    w_in_b = w_in.astype(BF16)
    w_out_b = w_out.astype(BF16)
    w1_b = mlp_w1.astype(BF16)
    w2_b = mlp_w2.astype(BF16)
    pww_b = conv_pw_w.astype(BF16)
    gf = final_norm_g.reshape(1, d)

    h_lat = x.reshape(nb * seq, d)
    h_ctx = ctx.reshape(nb * n_ctx, d)
    for i in range(depth):
        last = i == depth - 1
        g1 = norm1_g[i].reshape(1, d)
        g2 = norm2_g[i].reshape(1, d)
        conv_p = (conv_dw_w[i], conv_dw_b[i].reshape(1, -1), conv_norm_g[i].reshape(1, -1),
                  conv_norm_b[i].reshape(1, -1), pww_b[i], conv_pw_b[i].reshape(1, -1))
        bias = _na_bias(na_rpb[i])

        if last:
            kv_ctx = _kvproj(h_ctx, g1, mods[i], w_in_b, layer=i, nb=nb, tm=n_ctx)
            kc_off = 0
        else:
            zf_c, uc_c, qkv_c = _inproj(h_ctx, g1, mods[i], w_in_b, ab_ctx[i], layer=i, seq=n_ctx, tm=n_ctx, ctx=True)
            kv_ctx = qkv_c
            kc_off = HEAD_PAIRS

        zf, uc, qkv = _inproj(h_lat, g1, mods[i], w_in_b, ab_lat[i], layer=i, seq=seq, tm=TM_INPROJ, ctx=False)
        yf = _fourier(dft_lat, zf)
        cv = _conv(uc, *conv_p, seq=seq)
        at = _na(qkv, kv_ctx, bias, seq=seq, n_ctx=n_ctx, kc_off=kc_off)
        h_lat = _outmlp(h_lat, yf, cv, at, mods[i], g2, gf, w_out_b, w1_b, w2_b,
                        layer=i, seq=seq, tm=TM_MLP, ctx=False, final_norm=last)

        if not last:
            yf_c = _fourier(dft_ctx, zf_c)
            cv_c = _conv(uc_c, *conv_p, seq=n_ctx)
            at_c = _ctx_attn(qkv_c, n_ctx=n_ctx)
            h_ctx = _outmlp(h_ctx, yf_c, cv_c, at_c, mods[i], g2, gf, w_out_b, w1_b, w2_b,
                            layer=i, seq=n_ctx, tm=n_ctx, ctx=True, final_norm=False)
    return h_lat.reshape(nb, seq, d)
```

```python
import functools

import numpy as np
import jax
import jax.numpy as jnp
from jax import lax
from jax.experimental import pallas as pl
from jax.experimental.pallas import tpu as pltpu

D_MODEL = 1024
GRID_W = 64
D_FOURIER = 256
FOURIER_GROUPS = 4
D_CONV = 256
CONV_WIDTH = 31
N_NA_HEADS = 8
NA_HEAD_DIM = 64
D_NA = N_NA_HEADS * NA_HEAD_DIM
WIN_ROWS = 8
WIN_COLS = 16
QKV_START = D_FOURIER + 2 * D_CONV
KV_START = QKV_START + D_NA
D_MIX = D_FOURIER + D_CONV + D_NA
N_MOD = 6
RMS_EPS = 1e-6
LN_EPS = 1e-5

LANES = 128
SUBLANES = 8
HEAD_PAIRS = N_NA_HEADS * NA_HEAD_DIM // LANES
MOD_ROWS = 16
VMEM_LIMIT = 56 * 1024 * 1024

TM_INPROJ = 1024
TM_MLP = 512
TN_ADALN = 2048

F32 = jnp.float32
BF16 = jnp.bfloat16


def _params(*sem):
    return pltpu.CompilerParams(dimension_semantics=sem, vmem_limit_bytes=VMEM_LIMIT)


def _sigmoid(x):
    return 1.0 / (1.0 + jnp.exp(-x))


def _rms_mod(x, g, shift, scale):
    return x * lax.rsqrt(jnp.mean(x * x, axis=-1, keepdims=True) + RMS_EPS) * (g * (1.0 + scale)) + shift


def _bdot(a, b):
    return jnp.dot(a, b, preferred_element_type=F32)


def _adaln_kernel(c_ref, w_ref, b_ref, o_ref):
    cc = c_ref[...]
    a = cc * _sigmoid(cc)
    w = w_ref[0]
    a_hi = a.astype(BF16)
    a_lo = (a - a_hi.astype(F32)).astype(BF16)
    w_hi = w.astype(BF16)
    w_lo = (w - w_hi.astype(F32)).astype(BF16)
    by_hi = _bdot(jnp.concatenate([a_hi, a_lo], axis=0), w_hi)
    o_ref[0] = by_hi[:MOD_ROWS] + by_hi[MOD_ROWS:] + _bdot(a_hi, w_lo) + b_ref[0]


def _adaln(cc, ada_w, ada_b):
    depth, d, n = ada_w.shape
    tn = TN_ADALN
    return pl.pallas_call(
        _adaln_kernel,
        grid=(depth, n // tn),
        in_specs=[pl.BlockSpec((MOD_ROWS, d), lambda l, j: (0, 0)),
                  pl.BlockSpec((1, d, tn), lambda l, j: (l, 0, j)),
                  pl.BlockSpec((1, 1, tn), lambda l, j: (l, 0, j))],
        out_specs=pl.BlockSpec((1, MOD_ROWS, tn), lambda l, j: (l, 0, j)),
        out_shape=jax.ShapeDtypeStruct((depth, MOD_ROWS, n), F32),
        compiler_params=_params("parallel", "parallel"),
        name="adaln",
    )(cc, ada_w, ada_b.reshape(depth, 1, n))


def _fold_kernel(cc_ref, sc_ref, w_ref, o_ref):
    w = w_ref[0]
    hi = lax.Precision.HIGHEST
    o_ref[0, :, :D_FOURIER] = jnp.dot(cc_ref[...], w, preferred_element_type=F32, precision=hi).astype(BF16)
    o_ref[0, :, D_FOURIER:] = jnp.dot(sc_ref[...], w, preferred_element_type=F32, precision=hi).astype(BF16)


def _fold_fourier(w_fourier, seq):
    depth = w_fourier.shape[0]
    gs = D_FOURIER // FOURIER_GROUPS
    idx = np.arange(D_FOURIER)
    same = (idx[:, None] // gs) == (idx[None, :] // gs)
    ang = 2.0 * np.pi * (((idx[:, None] % gs) * (idx[None, :] % gs)) % gs) / gs
    scale = 1.0 / np.sqrt(seq * gs)
    cc = jnp.asarray(np.where(same, np.cos(ang), 0.0) * scale, F32)
    sc = jnp.asarray(np.where(same, np.sin(ang), 0.0) * scale, F32)
    full = pl.BlockSpec((D_FOURIER, D_FOURIER), lambda l: (0, 0))
    return pl.pallas_call(
        _fold_kernel,
        grid=(depth,),
        in_specs=[full, full, pl.BlockSpec((1, D_FOURIER, D_FOURIER), lambda l: (l, 0, 0))],
        out_specs=pl.BlockSpec((1, D_FOURIER, 2 * D_FOURIER), lambda l: (l, 0, 0)),
        out_shape=jax.ShapeDtypeStruct((depth, D_FOURIER, 2 * D_FOURIER), BF16),
        compiler_params=_params("parallel"),
        name="fold_fourier",
    )(cc, sc, w_fourier)


FLIP_BLOCK = 128


def _dft_mats(seq):
    half = seq // 2
    k = np.arange(half)
    ang = 2.0 * np.pi * ((k[:, None] * k[None, :]) % seq) / seq
    r = np.arange(FLIP_BLOCK)
    perm = (np.arange(2 * FLIP_BLOCK)[None, :] == FLIP_BLOCK - r[:, None]).astype(np.float32)
    alt_col = np.where(k % 2 == 0, 1.0, -1.0).astype(np.float32)[:, None]
    alt_row = np.zeros((SUBLANES, seq), np.float32)
    alt_row[0] = np.where(np.arange(seq) % 2 == 0, 1.0, -1.0)
    return (jnp.asarray(np.cos(ang), F32).astype(BF16), jnp.asarray(np.sin(ang), F32).astype(BF16),
            jnp.asarray(perm, F32).astype(BF16), jnp.asarray(alt_col), jnp.asarray(alt_row, F32).astype(BF16))


def _store_kv(hn, w_ref, out_ref, rows, col):
    kv = _bdot(hn, w_ref[:, KV_START:])
    v = kv[:, D_NA:]
    first = lax.broadcasted_iota(jnp.int32, (1, D_NA), 1) % LANES < NA_HEAD_DIM
    out_ref[rows, col:col + D_NA] = kv[:, :D_NA].astype(BF16)
    out_ref[rows, col + D_NA:col + 2 * D_NA] = jnp.where(first, v, 1.0).astype(BF16)
    out_ref[rows, col + 2 * D_NA:col + 3 * D_NA] = jnp.where(first, 1.0, v).astype(BF16)


INPROJ_SPLIT_ROWS = 512


def _inproj_kernel(x_ref, g_ref, mod_ref, w_ref, ab_ref, zf_ref, uc_ref, qkv_ref):
    tm = x_ref.shape[0]
    part = min(tm, INPROJ_SPLIT_ROWS)
    for r in range(tm // part):
        rows = slice(r * part, (r + 1) * part)
        hn = _rms_mod(x_ref[rows, :], g_ref[...], mod_ref[0, 0:1, :], mod_ref[0, 1:2, :]).astype(BF16)
        uf = _bdot(hn, w_ref[:, :D_FOURIER])
        uc_ref[rows, :] = _bdot(hn, w_ref[:, D_FOURIER:QKV_START])
        qkv_ref[rows, :D_NA] = (_bdot(hn, w_ref[:, QKV_START:KV_START]) * (NA_HEAD_DIM ** -0.5)).astype(BF16)
        _store_kv(hn, w_ref, qkv_ref, rows, D_NA)
        zf_ref[rows, :] = _bdot(uf.astype(BF16), ab_ref[...]).astype(BF16)


def _inproj(h, g, mods, w_in, ab, *, layer, seq, tm, ctx):
    t, d = h.shape
    nb = t // seq
    per = seq // tm
    row = (lambda i: nb) if ctx else (lambda i: i // per)
    d_in = w_in.shape[2]
    return pl.pallas_call(
        _inproj_kernel,
        grid=(t // tm,),
        in_specs=[pl.BlockSpec((tm, d), lambda i: (i, 0)),
                  pl.BlockSpec((1, d), lambda i: (0, 0)),
                  pl.BlockSpec((1, N_MOD, d), lambda i: (row(i), 0, 0)),
                  pl.BlockSpec((None, d, d_in), lambda i: (layer, 0, 0), pipeline_mode=pl.Buffered(1)),
                  pl.BlockSpec((D_FOURIER, 2 * D_FOURIER), lambda i: (0, 0))],
        out_specs=[pl.BlockSpec((tm, 2 * D_FOURIER), lambda i: (i % per, i // per)),
                   pl.BlockSpec((tm, 2 * D_CONV), lambda i: (i, 0)),
                   pl.BlockSpec((tm, 4 * D_NA), lambda i: (i, 0))],
        out_shape=[jax.ShapeDtypeStruct((seq, nb * 2 * D_FOURIER), BF16),
                   jax.ShapeDtypeStruct((t, 2 * D_CONV), F32),
                   jax.ShapeDtypeStruct((t, 4 * D_NA), BF16)],
        compiler_params=_params("parallel"),
        name="inproj_ctx" if ctx else "inproj_lat",
    )(h, g, mods, w_in, ab)


def _kvproj_kernel(x_ref, g_ref, mod_ref, w_ref, kv_ref):
    hn = _rms_mod(x_ref[...], g_ref[...], mod_ref[0, 0:1, :], mod_ref[0, 1:2, :]).astype(BF16)
    _store_kv(hn, w_ref, kv_ref, slice(None), 0)


def _kvproj(h, g, mods, w_in, *, layer, nb, tm):
    t, d = h.shape
    d_in = w_in.shape[2]
    n = 3 * D_NA
    return pl.pallas_call(
        _kvproj_kernel,
        grid=(t // tm,),
        in_specs=[pl.BlockSpec((tm, d), lambda i: (i, 0)),
                  pl.BlockSpec((1, d), lambda i: (0, 0)),
                  pl.BlockSpec((1, N_MOD, d), lambda i: (nb, 0, 0)),
                  pl.BlockSpec((None, d, d_in), lambda i: (layer, 0, 0))],
        out_specs=pl.BlockSpec((tm, n), lambda i: (i, 0)),
        out_shape=jax.ShapeDtypeStruct((t, n), BF16),
        compiler_params=_params("parallel"),
        name="kvproj_ctx",
    )(h, g, mods, w_in)


CONV_PAD = 16
CONV_CHUNK = 512


def _convfourier_kernel(u_ref, dww_ref, dwb_ref, lng_ref, lnb_ref, pww_ref, pwb_ref,
                        c_ref, s_ref, perm_ref, altc_ref, altr_ref, z_ref, o_ref, y_ref, vs_ref, ze_ref, yr_ref):
    seq = u_ref.shape[0]
    n_pad = seq + 2 * CONV_PAD
    half = seq // 2
    fb = FLIP_BLOCK
    nblk = half // fb
    zeros = jnp.zeros((CONV_PAD, D_CONV), F32)
    vs_ref[0, 0:CONV_PAD, :] = zeros
    vs_ref[0, seq + CONV_PAD:n_pad, :] = zeros
    vs_ref[0, CONV_PAD:CONV_PAD + seq, :] = u_ref[:, :D_CONV] * _sigmoid(u_ref[:, D_CONV:])
    for s in range(1, SUBLANES):
        vs_ref[s, 0:n_pad - SUBLANES, :] = vs_ref[0, s:s + n_pad - SUBLANES, :]
    first = CONV_PAD - CONV_WIDTH // 2
    assert (first + CONV_WIDTH - 1) // SUBLANES * SUBLANES + seq <= n_pad - SUBLANES
    rows = min(CONV_CHUNK, seq)
    n_chunks = seq // rows

    def conv_chunk(ci):
        base = ci * rows
        acc = jnp.zeros((rows, D_CONV), F32) + dwb_ref[...]
        for t in range(CONV_WIDTH):
            s, a = (first + t) % SUBLANES, (first + t) // SUBLANES
            acc = acc + vs_ref[s, base + a * SUBLANES:base + a * SUBLANES + rows, :] * dww_ref[t:t + 1, :]
        mu = jnp.mean(acc, axis=-1, keepdims=True)
        cen = acc - mu
        var = jnp.mean(cen * cen, axis=-1, keepdims=True)
        y = cen * lax.rsqrt(var + LN_EPS) * lng_ref[...] + lnb_ref[...]
        y = y * _sigmoid(y)
        o_ref[base:base + rows, :] = (_bdot(y.astype(BF16), pww_ref[...]) + pwb_ref[...]).astype(BF16)

    def reversed_block(ref, i, n):
        if i == 0:
            return _bdot(perm_ref[:, :fb], ref[n - fb:n, :])
        lo = n - (i + 1) * fb
        return _bdot(perm_ref[...], ref[lo:lo + 2 * fb, :])

    def fold_in():
        for i in range(nblk):
            rows_i = slice(i * fb, (i + 1) * fb)
            zr = reversed_block(z_ref, i, seq)
            ze_ref[rows_i, :D_FOURIER] = (z_ref[rows_i, :D_FOURIER].astype(F32) + zr[:, :D_FOURIER]).astype(BF16)
            ze_ref[rows_i, D_FOURIER:] = (z_ref[rows_i, D_FOURIER:].astype(F32) - zr[:, D_FOURIER:]).astype(BF16)

    def dft(part, n_parts):
        r = slice(part * half // n_parts, (part + 1) * half // n_parts)
        nyq = z_ref[half:half + 1, :D_FOURIER].astype(F32)
        yc = _bdot(c_ref[r, :], ze_ref[:, :D_FOURIER]) + altc_ref[r, :] * nyq
        ys = _bdot(s_ref[r, :], ze_ref[:, D_FOURIER:])
        y_ref[r, :] = (yc - ys).astype(BF16)
        yr_ref[r, :] = (yc + ys).astype(BF16)

    def fold_out():
        y_mid = _bdot(altr_ref[...], z_ref[:, :D_FOURIER])[0:1, :]
        first_row = lax.broadcasted_iota(jnp.int32, (fb, 1), 0) == 0
        for i in range(nblk):
            blk = reversed_block(yr_ref, i, half)
            if i == 0:
                blk = jnp.where(first_row, y_mid, blk)
            y_ref[half + i * fb:half + (i + 1) * fb, :] = blk.astype(BF16)

    phases = [fold_in, functools.partial(dft, 0, 2), functools.partial(dft, 1, 2), fold_out]
    for ci in range(n_chunks):
        for phase in phases[ci * len(phases) // n_chunks:(ci + 1) * len(phases) // n_chunks]:
            phase()
        conv_chunk(ci)


def _convfourier(uc, conv_p, mats, zf, *, seq):
    t = uc.shape[0]
    cmat, smat, perm, alt_col, alt_row = mats
    half = seq // 2
    vec = pl.BlockSpec((1, D_CONV), lambda b: (0, 0))
    const = lambda a: pl.BlockSpec(a.shape, lambda b: (0, 0))
    return pl.pallas_call(
        _convfourier_kernel,
        grid=(t // seq,),
        in_specs=[pl.BlockSpec((seq, 2 * D_CONV), lambda b: (b, 0)),
                  pl.BlockSpec((CONV_WIDTH, D_CONV), lambda b: (0, 0)),
                  vec, vec, vec,
                  pl.BlockSpec((D_CONV, D_CONV), lambda b: (0, 0)),
                  vec,
                  const(cmat), const(smat), const(perm), const(alt_col), const(alt_row),
                  pl.BlockSpec((seq, 2 * D_FOURIER), lambda b: (0, b))],
        out_specs=[pl.BlockSpec((seq, D_CONV), lambda b: (b, 0)),
                   pl.BlockSpec((seq, D_FOURIER), lambda b: (0, b))],
        out_shape=[jax.ShapeDtypeStruct((t, D_CONV), BF16),
                   jax.ShapeDtypeStruct((seq, t // seq * D_FOURIER), BF16)],
        scratch_shapes=[pltpu.VMEM((SUBLANES, seq + 2 * CONV_PAD, D_CONV), F32),
                        pltpu.VMEM((half, 2 * D_FOURIER), BF16), pltpu.VMEM((half, D_FOURIER), BF16)],
        compiler_params=_params("parallel"),
        name="conv_fourier",
    )(uc, *conv_p, cmat, smat, perm, alt_col, alt_row, zf)


def _head_mask():
    return lax.broadcasted_iota(jnp.int32, (1, LANES), 1) < NA_HEAD_DIM


Q_ROWS = 2
BAND_ROWS = 10
assert BAND_ROWS >= WIN_ROWS + Q_ROWS - 1 and (BAND_ROWS * GRID_W) % LANES == 0


def _na_patterns(rows):
    starts, sigs = [], []
    for g in range(rows // Q_ROWS):
        start = int(np.clip(Q_ROWS * g - WIN_ROWS // 2, 0, rows - BAND_ROWS))
        r = Q_ROWS * g + np.arange(Q_ROWS)
        rs = np.clip(r - WIN_ROWS // 2, 0, rows - WIN_ROWS)
        starts.append(start)
        sigs.append((start - Q_ROWS * g,) + tuple(rs - r))
    run_starts = [g for g in range(len(sigs)) if g == 0 or sigs[g] != sigs[g - 1]]
    assert len(set(sigs)) == len(run_starts)
    return starts, run_starts


def _na_kernel(q_ref, k_ref, va_ref, vb_ref, kc_ref, vca_ref, vcb_ref, rpb_ref, o_ref, s_ref, p_ref, bias_ref, *,
               seq, n_ctx):
    rows = seq // GRID_W
    per_batch = rows // Q_ROWS
    n_groups = q_ref.shape[0] // seq * per_batch
    n_q = Q_ROWS * GRID_W
    n_loc = BAND_ROWS * GRID_W
    _, run_starts = _na_patterns(rows)
    first = _head_mask()
    dn = (((1,), (1,)), ((), ()))
    assert n_groups >= 4 and seq == per_batch * n_q

    def split(t):
        t = jnp.asarray(t, jnp.int32)
        return t // per_batch, t % per_batch

    def band(t):
        b, g = split(t)
        start = jnp.clip(Q_ROWS * g - WIN_ROWS // 2, 0, rows - BAND_ROWS)
        return pl.multiple_of(b * seq + start * GRID_W, LANES)

    def q_start(t):
        return pl.multiple_of(jnp.asarray(t, jnp.int32) * n_q, n_q)

    def ctx_start(t):
        return pl.multiple_of(split(t)[0] * n_ctx, n_ctx)

    def scores(t, slot):
        g = split(t)[1]
        pat = sum((g >= s).astype(jnp.int32) for s in run_starts[1:])
        q = q_ref[pl.ds(q_start(t), n_q), :]
        kb = k_ref[pl.ds(band(t), n_loc), :]
        kc = kc_ref[pl.ds(ctx_start(t), n_ctx), :]
        for a in range(2):
            qa = jnp.where(first if a == 0 else ~first, q, jnp.zeros_like(q))
            s_ref[slot, a, :, :n_loc] = lax.dot_general(qa, kb, dn, preferred_element_type=F32) + bias_ref[a, pat]
            s_ref[slot, a, :, n_loc:] = lax.dot_general(qa, kc, dn, preferred_element_type=F32)

    def softmax(slot):
        for a in range(2):
            s = s_ref[slot, a]
            p_ref[slot, a] = jnp.exp((s - jnp.max(s, axis=-1, keepdims=True)).astype(BF16))

    def values(t, slot):
        outs = []
        for a, (v_ref, vc_ref) in enumerate(((va_ref, vca_ref), (vb_ref, vcb_ref))):
            o = (_bdot(p_ref[slot, a, :, :n_loc], v_ref[pl.ds(band(t), n_loc), :])
                 + _bdot(p_ref[slot, a, :, n_loc:], vc_ref[pl.ds(ctx_start(t), n_ctx), :]))
            outs.append(o * (1.0 / pltpu.roll(o, NA_HEAD_DIM, axis=1)))
        o_ref[pl.ds(q_start(t), n_q), :] = jnp.where(first, outs[0], outs[1]).astype(BF16)

    def tick(t, parity):
        values(t - 1, 1 - parity)
        scores(t + 1, 1 - parity)
        softmax(parity)

    def quad(j, carry):
        t = 4 * j + 1
        tick(t, 1)
        tick(t + 1, 0)
        tick(t + 2, 1)
        tick(t + 3, 0)
        return carry

    @pl.when(pl.program_id(1) == 0)
    def _():
        _na_assemble_bias(rpb_ref, bias_ref, rows)

    scores(0, 0)
    scores(1, 1)
    softmax(0)
    n_quads = (n_groups - 2) // 4
    lax.fori_loop(0, n_quads, quad, 0)
    for t in range(4 * n_quads + 1, n_groups - 1):
        tick(t, t % 2)
    softmax((n_groups - 1) % 2)
    values(n_groups - 2, n_groups % 2)
    values(n_groups - 1, (n_groups - 1) % 2)


def _na_bias(rpb):
    n_heads, n_dr, n_dc = rpb.shape
    r = rpb.astype(F32)
    w = jnp.concatenate([r[..., WIN_COLS - 1:], jnp.zeros((n_heads, n_dr, LANES - n_dc), F32), r[..., :WIN_COLS - 1]],
                        axis=-1)
    w = jnp.pad(w, ((0, 0), (0, 1), (0, 0)))
    return w.reshape(HEAD_PAIRS, 2, n_dr + 1, LANES)


def _na_assemble_bias(w_ref, bias_ref, rows):
    starts, run_starts = _na_patterns(rows)
    cq = lax.broadcasted_iota(jnp.int32, (GRID_W, LANES), 0)
    lane = lax.broadcasted_iota(jnp.int32, (GRID_W, LANES), 1)
    low = lane < GRID_W
    ck = jnp.where(low, lane, lane - GRID_W)
    col_start = jnp.clip(cq - WIN_COLS // 2, 0, GRID_W - WIN_COLS)
    in_window = (ck >= col_start) & (ck < col_start + WIN_COLS)
    masked = jnp.full((GRID_W, LANES), -jnp.inf, F32)
    rolled = {}

    def half(a, d, upper):
        if (a, d, upper) not in rolled:
            row = jnp.broadcast_to(w_ref[0, a, d:d + 1, :], (GRID_W, LANES))
            rolled[a, d, upper] = pltpu.roll(row, GRID_W if upper else 0, axis=1, stride=1, stride_axis=0)
        return rolled[a, d, upper]

    for a in range(2):
        for p, g in enumerate(run_starts):
            for qi in range(Q_ROWS):
                r = Q_ROWS * g + qi
                rs = int(np.clip(r - WIN_ROWS // 2, 0, rows - WIN_ROWS))
                below = rs - starts[g]
                dr0 = rs - r + WIN_ROWS - 1
                for t in range(BAND_ROWS // 2):
                    d0 = dr0 + 2 * t - below
                    v0 = below <= 2 * t < below + WIN_ROWS
                    v1 = below <= 2 * t + 1 < below + WIN_ROWS
                    if v0 and v1:
                        tile = jnp.where(low, half(a, d0, False), half(a, d0 + 1, True))
                    elif v0:
                        tile = jnp.where(low, half(a, d0, False), masked)
                    elif v1:
                        tile = jnp.where(low, masked, half(a, d0 + 1, True))
                    else:
                        tile = masked
                    if v0 or v1:
                        tile = jnp.where(in_window, tile, masked)
                    bias_ref[a, p, qi * GRID_W:(qi + 1) * GRID_W, t * LANES:(t + 1) * LANES] = tile


NA_BATCHES = 4


def _na(qkv, kv_ctx, rpb_rows, *, seq, n_ctx, kc_off):
    t = qkv.shape[0]
    hp = HEAD_PAIRS
    tb, tbc = NA_BATCHES * seq, NA_BATCHES * n_ctx
    n_pat = len(_na_patterns(seq // GRID_W)[1])
    n_keys = BAND_ROWS * GRID_W + n_ctx
    assert t % tb == 0
    return pl.pallas_call(
        functools.partial(_na_kernel, seq=seq, n_ctx=n_ctx),
        grid=(hp, t // tb),
        in_specs=[pl.BlockSpec((tb, LANES), lambda h, b: (b, h)),
                  pl.BlockSpec((tb, LANES), lambda h, b: (b, hp + h)),
                  pl.BlockSpec((tb, LANES), lambda h, b: (b, 2 * hp + h)),
                  pl.BlockSpec((tb, LANES), lambda h, b: (b, 3 * hp + h)),
                  pl.BlockSpec((tbc, LANES), lambda h, b: (b, kc_off + h)),
                  pl.BlockSpec((tbc, LANES), lambda h, b: (b, kc_off + hp + h)),
                  pl.BlockSpec((tbc, LANES), lambda h, b: (b, kc_off + 2 * hp + h)),
                  pl.BlockSpec((1,) + rpb_rows.shape[1:], lambda h, b: (h, 0, 0, 0))],
        out_specs=pl.BlockSpec((tb, LANES), lambda h, b: (b, h)),
        out_shape=jax.ShapeDtypeStruct((t, D_NA), BF16),
        scratch_shapes=[pltpu.VMEM((2, 2, Q_ROWS * GRID_W, n_keys), F32),
                        pltpu.VMEM((2, 2, Q_ROWS * GRID_W, n_keys), BF16),
                        pltpu.VMEM((2, n_pat, Q_ROWS * GRID_W, BAND_ROWS * GRID_W), F32)],
        compiler_params=_params("parallel", "arbitrary"),
        name="na_attention",
    )(qkv, qkv, qkv, qkv, kv_ctx, kv_ctx, kv_ctx, rpb_rows)


def _ctx_attn_kernel(q_ref, k_ref, va_ref, vb_ref, o_ref):
    first = _head_mask()
    for h in range(HEAD_PAIRS):
        lanes = slice(h * LANES, (h + 1) * LANES)
        q = q_ref[:, lanes]
        k = k_ref[:, lanes]
        outs = []
        for a, v_ref in enumerate((va_ref, vb_ref)):
            qa = jnp.where(first if a == 0 else ~first, q, jnp.zeros_like(q))
            s = lax.dot_general(qa, k, (((1,), (1,)), ((), ())), preferred_element_type=F32)
            p = jnp.exp(s - jnp.max(s, axis=-1, keepdims=True))
            den = jnp.sum(p, axis=-1, keepdims=True)
            outs.append(_bdot(p.astype(BF16), v_ref[:, lanes]) * (1.0 / den))
        o_ref[:, lanes] = jnp.where(first, outs[0], outs[1]).astype(BF16)


def _ctx_attn(qkv, *, n_ctx):
    t = qkv.shape[0]
    return pl.pallas_call(
        _ctx_attn_kernel,
        grid=(t // n_ctx,),
        in_specs=[pl.BlockSpec((n_ctx, D_NA), lambda b: (b, 0)),
                  pl.BlockSpec((n_ctx, D_NA), lambda b: (b, 1)),
                  pl.BlockSpec((n_ctx, D_NA), lambda b: (b, 2)),
                  pl.BlockSpec((n_ctx, D_NA), lambda b: (b, 3))],
        out_specs=pl.BlockSpec((n_ctx, D_NA), lambda b: (b, 0)),
        out_shape=jax.ShapeDtypeStruct((t, D_NA), BF16),
        compiler_params=_params("parallel"),
        name="ctx_attention",
    )(qkv, qkv, qkv, qkv)


def _outmlp_kernel(h_ref, yf_ref, cv_ref, at_ref, mod_ref, g_ref, gf_ref, wo_ref, w1_ref, w2_ref, o_ref, *, final_norm):
    mix = (_bdot(yf_ref[...], wo_ref[:D_FOURIER, :])
           + _bdot(cv_ref[...], wo_ref[D_FOURIER:D_FOURIER + D_CONV, :])
           + _bdot(at_ref[...], wo_ref[D_FOURIER + D_CONV:, :]))
    h1 = h_ref[...] + mod_ref[0, 2:3, :] * mix
    hn = _rms_mod(h1, g_ref[...], mod_ref[0, 3:4, :], mod_ref[0, 4:5, :]).astype(BF16)
    a = jnp.maximum(_bdot(hn, w1_ref[...]), 0.0)
    out = h1 + mod_ref[0, 5:6, :] * _bdot((a * a).astype(BF16), w2_ref[...])
    if final_norm:
        out = out * lax.rsqrt(jnp.mean(out * out, axis=-1, keepdims=True) + RMS_EPS) * gf_ref[...]
    o_ref[...] = out


def _outmlp(h, yf, cv, at, mods, g2, gf, wo, w1, w2, *, layer, seq, tm, ctx, final_norm):
    t, d = h.shape
    nb = t // seq
    per = seq // tm
    row = (lambda i: nb) if ctx else (lambda i: i // per)
    d_ff = w1.shape[2]
    vec = pl.BlockSpec((1, d), lambda i: (0, 0))
    once = dict(pipeline_mode=pl.Buffered(1))
    return pl.pallas_call(
        functools.partial(_outmlp_kernel, final_norm=final_norm),
        grid=(t // tm,),
        in_specs=[pl.BlockSpec((tm, d), lambda i: (i, 0)),
                  pl.BlockSpec((tm, D_FOURIER), lambda i: (i % per, i // per)),
                  pl.BlockSpec((tm, D_CONV), lambda i: (i, 0)),
                  pl.BlockSpec((tm, D_NA), lambda i: (i, 0)),
                  pl.BlockSpec((1, N_MOD, d), lambda i: (row(i), 0, 0)),
                  vec, vec,
                  pl.BlockSpec((None, D_MIX, d), lambda i: (layer, 0, 0), **once),
                  pl.BlockSpec((None, d, d_ff), lambda i: (layer, 0, 0), **once),
                  pl.BlockSpec((None, d_ff, d), lambda i: (layer, 0, 0), **once)],
        out_specs=pl.BlockSpec((tm, d), lambda i: (i, 0)),
        out_shape=jax.ShapeDtypeStruct((t, d), F32),
        compiler_params=_params("parallel"),
        name="outmlp_ctx" if ctx else "outmlp_lat",
    )(h, yf, cv, at, mods, g2, gf, wo, w1, w2)


def kernel(x, c, ctx, c_ctx, ada_w, ada_b, norm1_g, norm2_g, w_in, w_fourier, conv_dw_w, conv_dw_b, conv_norm_g,
           conv_norm_b, conv_pw_w, conv_pw_b, na_rpb, w_out, mlp_w1, mlp_w2, final_norm_g):
    nb, seq, d = x.shape
    n_ctx = ctx.shape[1]
    depth = ada_w.shape[0]
    assert nb < MOD_ROWS and seq % GRID_W == 0 and d == D_MODEL

    cc = jnp.concatenate([c, c_ctx[None], jnp.zeros((MOD_ROWS - nb - 1, d), F32)], axis=0)
    mods = _adaln(cc, ada_w, ada_b).reshape(depth, MOD_ROWS, N_MOD, d)
    ab_lat = _fold_fourier(w_fourier, seq)
    ab_ctx = _fold_fourier(w_fourier, n_ctx)
    dft_lat = _dft_mats(seq)
    dft_ctx = _dft_mats(n_ctx)

    w_in_b = w_in.astype(BF16)
    w_out_b = w_out.astype(BF16)
    w1_b = mlp_w1.astype(BF16)
    w2_b = mlp_w2.astype(BF16)
    pww_b = conv_pw_w.astype(BF16)
    gf = final_norm_g.reshape(1, d)

    h_lat = x.reshape(nb * seq, d)
    h_ctx = ctx.reshape(nb * n_ctx, d)
    for i in range(depth):
        last = i == depth - 1
        g1 = norm1_g[i].reshape(1, d)
        g2 = norm2_g[i].reshape(1, d)
        conv_p = (conv_dw_w[i], conv_dw_b[i].reshape(1, -1), conv_norm_g[i].reshape(1, -1),
                  conv_norm_b[i].reshape(1, -1), pww_b[i], conv_pw_b[i].reshape(1, -1))
        bias = _na_bias(na_rpb[i])

        if last:
            kv_ctx = _kvproj(h_ctx, g1, mods[i], w_in_b, layer=i, nb=nb, tm=n_ctx)
            kc_off = 0
        else:
            zf_c, uc_c, qkv_c = _inproj(h_ctx, g1, mods[i], w_in_b, ab_ctx[i], layer=i, seq=n_ctx, tm=n_ctx, ctx=True)
            kv_ctx = qkv_c
            kc_off = HEAD_PAIRS

        zf, uc, qkv = _inproj(h_lat, g1, mods[i], w_in_b, ab_lat[i], layer=i, seq=seq, tm=TM_INPROJ, ctx=False)
        cv, yf = _convfourier(uc, conv_p, dft_lat, zf, seq=seq)
        at = _na(qkv, kv_ctx, bias, seq=seq, n_ctx=n_ctx, kc_off=kc_off)
        h_lat = _outmlp(h_lat, yf, cv, at, mods[i], g2, gf, w_out_b, w1_b, w2_b,
                        layer=i, seq=seq, tm=TM_MLP, ctx=False, final_norm=last)

        if not last:
            cv_c, yf_c = _convfourier(uc_c, conv_p, dft_ctx, zf_c, seq=n_ctx)
            at_c = _ctx_attn(qkv_c, n_ctx=n_ctx)
            h_ctx = _outmlp(h_ctx, yf_c, cv_c, at_c, mods[i], g2, gf, w_out_b, w1_b, w2_b,
                            layer=i, seq=n_ctx, tm=n_ctx, ctx=True, final_norm=False)
    return h_lat.reshape(nb, seq, d)
```

```python
import functools
from typing import Callable, NamedTuple

import numpy as np
import jax
import jax.numpy as jnp
from jax import lax
from jax.experimental import pallas as pl
from jax.experimental.pallas import tpu as pltpu

D_MODEL = 1024
GRID_W = 64
D_FOURIER = 256
FOURIER_GROUPS = 4
D_CONV = 256
CONV_WIDTH = 31
N_NA_HEADS = 8
NA_HEAD_DIM = 64
D_NA = N_NA_HEADS * NA_HEAD_DIM
WIN_ROWS = 8
WIN_COLS = 16
QKV_START = D_FOURIER + 2 * D_CONV
KV_START = QKV_START + D_NA
D_MIX = D_FOURIER + D_CONV + D_NA
N_MOD = 6
RMS_EPS = 1e-6
LN_EPS = 1e-5

LANES = 128
SUBLANES = 8
HEAD_PAIRS = N_NA_HEADS * NA_HEAD_DIM // LANES
MOD_ROWS = 16
VMEM_LIMIT = 60 * 1024 * 1024

TM_INPROJ = 1024
TM_MLP = 512
TN_ADALN = 2048

F32 = jnp.float32
BF16 = jnp.bfloat16


def _params(*sem):
    return pltpu.CompilerParams(dimension_semantics=sem, vmem_limit_bytes=VMEM_LIMIT)


def _sigmoid(x):
    return 1.0 / (1.0 + jnp.exp(-x))


def _rms_mod(x, g, shift, scale):
    return x * lax.rsqrt(jnp.mean(x * x, axis=-1, keepdims=True) + RMS_EPS) * (g * (1.0 + scale)) + shift


def _bdot(a, b):
    return jnp.dot(a, b, preferred_element_type=F32)


def _adaln_kernel(c_ref, w_ref, b_ref, o_ref):
    cc = c_ref[...]
    a = cc * _sigmoid(cc)
    w = w_ref[0]
    a_hi = a.astype(BF16)
    a_lo = (a - a_hi.astype(F32)).astype(BF16)
    w_hi = w.astype(BF16)
    w_lo = (w - w_hi.astype(F32)).astype(BF16)
    by_hi = _bdot(jnp.concatenate([a_hi, a_lo], axis=0), w_hi)
    o_ref[0] = by_hi[:MOD_ROWS] + by_hi[MOD_ROWS:] + _bdot(a_hi, w_lo) + b_ref[0]


def _adaln(cc, ada_w, ada_b):
    depth, d, n = ada_w.shape
    tn = TN_ADALN
    return pl.pallas_call(
        _adaln_kernel,
        grid=(depth, n // tn),
        in_specs=[pl.BlockSpec((MOD_ROWS, d), lambda l, j: (0, 0)),
                  pl.BlockSpec((1, d, tn), lambda l, j: (l, 0, j)),
                  pl.BlockSpec((1, 1, tn), lambda l, j: (l, 0, j))],
        out_specs=pl.BlockSpec((1, MOD_ROWS, tn), lambda l, j: (l, 0, j)),
        out_shape=jax.ShapeDtypeStruct((depth, MOD_ROWS, n), F32),
        compiler_params=_params("parallel", "parallel"),
        name="adaln",
    )(cc, ada_w, ada_b.reshape(depth, 1, n))


def _fold_kernel(cc_ref, sc_ref, w_ref, o_ref):
    w = w_ref[0]
    hi = lax.Precision.HIGHEST
    o_ref[0, :, :D_FOURIER] = jnp.dot(cc_ref[...], w, preferred_element_type=F32, precision=hi).astype(BF16)
    o_ref[0, :, D_FOURIER:] = jnp.dot(sc_ref[...], w, preferred_element_type=F32, precision=hi).astype(BF16)


def _fold_fourier(w_fourier, seq):
    depth = w_fourier.shape[0]
    gs = D_FOURIER // FOURIER_GROUPS
    idx = np.arange(D_FOURIER)
    same = (idx[:, None] // gs) == (idx[None, :] // gs)
    ang = 2.0 * np.pi * (((idx[:, None] % gs) * (idx[None, :] % gs)) % gs) / gs
    scale = 1.0 / np.sqrt(seq * gs)
    cc = jnp.asarray(np.where(same, np.cos(ang), 0.0) * scale, F32)
    sc = jnp.asarray(np.where(same, np.sin(ang), 0.0) * scale, F32)
    full = pl.BlockSpec((D_FOURIER, D_FOURIER), lambda l: (0, 0))
    return pl.pallas_call(
        _fold_kernel,
        grid=(depth,),
        in_specs=[full, full, pl.BlockSpec((1, D_FOURIER, D_FOURIER), lambda l: (l, 0, 0))],
        out_specs=pl.BlockSpec((1, D_FOURIER, 2 * D_FOURIER), lambda l: (l, 0, 0)),
        out_shape=jax.ShapeDtypeStruct((depth, D_FOURIER, 2 * D_FOURIER), BF16),
        compiler_params=_params("parallel"),
        name="fold_fourier",
    )(cc, sc, w_fourier)


FLIP_BLOCK = 128


def _dft_mats(seq):
    half = seq // 2
    k = np.arange(half)
    ang = 2.0 * np.pi * ((k[:, None] * k[None, :]) % seq) / seq
    r = np.arange(FLIP_BLOCK)
    perm = (np.arange(2 * FLIP_BLOCK)[None, :] == FLIP_BLOCK - r[:, None]).astype(np.float32)
    alt_col = np.where(k % 2 == 0, 1.0, -1.0).astype(np.float32)[:, None]
    alt_row = np.zeros((SUBLANES, seq), np.float32)
    alt_row[0] = np.where(np.arange(seq) % 2 == 0, 1.0, -1.0)
    return (jnp.asarray(np.cos(ang), F32).astype(BF16), jnp.asarray(np.sin(ang), F32).astype(BF16),
            jnp.asarray(perm, F32).astype(BF16), jnp.asarray(alt_col), jnp.asarray(alt_row, F32).astype(BF16))


def _store_kv(hn, w_ref, out_ref, rows, col):
    kv = _bdot(hn, w_ref[:, KV_START:])
    v = kv[:, D_NA:]
    first = lax.broadcasted_iota(jnp.int32, (1, D_NA), 1) % LANES < NA_HEAD_DIM
    out_ref[rows, col:col + D_NA] = kv[:, :D_NA].astype(BF16)
    out_ref[rows, col + D_NA:col + 2 * D_NA] = jnp.where(first, v, 1.0).astype(BF16)
    out_ref[rows, col + 2 * D_NA:col + 3 * D_NA] = jnp.where(first, 1.0, v).astype(BF16)


INPROJ_SPLIT_ROWS = 512


def _inproj_kernel(x_ref, g_ref, mod_ref, w_ref, ab_ref, zf_ref, uc_ref, qkv_ref):
    tm = x_ref.shape[0]
    part = min(tm, INPROJ_SPLIT_ROWS)
    for r in range(tm // part):
        rows = slice(r * part, (r + 1) * part)
        hn = _rms_mod(x_ref[rows, :], g_ref[...], mod_ref[0, 0:1, :], mod_ref[0, 1:2, :]).astype(BF16)
        uf = _bdot(hn, w_ref[:, :D_FOURIER])
        uc_ref[rows, :] = _bdot(hn, w_ref[:, D_FOURIER:QKV_START])
        qkv_ref[rows, :D_NA] = (_bdot(hn, w_ref[:, QKV_START:KV_START]) * (NA_HEAD_DIM ** -0.5)).astype(BF16)
        _store_kv(hn, w_ref, qkv_ref, rows, D_NA)
        zf_ref[rows, :] = _bdot(uf.astype(BF16), ab_ref[...]).astype(BF16)


def _inproj(h, g, mods, w_in, ab, *, layer, seq, tm, ctx):
    t, d = h.shape
    nb = t // seq
    per = seq // tm
    row = (lambda i: nb) if ctx else (lambda i: i // per)
    d_in = w_in.shape[2]
    return pl.pallas_call(
        _inproj_kernel,
        grid=(t // tm,),
        in_specs=[pl.BlockSpec((tm, d), lambda i: (i, 0)),
                  pl.BlockSpec((1, d), lambda i: (0, 0)),
                  pl.BlockSpec((1, N_MOD, d), lambda i: (row(i), 0, 0)),
                  pl.BlockSpec((None, d, d_in), lambda i: (layer, 0, 0), pipeline_mode=pl.Buffered(1)),
                  pl.BlockSpec((D_FOURIER, 2 * D_FOURIER), lambda i: (0, 0))],
        out_specs=[pl.BlockSpec((tm, 2 * D_FOURIER), lambda i: (i % per, i // per)),
                   pl.BlockSpec((tm, 2 * D_CONV), lambda i: (i, 0)),
                   pl.BlockSpec((tm, 4 * D_NA), lambda i: (i, 0))],
        out_shape=[jax.ShapeDtypeStruct((seq, nb * 2 * D_FOURIER), BF16),
                   jax.ShapeDtypeStruct((t, 2 * D_CONV), F32),
                   jax.ShapeDtypeStruct((t, 4 * D_NA), BF16)],
        compiler_params=_params("parallel"),
        name="inproj_ctx" if ctx else "inproj_lat",
    )(h, g, mods, w_in, ab)


def _kvproj_kernel(x_ref, g_ref, mod_ref, w_ref, kv_ref):
    hn = _rms_mod(x_ref[...], g_ref[...], mod_ref[0, 0:1, :], mod_ref[0, 1:2, :]).astype(BF16)
    _store_kv(hn, w_ref, kv_ref, slice(None), 0)


def _kvproj(h, g, mods, w_in, *, layer, nb, tm):
    t, d = h.shape
    d_in = w_in.shape[2]
    n = 3 * D_NA
    return pl.pallas_call(
        _kvproj_kernel,
        grid=(t // tm,),
        in_specs=[pl.BlockSpec((tm, d), lambda i: (i, 0)),
                  pl.BlockSpec((1, d), lambda i: (0, 0)),
                  pl.BlockSpec((1, N_MOD, d), lambda i: (nb, 0, 0)),
                  pl.BlockSpec((None, d, d_in), lambda i: (layer, 0, 0))],
        out_specs=pl.BlockSpec((tm, n), lambda i: (i, 0)),
        out_shape=jax.ShapeDtypeStruct((t, n), BF16),
        compiler_params=_params("parallel"),
        name="kvproj_ctx",
    )(h, g, mods, w_in)


CONV_PAD = 16
CONV_CHUNK = 512
CONV_SPAN = 512


N_CONVFOURIER_IN = 13


def _convfourier_kernel(*refs, rider):
    (u_ref, dww_ref, dwb_ref, lng_ref, lnb_ref, pww_ref, pwb_ref,
     c_ref, s_ref, perm_ref, altc_ref, altr_ref, z_ref) = refs[:N_CONVFOURIER_IN]
    n_in = N_CONVFOURIER_IN + (rider.n_in if rider else 0)
    n_out = 2 + (rider.n_out if rider else 0)
    o_ref, y_ref = refs[n_in:n_in + 2]
    vs_ref, ze_ref, yr_ref = refs[n_in + n_out:n_in + n_out + 3]
    seq = u_ref.shape[0]
    half = seq // 2
    fb = FLIP_BLOCK
    nblk = half // fb
    first = CONV_PAD - CONV_WIDTH // 2
    rows = min(CONV_CHUNK, seq)
    span = min(CONV_SPAN, seq)
    n_pad = span + 2 * CONV_PAD
    n_chunks = seq // rows
    assert (first + CONV_WIDTH - 1) // SUBLANES * SUBLANES + span <= n_pad - SUBLANES

    def fill_span(lo):
        a, b = max(lo - CONV_PAD, 0), min(lo + span + CONV_PAD, seq)
        zeros = jnp.zeros((CONV_PAD, D_CONV), F32)
        if lo == 0:
            vs_ref[0, 0:CONV_PAD, :] = zeros
        if lo + span == seq:
            vs_ref[0, span + CONV_PAD:n_pad, :] = zeros
        dst = a - (lo - CONV_PAD)
        vs_ref[0, dst:dst + b - a, :] = u_ref[a:b, :D_CONV] * _sigmoid(u_ref[a:b, D_CONV:])
        for s in range(1, SUBLANES):
            vs_ref[s, 0:n_pad - SUBLANES, :] = vs_ref[0, s:s + n_pad - SUBLANES, :]

    def conv_chunk(lo, base):
        acc = jnp.zeros((rows, D_CONV), F32) + dwb_ref[...]
        for t in range(CONV_WIDTH):
            s, a = (first + t) % SUBLANES, (first + t) // SUBLANES
            acc = acc + vs_ref[s, base + a * SUBLANES:base + a * SUBLANES + rows, :] * dww_ref[t:t + 1, :]
        mu = jnp.mean(acc, axis=-1, keepdims=True)
        cen = acc - mu
        var = jnp.mean(cen * cen, axis=-1, keepdims=True)
        y = cen * lax.rsqrt(var + LN_EPS) * lng_ref[...] + lnb_ref[...]
        y = y * _sigmoid(y)
        o_ref[lo + base:lo + base + rows, :] = (_bdot(y.astype(BF16), pww_ref[...]) + pwb_ref[...]).astype(BF16)

    def reversed_block(ref, i, n):
        if i == 0:
            return _bdot(perm_ref[:, :fb], ref[n - fb:n, :])
        lo = n - (i + 1) * fb
        return _bdot(perm_ref[...], ref[lo:lo + 2 * fb, :])

    def fold_in():
        for i in range(nblk):
            rows_i = slice(i * fb, (i + 1) * fb)
            zr = reversed_block(z_ref, i, seq)
            ze_ref[rows_i, :D_FOURIER] = (z_ref[rows_i, :D_FOURIER].astype(F32) + zr[:, :D_FOURIER]).astype(BF16)
            ze_ref[rows_i, D_FOURIER:] = (z_ref[rows_i, D_FOURIER:].astype(F32) - zr[:, D_FOURIER:]).astype(BF16)

    def dft(part, n_parts):
        r = slice(part * half // n_parts, (part + 1) * half // n_parts)
        nyq = z_ref[half:half + 1, :D_FOURIER].astype(F32)
        yc = _bdot(c_ref[r, :], ze_ref[:, :D_FOURIER]) + altc_ref[r, :] * nyq
        ys = _bdot(s_ref[r, :], ze_ref[:, D_FOURIER:])
        y_ref[r, :] = (yc - ys).astype(BF16)
        yr_ref[r, :] = (yc + ys).astype(BF16)

    def fold_out():
        y_mid = _bdot(altr_ref[...], z_ref[:, :D_FOURIER])[0:1, :]
        first_row = lax.broadcasted_iota(jnp.int32, (fb, 1), 0) == 0
        for i in range(nblk):
            blk = reversed_block(yr_ref, i, half)
            if i == 0:
                blk = jnp.where(first_row, y_mid, blk)
            y_ref[half + i * fb:half + (i + 1) * fb, :] = blk.astype(BF16)

    phases = [[fold_in], [functools.partial(dft, 0, 2)], [functools.partial(dft, 1, 2)], [fold_out]]
    if rider:
        extra = rider.phases(refs[N_CONVFOURIER_IN:n_in], refs[n_in + 2:n_in + n_out], refs[n_in + n_out + 3:])
        assert len(extra) == len(phases)
        phases = [own + more for own, more in zip(phases, extra)]
    for ci in range(n_chunks):
        lo = ci * rows // span * span
        if ci * rows == lo:
            fill_span(lo)
        for group in phases[ci * len(phases) // n_chunks:(ci + 1) * len(phases) // n_chunks]:
            for phase in group:
                phase()
        conv_chunk(lo, ci * rows - lo)


class Rider(NamedTuple):
    phases: Callable
    inputs: tuple
    in_specs: tuple
    out_shapes: tuple
    out_specs: tuple
    scratch: tuple

    @property
    def n_in(self):
        return len(self.inputs)

    @property
    def n_out(self):
        return len(self.out_shapes)


def _convfourier(uc, conv_p, mats, zf, *, seq, rider=None):
    t = uc.shape[0]
    cmat, smat, perm, alt_col, alt_row = mats
    half = seq // 2
    vec = pl.BlockSpec((1, D_CONV), lambda b: (0, 0))
    const = lambda a: pl.BlockSpec(a.shape, lambda b: (0, 0), pipeline_mode=pl.Buffered(1))
    in_specs = [pl.BlockSpec((seq, 2 * D_CONV), lambda b: (b, 0)),
                pl.BlockSpec((CONV_WIDTH, D_CONV), lambda b: (0, 0)),
                vec, vec, vec,
                pl.BlockSpec((D_CONV, D_CONV), lambda b: (0, 0)),
                vec,
                const(cmat), const(smat), const(perm), const(alt_col), const(alt_row),
                pl.BlockSpec((seq, 2 * D_FOURIER), lambda b: (0, b))]
    assert len(in_specs) == N_CONVFOURIER_IN
    out_specs = [pl.BlockSpec((seq, D_CONV), lambda b: (b, 0)),
                 pl.BlockSpec((seq, D_FOURIER), lambda b: (0, b))]
    out_shape = [jax.ShapeDtypeStruct((t, D_CONV), BF16),
                 jax.ShapeDtypeStruct((seq, t // seq * D_FOURIER), BF16)]
    scratch = [pltpu.VMEM((SUBLANES, min(CONV_SPAN, seq) + 2 * CONV_PAD, D_CONV), F32),
               pltpu.VMEM((half, 2 * D_FOURIER), BF16), pltpu.VMEM((half, D_FOURIER), BF16)]
    operands = [uc, *conv_p, cmat, smat, perm, alt_col, alt_row, zf]
    if rider:
        in_specs += rider.in_specs
        out_specs += rider.out_specs
        out_shape += rider.out_shapes
        scratch += rider.scratch
        operands += rider.inputs
    return pl.pallas_call(
        functools.partial(_convfourier_kernel, rider=rider),
        grid=(t // seq,),
        in_specs=in_specs,
        out_specs=out_specs,
        out_shape=out_shape,
        scratch_shapes=scratch,
        compiler_params=_params("parallel"),
        name="conv_fourier",
    )(*operands)


def _inproj_rider(h_ctx, g1, mods, w_in, ab, *, layer, nb, n_ctx):
    d = h_ctx.shape[1]
    d_in = w_in.shape[2]
    t = h_ctx.shape[0]

    def phases(in_refs, out_refs, scratch_refs):
        return [[], [functools.partial(_inproj_kernel, *in_refs, *out_refs)], [], []]

    return Rider(
        phases=phases,
        inputs=(h_ctx, g1, mods, w_in, ab),
        in_specs=(pl.BlockSpec((n_ctx, d), lambda b: (b, 0)),
                  pl.BlockSpec((1, d), lambda b: (0, 0)),
                  pl.BlockSpec((1, N_MOD, d), lambda b: (nb, 0, 0)),
                  pl.BlockSpec((None, d, d_in), lambda b: (layer, 0, 0), pipeline_mode=pl.Buffered(1)),
                  pl.BlockSpec((D_FOURIER, 2 * D_FOURIER), lambda b: (0, 0))),
        out_shapes=(jax.ShapeDtypeStruct((n_ctx, nb * 2 * D_FOURIER), BF16),
                    jax.ShapeDtypeStruct((t, 2 * D_CONV), F32),
                    jax.ShapeDtypeStruct((t, 4 * D_NA), BF16)),
        out_specs=(pl.BlockSpec((n_ctx, 2 * D_FOURIER), lambda b: (0, b)),
                   pl.BlockSpec((n_ctx, 2 * D_CONV), lambda b: (b, 0)),
                   pl.BlockSpec((n_ctx, 4 * D_NA), lambda b: (b, 0))),
        scratch=())


def _ctx_tail_rider(h_ctx, yf_c, cv_c, at_c, mods_prev, mods_cur, g2_prev, g1_cur, wo, w1, w2, w_in, *,
                    layer, nb, n_ctx):
    d = h_ctx.shape[1]
    d_ff = w1.shape[2]
    d_in = w_in.shape[2]
    t = h_ctx.shape[0]

    def phases(in_refs, out_refs, scratch_refs):
        (h_ref, yf_ref, cv_ref, at_ref, modp_ref, modc_ref, g2_ref, g1_ref, wo_ref, w1_ref, w2_ref, w_ref) = in_refs
        (kv_ref,) = out_refs
        h1_ref, hn_ref, a_ref = scratch_refs

        def mix():
            m = (_bdot(yf_ref[...], wo_ref[:D_FOURIER, :])
                 + _bdot(cv_ref[...], wo_ref[D_FOURIER:D_FOURIER + D_CONV, :])
                 + _bdot(at_ref[...], wo_ref[D_FOURIER + D_CONV:, :]))
            h1 = h_ref[...] + modp_ref[0, 2:3, :] * m
            h1_ref[...] = h1
            hn_ref[...] = _rms_mod(h1, g2_ref[...], modp_ref[0, 3:4, :], modp_ref[0, 4:5, :]).astype(BF16)

        def up(lo, hi):
            a = jnp.maximum(_bdot(hn_ref[...], w1_ref[:, lo:hi]), 0.0)
            a_ref[:, lo:hi] = (a * a).astype(BF16)

        def down():
            h2 = h1_ref[...] + modp_ref[0, 5:6, :] * _bdot(a_ref[...], w2_ref[...])
            hn = _rms_mod(h2, g1_ref[...], modc_ref[0, 0:1, :], modc_ref[0, 1:2, :]).astype(BF16)
            _store_kv(hn, w_ref, kv_ref, slice(None), 0)

        return [[mix], [functools.partial(up, 0, d_ff // 2)], [functools.partial(up, d_ff // 2, d_ff)], [down]]

    row = lambda b: (b, 0)
    vec = pl.BlockSpec((1, d), lambda b: (0, 0))
    mod = pl.BlockSpec((1, N_MOD, d), lambda b: (nb, 0, 0))
    once = dict(pipeline_mode=pl.Buffered(1))
    return Rider(
        phases=phases,
        inputs=(h_ctx, yf_c, cv_c, at_c, mods_prev, mods_cur, g2_prev, g1_cur, wo, w1, w2, w_in),
        in_specs=(pl.BlockSpec((n_ctx, d), row),
                  pl.BlockSpec((n_ctx, D_FOURIER), lambda b: (0, b)),
                  pl.BlockSpec((n_ctx, D_CONV), row),
                  pl.BlockSpec((n_ctx, D_NA), row),
                  mod, mod, vec, vec,
                  pl.BlockSpec((None, D_MIX, d), lambda b: (layer - 1, 0, 0), **once),
                  pl.BlockSpec((None, d, d_ff), lambda b: (layer - 1, 0, 0), **once),
                  pl.BlockSpec((None, d_ff, d), lambda b: (layer - 1, 0, 0), **once),
                  pl.BlockSpec((None, d, d_in), lambda b: (layer, 0, 0), **once)),
        out_shapes=(jax.ShapeDtypeStruct((t, 3 * D_NA), BF16),),
        out_specs=(pl.BlockSpec((n_ctx, 3 * D_NA), row),),
        scratch=(pltpu.VMEM((n_ctx, d), F32), pltpu.VMEM((n_ctx, d), BF16), pltpu.VMEM((n_ctx, d_ff), BF16)))


def _head_mask():
    return lax.broadcasted_iota(jnp.int32, (1, LANES), 1) < NA_HEAD_DIM


Q_ROWS = 2
BAND_ROWS = 10
assert BAND_ROWS >= WIN_ROWS + Q_ROWS - 1 and (BAND_ROWS * GRID_W) % LANES == 0


def _na_patterns(rows):
    starts, sigs = [], []
    for g in range(rows // Q_ROWS):
        start = int(np.clip(Q_ROWS * g - WIN_ROWS // 2, 0, rows - BAND_ROWS))
        r = Q_ROWS * g + np.arange(Q_ROWS)
        rs = np.clip(r - WIN_ROWS // 2, 0, rows - WIN_ROWS)
        starts.append(start)
        sigs.append((start - Q_ROWS * g,) + tuple(rs - r))
    run_starts = [g for g in range(len(sigs)) if g == 0 or sigs[g] != sigs[g - 1]]
    assert len(set(sigs)) == len(run_starts)
    return starts, run_starts


def _na_kernel(q_ref, k_ref, va_ref, vb_ref, kc_ref, vca_ref, vcb_ref, rpb_ref, o_ref, s_ref, p_ref, bias_ref, *,
               seq, n_ctx):
    rows = seq // GRID_W
    per_batch = rows // Q_ROWS
    n_groups = q_ref.shape[0] // seq * per_batch
    n_q = Q_ROWS * GRID_W
    n_loc = BAND_ROWS * GRID_W
    _, run_starts = _na_patterns(rows)
    first = _head_mask()
    dn = (((1,), (1,)), ((), ()))
    assert n_groups >= 4 and seq == per_batch * n_q

    def split(t):
        t = jnp.asarray(t, jnp.int32)
        return t // per_batch, t % per_batch

    def band(t):
        b, g = split(t)
        start = jnp.clip(Q_ROWS * g - WIN_ROWS // 2, 0, rows - BAND_ROWS)
        return pl.multiple_of(b * seq + start * GRID_W, LANES)

    def q_start(t):
        return pl.multiple_of(jnp.asarray(t, jnp.int32) * n_q, n_q)

    def ctx_start(t):
        return pl.multiple_of(split(t)[0] * n_ctx, n_ctx)

    def scores(t, slot):
        g = split(t)[1]
        pat = sum((g >= s).astype(jnp.int32) for s in run_starts[1:])
        q = q_ref[pl.ds(q_start(t), n_q), :]
        kb = k_ref[pl.ds(band(t), n_loc), :]
        kc = kc_ref[pl.ds(ctx_start(t), n_ctx), :]
        for a in range(2):
            qa = jnp.where(first if a == 0 else ~first, q, jnp.zeros_like(q))
            s_ref[slot, a, :, :n_loc] = lax.dot_general(qa, kb, dn, preferred_element_type=F32) + bias_ref[a, pat]
            s_ref[slot, a, :, n_loc:] = lax.dot_general(qa, kc, dn, preferred_element_type=F32)

    def softmax(slot):
        for a in range(2):
            s = s_ref[slot, a]
            p_ref[slot, a] = jnp.exp((s - jnp.max(s, axis=-1, keepdims=True)).astype(BF16))

    def values(t, slot):
        outs = []
        for a, (v_ref, vc_ref) in enumerate(((va_ref, vca_ref), (vb_ref, vcb_ref))):
            o = (_bdot(p_ref[slot, a, :, :n_loc], v_ref[pl.ds(band(t), n_loc), :])
                 + _bdot(p_ref[slot, a, :, n_loc:], vc_ref[pl.ds(ctx_start(t), n_ctx), :]))
            outs.append(o * (1.0 / pltpu.roll(o, NA_HEAD_DIM, axis=1)))
        o_ref[pl.ds(q_start(t), n_q), :] = jnp.where(first, outs[0], outs[1]).astype(BF16)

    def tick(t, parity):
        values(t - 1, 1 - parity)
        scores(t + 1, 1 - parity)
        softmax(parity)

    def quad(j, carry):
        t = 4 * j + 1
        tick(t, 1)
        tick(t + 1, 0)
        tick(t + 2, 1)
        tick(t + 3, 0)
        return carry

    @pl.when(pl.program_id(1) == 0)
    def _():
        _na_assemble_bias(rpb_ref, bias_ref, rows)

    scores(0, 0)
    scores(1, 1)
    softmax(0)
    n_quads = (n_groups - 2) // 4
    lax.fori_loop(0, n_quads, quad, 0)
    for t in range(4 * n_quads + 1, n_groups - 1):
        tick(t, t % 2)
    softmax((n_groups - 1) % 2)
    values(n_groups - 2, n_groups % 2)
    values(n_groups - 1, (n_groups - 1) % 2)


def _na_bias(rpb):
    n_heads, n_dr, n_dc = rpb.shape
    r = rpb.astype(F32)
    w = jnp.concatenate([r[..., WIN_COLS - 1:], jnp.zeros((n_heads, n_dr, LANES - n_dc), F32), r[..., :WIN_COLS - 1]],
                        axis=-1)
    w = jnp.pad(w, ((0, 0), (0, 1), (0, 0)))
    return w.reshape(HEAD_PAIRS, 2, n_dr + 1, LANES)


def _na_assemble_bias(w_ref, bias_ref, rows):
    starts, run_starts = _na_patterns(rows)
    cq = lax.broadcasted_iota(jnp.int32, (GRID_W, LANES), 0)
    lane = lax.broadcasted_iota(jnp.int32, (GRID_W, LANES), 1)
    low = lane < GRID_W
    ck = jnp.where(low, lane, lane - GRID_W)
    col_start = jnp.clip(cq - WIN_COLS // 2, 0, GRID_W - WIN_COLS)
    in_window = (ck >= col_start) & (ck < col_start + WIN_COLS)
    masked = jnp.full((GRID_W, LANES), -jnp.inf, F32)
    rolled = {}

    def half(a, d, upper):
        if (a, d, upper) not in rolled:
            row = jnp.broadcast_to(w_ref[0, a, d:d + 1, :], (GRID_W, LANES))
            rolled[a, d, upper] = pltpu.roll(row, GRID_W if upper else 0, axis=1, stride=1, stride_axis=0)
        return rolled[a, d, upper]

    for a in range(2):
        for p, g in enumerate(run_starts):
            for qi in range(Q_ROWS):
                r = Q_ROWS * g + qi
                rs = int(np.clip(r - WIN_ROWS // 2, 0, rows - WIN_ROWS))
                below = rs - starts[g]
                dr0 = rs - r + WIN_ROWS - 1
                for t in range(BAND_ROWS // 2):
                    d0 = dr0 + 2 * t - below
                    v0 = below <= 2 * t < below + WIN_ROWS
                    v1 = below <= 2 * t + 1 < below + WIN_ROWS
                    if v0 and v1:
                        tile = jnp.where(low, half(a, d0, False), half(a, d0 + 1, True))
                    elif v0:
                        tile = jnp.where(low, half(a, d0, False), masked)
                    elif v1:
                        tile = jnp.where(low, masked, half(a, d0 + 1, True))
                    else:
                        tile = masked
                    if v0 or v1:
                        tile = jnp.where(in_window, tile, masked)
                    bias_ref[a, p, qi * GRID_W:(qi + 1) * GRID_W, t * LANES:(t + 1) * LANES] = tile


NA_BATCHES = 4


def _na(qkv, kv_ctx, rpb_rows, *, seq, n_ctx, kc_off):
    t = qkv.shape[0]
    hp = HEAD_PAIRS
    tb, tbc = NA_BATCHES * seq, NA_BATCHES * n_ctx
    n_pat = len(_na_patterns(seq // GRID_W)[1])
    n_keys = BAND_ROWS * GRID_W + n_ctx
    assert t % tb == 0
    return pl.pallas_call(
        functools.partial(_na_kernel, seq=seq, n_ctx=n_ctx),
        grid=(hp, t // tb),
        in_specs=[pl.BlockSpec((tb, LANES), lambda h, b: (b, h)),
                  pl.BlockSpec((tb, LANES), lambda h, b: (b, hp + h)),
                  pl.BlockSpec((tb, LANES), lambda h, b: (b, 2 * hp + h)),
                  pl.BlockSpec((tb, LANES), lambda h, b: (b, 3 * hp + h)),
                  pl.BlockSpec((tbc, LANES), lambda h, b: (b, kc_off + h)),
                  pl.BlockSpec((tbc, LANES), lambda h, b: (b, kc_off + hp + h)),
                  pl.BlockSpec((tbc, LANES), lambda h, b: (b, kc_off + 2 * hp + h)),
                  pl.BlockSpec((1,) + rpb_rows.shape[1:], lambda h, b: (h, 0, 0, 0))],
        out_specs=pl.BlockSpec((tb, LANES), lambda h, b: (b, h)),
        out_shape=jax.ShapeDtypeStruct((t, D_NA), BF16),
        scratch_shapes=[pltpu.VMEM((2, 2, Q_ROWS * GRID_W, n_keys), F32),
                        pltpu.VMEM((2, 2, Q_ROWS * GRID_W, n_keys), BF16),
                        pltpu.VMEM((2, n_pat, Q_ROWS * GRID_W, BAND_ROWS * GRID_W), F32)],
        compiler_params=_params("parallel", "arbitrary"),
        name="na_attention",
    )(qkv, qkv, qkv, qkv, kv_ctx, kv_ctx, kv_ctx, rpb_rows)


def _ctx_attn_kernel(q_ref, k_ref, va_ref, vb_ref, o_ref):
    first = _head_mask()
    for h in range(HEAD_PAIRS):
        lanes = slice(h * LANES, (h + 1) * LANES)
        q = q_ref[:, lanes]
        k = k_ref[:, lanes]
        outs = []
        for a, v_ref in enumerate((va_ref, vb_ref)):
            qa = jnp.where(first if a == 0 else ~first, q, jnp.zeros_like(q))
            s = lax.dot_general(qa, k, (((1,), (1,)), ((), ())), preferred_element_type=F32)
            p = jnp.exp(s - jnp.max(s, axis=-1, keepdims=True))
            den = jnp.sum(p, axis=-1, keepdims=True)
            outs.append(_bdot(p.astype(BF16), v_ref[:, lanes]) * (1.0 / den))
        o_ref[:, lanes] = jnp.where(first, outs[0], outs[1]).astype(BF16)


def _ctx_attn(qkv, *, n_ctx):
    t = qkv.shape[0]
    return pl.pallas_call(
        _ctx_attn_kernel,
        grid=(t // n_ctx,),
        in_specs=[pl.BlockSpec((n_ctx, D_NA), lambda b: (b, 0)),
                  pl.BlockSpec((n_ctx, D_NA), lambda b: (b, 1)),
                  pl.BlockSpec((n_ctx, D_NA), lambda b: (b, 2)),
                  pl.BlockSpec((n_ctx, D_NA), lambda b: (b, 3))],
        out_specs=pl.BlockSpec((n_ctx, D_NA), lambda b: (b, 0)),
        out_shape=jax.ShapeDtypeStruct((t, D_NA), BF16),
        compiler_params=_params("parallel"),
        name="ctx_attention",
    )(qkv, qkv, qkv, qkv)


def _outmlp_kernel(h_ref, yf_ref, cv_ref, at_ref, mod_ref, g_ref, gf_ref, wo_ref, w1_ref, w2_ref, o_ref, *, final_norm):
    mix = (_bdot(yf_ref[...], wo_ref[:D_FOURIER, :])
           + _bdot(cv_ref[...], wo_ref[D_FOURIER:D_FOURIER + D_CONV, :])
           + _bdot(at_ref[...], wo_ref[D_FOURIER + D_CONV:, :]))
    h1 = h_ref[...] + mod_ref[0, 2:3, :] * mix
    hn = _rms_mod(h1, g_ref[...], mod_ref[0, 3:4, :], mod_ref[0, 4:5, :]).astype(BF16)
    a = jnp.maximum(_bdot(hn, w1_ref[...]), 0.0)
    out = h1 + mod_ref[0, 5:6, :] * _bdot((a * a).astype(BF16), w2_ref[...])
    if final_norm:
        out = out * lax.rsqrt(jnp.mean(out * out, axis=-1, keepdims=True) + RMS_EPS) * gf_ref[...]
    o_ref[...] = out


def _outmlp(h, yf, cv, at, mods, g2, gf, wo, w1, w2, *, layer, seq, tm, ctx, final_norm):
    t, d = h.shape
    nb = t // seq
    per = seq // tm
    row = (lambda i: nb) if ctx else (lambda i: i // per)
    d_ff = w1.shape[2]
    vec = pl.BlockSpec((1, d), lambda i: (0, 0))
    once = dict(pipeline_mode=pl.Buffered(1))
    return pl.pallas_call(
        functools.partial(_outmlp_kernel, final_norm=final_norm),
        grid=(t // tm,),
        in_specs=[pl.BlockSpec((tm, d), lambda i: (i, 0)),
                  pl.BlockSpec((tm, D_FOURIER), lambda i: (i % per, i // per)),
                  pl.BlockSpec((tm, D_CONV), lambda i: (i, 0)),
                  pl.BlockSpec((tm, D_NA), lambda i: (i, 0)),
                  pl.BlockSpec((1, N_MOD, d), lambda i: (row(i), 0, 0)),
                  vec, vec,
                  pl.BlockSpec((None, D_MIX, d), lambda i: (layer, 0, 0), **once),
                  pl.BlockSpec((None, d, d_ff), lambda i: (layer, 0, 0), **once),
                  pl.BlockSpec((None, d_ff, d), lambda i: (layer, 0, 0), **once)],
        out_specs=pl.BlockSpec((tm, d), lambda i: (i, 0)),
        out_shape=jax.ShapeDtypeStruct((t, d), F32),
        compiler_params=_params("parallel"),
        name="outmlp_ctx" if ctx else "outmlp_lat",
    )(h, yf, cv, at, mods, g2, gf, wo, w1, w2)


def kernel(x, c, ctx, c_ctx, ada_w, ada_b, norm1_g, norm2_g, w_in, w_fourier, conv_dw_w, conv_dw_b, conv_norm_g,
           conv_norm_b, conv_pw_w, conv_pw_b, na_rpb, w_out, mlp_w1, mlp_w2, final_norm_g):
    nb, seq, d = x.shape
    n_ctx = ctx.shape[1]
    depth = ada_w.shape[0]
    assert nb < MOD_ROWS and seq % GRID_W == 0 and d == D_MODEL

    cc = jnp.concatenate([c, c_ctx[None], jnp.zeros((MOD_ROWS - nb - 1, d), F32)], axis=0)
    mods = _adaln(cc, ada_w, ada_b).reshape(depth, MOD_ROWS, N_MOD, d)
    ab_lat = _fold_fourier(w_fourier, seq)
    ab_ctx = _fold_fourier(w_fourier, n_ctx)
    dft_lat = _dft_mats(seq)
    dft_ctx = _dft_mats(n_ctx)

    w_in_b = w_in.astype(BF16)
    w_out_b = w_out.astype(BF16)
    w1_b = mlp_w1.astype(BF16)
    w2_b = mlp_w2.astype(BF16)
    pww_b = conv_pw_w.astype(BF16)
    gf = final_norm_g.reshape(1, d)

    h_lat = x.reshape(nb * seq, d)
    h_ctx = ctx.reshape(nb * n_ctx, d)
    ctx_mix = None
    for i in range(depth):
        last = i == depth - 1
        g1 = norm1_g[i].reshape(1, d)
        g2 = norm2_g[i].reshape(1, d)
        conv_p = (conv_dw_w[i], conv_dw_b[i].reshape(1, -1), conv_norm_g[i].reshape(1, -1),
                  conv_norm_b[i].reshape(1, -1), pww_b[i], conv_pw_b[i].reshape(1, -1))
        bias = _na_bias(na_rpb[i])

        zf, uc, qkv = _inproj(h_lat, g1, mods[i], w_in_b, ab_lat[i], layer=i, seq=seq, tm=TM_INPROJ, ctx=False)
        if last and ctx_mix is not None:
            rider = _ctx_tail_rider(h_ctx, *ctx_mix, mods[i - 1], mods[i], norm2_g[i - 1].reshape(1, d), g1,
                                    w_out_b, w1_b, w2_b, w_in_b, layer=i, nb=nb, n_ctx=n_ctx)
            cv, yf, kv_ctx = _convfourier(uc, conv_p, dft_lat, zf, seq=seq, rider=rider)
            kc_off = 0
        elif last:
            kv_ctx = _kvproj(h_ctx, g1, mods[i], w_in_b, layer=i, nb=nb, tm=n_ctx)
            cv, yf = _convfourier(uc, conv_p, dft_lat, zf, seq=seq)
            kc_off = 0
        else:
            if ctx_mix is not None:
                h_ctx = _outmlp(h_ctx, *ctx_mix, mods[i - 1], norm2_g[i - 1].reshape(1, d), gf, w_out_b, w1_b, w2_b,
                                layer=i - 1, seq=n_ctx, tm=n_ctx, ctx=True, final_norm=False)
            rider = _inproj_rider(h_ctx, g1, mods[i], w_in_b, ab_ctx[i], layer=i, nb=nb, n_ctx=n_ctx)
            cv, yf, zf_c, uc_c, qkv_c = _convfourier(uc, conv_p, dft_lat, zf, seq=seq, rider=rider)
            kv_ctx = qkv_c
            kc_off = HEAD_PAIRS

        at = _na(qkv, kv_ctx, bias, seq=seq, n_ctx=n_ctx, kc_off=kc_off)
        h_lat = _outmlp(h_lat, yf, cv, at, mods[i], g2, gf, w_out_b, w1_b, w2_b,
                        layer=i, seq=seq, tm=TM_MLP, ctx=False, final_norm=last)

        if not last:
            cv_c, yf_c = _convfourier(uc_c, conv_p, dft_ctx, zf_c, seq=n_ctx)
            at_c = _ctx_attn(qkv_c, n_ctx=n_ctx)
            ctx_mix = (yf_c, cv_c, at_c)
    return h_lat.reshape(nb, seq, d)
```

```python
import functools
from typing import Callable, NamedTuple

import numpy as np
import jax
import jax.numpy as jnp
from jax import lax
from jax.experimental import pallas as pl
from jax.experimental.pallas import tpu as pltpu

D_MODEL = 1024
GRID_W = 64
D_FOURIER = 256
FOURIER_GROUPS = 4
D_CONV = 256
CONV_WIDTH = 31
N_NA_HEADS = 8
NA_HEAD_DIM = 64
D_NA = N_NA_HEADS * NA_HEAD_DIM
WIN_ROWS = 8
WIN_COLS = 16
QKV_START = D_FOURIER + 2 * D_CONV
KV_START = QKV_START + D_NA
D_MIX = D_FOURIER + D_CONV + D_NA
N_MOD = 6
RMS_EPS = 1e-6
LN_EPS = 1e-5

LANES = 128
SUBLANES = 8
HEAD_PAIRS = N_NA_HEADS * NA_HEAD_DIM // LANES
MOD_ROWS = 16
VMEM_LIMIT = 60 * 1024 * 1024

TM_INPROJ = 1024
TM_MLP = 512
TN_ADALN = 2048

F32 = jnp.float32
BF16 = jnp.bfloat16


def _params(*sem):
    return pltpu.CompilerParams(dimension_semantics=sem, vmem_limit_bytes=VMEM_LIMIT)


def _sigmoid(x):
    return 1.0 / (1.0 + jnp.exp(-x))


def _rms_mod(x, g, shift, scale):
    return x * lax.rsqrt(jnp.mean(x * x, axis=-1, keepdims=True) + RMS_EPS) * (g * (1.0 + scale)) + shift


def _bdot(a, b):
    return jnp.dot(a, b, preferred_element_type=F32)


def _adaln_kernel(c_ref, w_ref, b_ref, o_ref):
    cc = c_ref[...]
    a = cc * _sigmoid(cc)
    w = w_ref[0]
    a_hi = a.astype(BF16)
    a_lo = (a - a_hi.astype(F32)).astype(BF16)
    w_hi = w.astype(BF16)
    w_lo = (w - w_hi.astype(F32)).astype(BF16)
    by_hi = _bdot(jnp.concatenate([a_hi, a_lo], axis=0), w_hi)
    o_ref[0] = by_hi[:MOD_ROWS] + by_hi[MOD_ROWS:] + _bdot(a_hi, w_lo) + b_ref[0]


def _adaln(cc, ada_w, ada_b):
    depth, d, n = ada_w.shape
    tn = TN_ADALN
    return pl.pallas_call(
        _adaln_kernel,
        grid=(depth, n // tn),
        in_specs=[pl.BlockSpec((MOD_ROWS, d), lambda l, j: (0, 0)),
                  pl.BlockSpec((1, d, tn), lambda l, j: (l, 0, j)),
                  pl.BlockSpec((1, 1, tn), lambda l, j: (l, 0, j))],
        out_specs=pl.BlockSpec((1, MOD_ROWS, tn), lambda l, j: (l, 0, j)),
        out_shape=jax.ShapeDtypeStruct((depth, MOD_ROWS, n), F32),
        compiler_params=_params("parallel", "parallel"),
        name="adaln",
    )(cc, ada_w, ada_b.reshape(depth, 1, n))


def _fold_kernel(cc_ref, sc_ref, w_ref, o_ref):
    w = w_ref[0]
    hi = lax.Precision.HIGHEST
    o_ref[0, :, :D_FOURIER] = jnp.dot(cc_ref[...], w, preferred_element_type=F32, precision=hi).astype(BF16)
    o_ref[0, :, D_FOURIER:] = jnp.dot(sc_ref[...], w, preferred_element_type=F32, precision=hi).astype(BF16)


def _fold_fourier(w_fourier, seq):
    depth = w_fourier.shape[0]
    gs = D_FOURIER // FOURIER_GROUPS
    idx = np.arange(D_FOURIER)
    same = (idx[:, None] // gs) == (idx[None, :] // gs)
    ang = 2.0 * np.pi * (((idx[:, None] % gs) * (idx[None, :] % gs)) % gs) / gs
    scale = 1.0 / np.sqrt(seq * gs)
    cc = jnp.asarray(np.where(same, np.cos(ang), 0.0) * scale, F32)
    sc = jnp.asarray(np.where(same, np.sin(ang), 0.0) * scale, F32)
    full = pl.BlockSpec((D_FOURIER, D_FOURIER), lambda l: (0, 0))
    return pl.pallas_call(
        _fold_kernel,
        grid=(depth,),
        in_specs=[full, full, pl.BlockSpec((1, D_FOURIER, D_FOURIER), lambda l: (l, 0, 0))],
        out_specs=pl.BlockSpec((1, D_FOURIER, 2 * D_FOURIER), lambda l: (l, 0, 0)),
        out_shape=jax.ShapeDtypeStruct((depth, D_FOURIER, 2 * D_FOURIER), BF16),
        compiler_params=_params("parallel"),
        name="fold_fourier",
    )(cc, sc, w_fourier)


FLIP_BLOCK = 128


def _dft_mats(seq):
    half = seq // 2
    k = np.arange(half)
    ang = 2.0 * np.pi * ((k[:, None] * k[None, :]) % seq) / seq
    r = np.arange(FLIP_BLOCK)
    perm = (np.arange(2 * FLIP_BLOCK)[None, :] == FLIP_BLOCK - r[:, None]).astype(np.float32)
    alt_col = np.where(k % 2 == 0, 1.0, -1.0).astype(np.float32)[:, None]
    alt_row = np.zeros((SUBLANES, seq), np.float32)
    alt_row[0] = np.where(np.arange(seq) % 2 == 0, 1.0, -1.0)
    return (jnp.asarray(np.cos(ang), F32).astype(BF16), jnp.asarray(np.sin(ang), F32).astype(BF16),
            jnp.asarray(perm, F32).astype(BF16), jnp.asarray(alt_col), jnp.asarray(alt_row, F32).astype(BF16))


def _store_kv(hn, w_ref, out_ref, rows, col):
    kv = _bdot(hn, w_ref[:, KV_START:])
    v = kv[:, D_NA:]
    first = lax.broadcasted_iota(jnp.int32, (1, D_NA), 1) % LANES < NA_HEAD_DIM
    out_ref[rows, col:col + D_NA] = kv[:, :D_NA].astype(BF16)
    out_ref[rows, col + D_NA:col + 2 * D_NA] = jnp.where(first, v, 1.0).astype(BF16)
    out_ref[rows, col + 2 * D_NA:col + 3 * D_NA] = jnp.where(first, 1.0, v).astype(BF16)


INPROJ_SPLIT_ROWS = 512


def _inproj_kernel(x_ref, g_ref, mod_ref, w_ref, ab_ref, zf_ref, uc_ref, qkv_ref):
    tm = x_ref.shape[0]
    part = min(tm, INPROJ_SPLIT_ROWS)
    for r in range(tm // part):
        rows = slice(r * part, (r + 1) * part)
        hn = _rms_mod(x_ref[rows, :], g_ref[...], mod_ref[0, 0:1, :], mod_ref[0, 1:2, :]).astype(BF16)
        uf = _bdot(hn, w_ref[:, :D_FOURIER])
        uc_ref[rows, :] = _bdot(hn, w_ref[:, D_FOURIER:QKV_START])
        qkv_ref[rows, :D_NA] = (_bdot(hn, w_ref[:, QKV_START:KV_START]) * (NA_HEAD_DIM ** -0.5)).astype(BF16)
        _store_kv(hn, w_ref, qkv_ref, rows, D_NA)
        zf_ref[rows, :] = _bdot(uf.astype(BF16), ab_ref[...]).astype(BF16)


def _inproj(h, g, mods, w_in, ab, *, layer, seq, tm, ctx):
    t, d = h.shape
    nb = t // seq
    per = seq // tm
    row = (lambda i: nb) if ctx else (lambda i: i // per)
    d_in = w_in.shape[2]
    return pl.pallas_call(
        _inproj_kernel,
        grid=(t // tm,),
        in_specs=[pl.BlockSpec((tm, d), lambda i: (i, 0)),
                  pl.BlockSpec((1, d), lambda i: (0, 0)),
                  pl.BlockSpec((1, N_MOD, d), lambda i: (row(i), 0, 0)),
                  pl.BlockSpec((None, d, d_in), lambda i: (layer, 0, 0), pipeline_mode=pl.Buffered(1)),
                  pl.BlockSpec((D_FOURIER, 2 * D_FOURIER), lambda i: (0, 0))],
        out_specs=[pl.BlockSpec((tm, 2 * D_FOURIER), lambda i: (i % per, i // per)),
                   pl.BlockSpec((tm, 2 * D_CONV), lambda i: (i, 0)),
                   pl.BlockSpec((tm, 4 * D_NA), lambda i: (i, 0))],
        out_shape=[jax.ShapeDtypeStruct((seq, nb * 2 * D_FOURIER), BF16),
                   jax.ShapeDtypeStruct((t, 2 * D_CONV), F32),
                   jax.ShapeDtypeStruct((t, 4 * D_NA), BF16)],
        compiler_params=_params("parallel"),
        name="inproj_ctx" if ctx else "inproj_lat",
    )(h, g, mods, w_in, ab)


def _kvproj_kernel(x_ref, g_ref, mod_ref, w_ref, kv_ref):
    hn = _rms_mod(x_ref[...], g_ref[...], mod_ref[0, 0:1, :], mod_ref[0, 1:2, :]).astype(BF16)
    _store_kv(hn, w_ref, kv_ref, slice(None), 0)


def _kvproj(h, g, mods, w_in, *, layer, nb, tm):
    t, d = h.shape
    d_in = w_in.shape[2]
    n = 3 * D_NA
    return pl.pallas_call(
        _kvproj_kernel,
        grid=(t // tm,),
        in_specs=[pl.BlockSpec((tm, d), lambda i: (i, 0)),
                  pl.BlockSpec((1, d), lambda i: (0, 0)),
                  pl.BlockSpec((1, N_MOD, d), lambda i: (nb, 0, 0)),
                  pl.BlockSpec((None, d, d_in), lambda i: (layer, 0, 0))],
        out_specs=pl.BlockSpec((tm, n), lambda i: (i, 0)),
        out_shape=jax.ShapeDtypeStruct((t, n), BF16),
        compiler_params=_params("parallel"),
        name="kvproj_ctx",
    )(h, g, mods, w_in)


CONV_PAD = 16
CONV_CHUNK = 512
CONV_TAP_GROUPS = 4
CONV_SPAN = 512


N_CONVFOURIER_IN = 13


def _convfourier_kernel(*refs, rider):
    (u_ref, dww_ref, dwb_ref, lng_ref, lnb_ref, pww_ref, pwb_ref,
     c_ref, s_ref, perm_ref, altc_ref, altr_ref, z_ref) = refs[:N_CONVFOURIER_IN]
    n_in = N_CONVFOURIER_IN + (rider.n_in if rider else 0)
    n_out = 2 + (rider.n_out if rider else 0)
    o_ref, y_ref = refs[n_in:n_in + 2]
    vs_ref, ze_ref, yr_ref = refs[n_in + n_out:n_in + n_out + 3]
    seq = u_ref.shape[0]
    half = seq // 2
    fb = FLIP_BLOCK
    nblk = half // fb
    first = CONV_PAD - CONV_WIDTH // 2
    rows = min(CONV_CHUNK, seq)
    span = min(CONV_SPAN, seq)
    n_pad = span + 2 * CONV_PAD
    n_chunks = seq // rows
    assert (first + CONV_WIDTH - 1) // SUBLANES * SUBLANES + span <= n_pad - SUBLANES

    def fill_span(lo):
        a, b = max(lo - CONV_PAD, 0), min(lo + span + CONV_PAD, seq)
        zeros = jnp.zeros((CONV_PAD, D_CONV), F32)
        if lo == 0:
            vs_ref[0, 0:CONV_PAD, :] = zeros
        if lo + span == seq:
            vs_ref[0, span + CONV_PAD:n_pad, :] = zeros
        dst = a - (lo - CONV_PAD)
        vs_ref[0, dst:dst + b - a, :] = u_ref[a:b, :D_CONV] * _sigmoid(u_ref[a:b, D_CONV:])
        for s in range(1, SUBLANES):
            vs_ref[s, 0:n_pad - SUBLANES, :] = vs_ref[0, s:s + n_pad - SUBLANES, :]

    def conv_chunk(lo, base, emit):
        acc = jnp.zeros((rows, D_CONV), F32) + dwb_ref[...]
        for k in range(CONV_TAP_GROUPS):
            emit()
            for t in range(k * CONV_WIDTH // CONV_TAP_GROUPS, (k + 1) * CONV_WIDTH // CONV_TAP_GROUPS):
                s, a = (first + t) % SUBLANES, (first + t) // SUBLANES
                acc = acc + vs_ref[s, base + a * SUBLANES:base + a * SUBLANES + rows, :] * dww_ref[t:t + 1, :]
        mu = jnp.mean(acc, axis=-1, keepdims=True)
        cen = acc - mu
        var = jnp.mean(cen * cen, axis=-1, keepdims=True)
        y = cen * lax.rsqrt(var + LN_EPS) * lng_ref[...] + lnb_ref[...]
        y = y * _sigmoid(y)
        o_ref[lo + base:lo + base + rows, :] = (_bdot(y.astype(BF16), pww_ref[...]) + pwb_ref[...]).astype(BF16)

    def reversed_block(ref, i, n):
        if i == 0:
            return _bdot(perm_ref[:, :fb], ref[n - fb:n, :])
        lo = n - (i + 1) * fb
        return _bdot(perm_ref[...], ref[lo:lo + 2 * fb, :])

    def fold_in():
        for i in range(nblk):
            rows_i = slice(i * fb, (i + 1) * fb)
            zr = reversed_block(z_ref, i, seq)
            ze_ref[rows_i, :D_FOURIER] = (z_ref[rows_i, :D_FOURIER].astype(F32) + zr[:, :D_FOURIER]).astype(BF16)
            ze_ref[rows_i, D_FOURIER:] = (z_ref[rows_i, D_FOURIER:].astype(F32) - zr[:, D_FOURIER:]).astype(BF16)

    def dft(part, n_parts):
        r = slice(part * half // n_parts, (part + 1) * half // n_parts)
        nyq = z_ref[half:half + 1, :D_FOURIER].astype(F32)
        yc = _bdot(c_ref[r, :], ze_ref[:, :D_FOURIER]) + altc_ref[r, :] * nyq
        ys = _bdot(s_ref[r, :], ze_ref[:, D_FOURIER:])
        y_ref[r, :] = (yc - ys).astype(BF16)
        yr_ref[r, :] = (yc + ys).astype(BF16)

    def fold_out():
        y_mid = _bdot(altr_ref[...], z_ref[:, :D_FOURIER])[0:1, :]
        first_row = lax.broadcasted_iota(jnp.int32, (fb, 1), 0) == 0
        for i in range(nblk):
            blk = reversed_block(yr_ref, i, half)
            if i == 0:
                blk = jnp.where(first_row, y_mid, blk)
            y_ref[half + i * fb:half + (i + 1) * fb, :] = blk.astype(BF16)

    lists = [[fold_in, functools.partial(dft, 0, 2), functools.partial(dft, 1, 2), fold_out]]
    if rider:
        lists.append(rider.phases(refs[N_CONVFOURIER_IN:n_in], refs[n_in + 2:n_in + n_out], refs[n_in + n_out + 3:]))
    tagged = sorted(((k + 0.5) / len(lst), n, k, piece) for n, lst in enumerate(lists) for k, piece in enumerate(lst))
    pieces = [piece for _, _, _, piece in tagged]
    n_slots = n_chunks * CONV_TAP_GROUPS
    cursor = [0, 0]

    def emit():
        cursor[1] += 1
        while cursor[0] < len(pieces) and cursor[0] * n_slots < cursor[1] * len(pieces):
            pieces[cursor[0]]()
            cursor[0] += 1

    for ci in range(n_chunks):
        lo = ci * rows // span * span
        if ci * rows == lo:
            fill_span(lo)
        conv_chunk(lo, ci * rows - lo, emit)
    assert cursor[0] == len(pieces)


class Rider(NamedTuple):
    phases: Callable
    inputs: tuple
    in_specs: tuple
    out_shapes: tuple
    out_specs: tuple
    scratch: tuple

    @property
    def n_in(self):
        return len(self.inputs)

    @property
    def n_out(self):
        return len(self.out_shapes)


def _convfourier(uc, conv_p, mats, zf, *, seq, rider=None):
    t = uc.shape[0]
    cmat, smat, perm, alt_col, alt_row = mats
    half = seq // 2
    vec = pl.BlockSpec((1, D_CONV), lambda b: (0, 0))
    const = lambda a: pl.BlockSpec(a.shape, lambda b: (0, 0), pipeline_mode=pl.Buffered(1))
    in_specs = [pl.BlockSpec((seq, 2 * D_CONV), lambda b: (b, 0)),
                pl.BlockSpec((CONV_WIDTH, D_CONV), lambda b: (0, 0)),
                vec, vec, vec,
                pl.BlockSpec((D_CONV, D_CONV), lambda b: (0, 0)),
                vec,
                const(cmat), const(smat), const(perm), const(alt_col), const(alt_row),
                pl.BlockSpec((seq, 2 * D_FOURIER), lambda b: (0, b))]
    assert len(in_specs) == N_CONVFOURIER_IN
    out_specs = [pl.BlockSpec((seq, D_CONV), lambda b: (b, 0)),
                 pl.BlockSpec((seq, D_FOURIER), lambda b: (0, b))]
    out_shape = [jax.ShapeDtypeStruct((t, D_CONV), BF16),
                 jax.ShapeDtypeStruct((seq, t // seq * D_FOURIER), BF16)]
    scratch = [pltpu.VMEM((SUBLANES, min(CONV_SPAN, seq) + 2 * CONV_PAD, D_CONV), F32),
               pltpu.VMEM((half, 2 * D_FOURIER), BF16), pltpu.VMEM((half, D_FOURIER), BF16)]
    operands = [uc, *conv_p, cmat, smat, perm, alt_col, alt_row, zf]
    if rider:
        in_specs += rider.in_specs
        out_specs += rider.out_specs
        out_shape += rider.out_shapes
        scratch += rider.scratch
        operands += rider.inputs
    return pl.pallas_call(
        functools.partial(_convfourier_kernel, rider=rider),
        grid=(t // seq,),
        in_specs=in_specs,
        out_specs=out_specs,
        out_shape=out_shape,
        scratch_shapes=scratch,
        compiler_params=_params("parallel"),
        name="conv_fourier",
    )(*operands)


def _inproj_rider(h_ctx, g1, mods, w_in, ab, *, layer, nb, n_ctx):
    d = h_ctx.shape[1]
    d_in = w_in.shape[2]
    t = h_ctx.shape[0]

    def phases(in_refs, out_refs, scratch_refs):
        x_ref, g_ref, mod_ref, w_ref, ab_ref = in_refs
        zf_ref, uc_ref, qkv_ref = out_refs
        hn_ref, uf_ref = scratch_refs
        first = lax.broadcasted_iota(jnp.int32, (1, D_NA), 1) % LANES < NA_HEAD_DIM

        def norm():
            hn_ref[...] = _rms_mod(x_ref[...], g_ref[...], mod_ref[0, 0:1, :], mod_ref[0, 1:2, :]).astype(BF16)

        def fourier_in():
            uf_ref[...] = _bdot(hn_ref[...], w_ref[:, :D_FOURIER]).astype(BF16)

        def conv_in():
            uc_ref[...] = _bdot(hn_ref[...], w_ref[:, D_FOURIER:QKV_START])

        def queries():
            qkv_ref[:, :D_NA] = (_bdot(hn_ref[...], w_ref[:, QKV_START:KV_START]) * (NA_HEAD_DIM ** -0.5)).astype(BF16)

        def keys():
            qkv_ref[:, D_NA:2 * D_NA] = _bdot(hn_ref[...], w_ref[:, KV_START:KV_START + D_NA]).astype(BF16)

        def vals():
            v = _bdot(hn_ref[...], w_ref[:, KV_START + D_NA:])
            qkv_ref[:, 2 * D_NA:3 * D_NA] = jnp.where(first, v, 1.0).astype(BF16)
            qkv_ref[:, 3 * D_NA:] = jnp.where(first, 1.0, v).astype(BF16)

        def fold():
            zf_ref[...] = _bdot(uf_ref[...], ab_ref[...]).astype(BF16)

        return [norm, fourier_in, conv_in, queries, keys, vals, fold]

    return Rider(
        phases=phases,
        inputs=(h_ctx, g1, mods, w_in, ab),
        in_specs=(pl.BlockSpec((n_ctx, d), lambda b: (b, 0)),
                  pl.BlockSpec((1, d), lambda b: (0, 0)),
                  pl.BlockSpec((1, N_MOD, d), lambda b: (nb, 0, 0)),
                  pl.BlockSpec((None, d, d_in), lambda b: (layer, 0, 0), pipeline_mode=pl.Buffered(1)),
                  pl.BlockSpec((D_FOURIER, 2 * D_FOURIER), lambda b: (0, 0))),
        out_shapes=(jax.ShapeDtypeStruct((n_ctx, nb * 2 * D_FOURIER), BF16),
                    jax.ShapeDtypeStruct((t, 2 * D_CONV), F32),
                    jax.ShapeDtypeStruct((t, 4 * D_NA), BF16)),
        out_specs=(pl.BlockSpec((n_ctx, 2 * D_FOURIER), lambda b: (0, b)),
                   pl.BlockSpec((n_ctx, 2 * D_CONV), lambda b: (b, 0)),
                   pl.BlockSpec((n_ctx, 4 * D_NA), lambda b: (b, 0))),
        scratch=(pltpu.VMEM((n_ctx, d), BF16), pltpu.VMEM((n_ctx, D_FOURIER), BF16)))


TAIL_SPLIT = 4


def _ctx_tail_rider(h_ctx, yf_c, cv_c, at_c, mods_prev, mods_cur, g2_prev, g1_cur, wo, w1, w2, w_in, *,
                    layer, nb, n_ctx):
    d = h_ctx.shape[1]
    d_ff = w1.shape[2]
    d_in = w_in.shape[2]
    t = h_ctx.shape[0]

    def phases(in_refs, out_refs, scratch_refs):
        (h_ref, yf_ref, cv_ref, at_ref, modp_ref, modc_ref, g2_ref, g1_ref, wo_ref, w1_ref, w2_ref, w_ref) = in_refs
        (kv_ref,) = out_refs
        h1_ref, hn_ref, a_ref = scratch_refs

        def mix():
            m = (_bdot(yf_ref[...], wo_ref[:D_FOURIER, :])
                 + _bdot(cv_ref[...], wo_ref[D_FOURIER:D_FOURIER + D_CONV, :])
                 + _bdot(at_ref[...], wo_ref[D_FOURIER + D_CONV:, :]))
            h1 = h_ref[...] + modp_ref[0, 2:3, :] * m
            h1_ref[...] = h1
            hn_ref[...] = _rms_mod(h1, g2_ref[...], modp_ref[0, 3:4, :], modp_ref[0, 4:5, :]).astype(BF16)

        def up(k):
            cols = slice(k * d_ff // TAIL_SPLIT, (k + 1) * d_ff // TAIL_SPLIT)
            a = jnp.maximum(_bdot(hn_ref[...], w1_ref[:, cols]), 0.0)
            a_ref[:, cols] = (a * a).astype(BF16)

        def down(k):
            cols = slice(k * d // TAIL_SPLIT, (k + 1) * d // TAIL_SPLIT)
            h1_ref[:, cols] = h1_ref[:, cols] + modp_ref[0, 5:6, cols] * _bdot(a_ref[...], w2_ref[:, cols])

        def norm():
            hn_ref[...] = _rms_mod(h1_ref[...], g1_ref[...], modc_ref[0, 0:1, :], modc_ref[0, 1:2, :]).astype(BF16)

        def keys():
            kv_ref[:, :D_NA] = _bdot(hn_ref[...], w_ref[:, KV_START:KV_START + D_NA]).astype(BF16)

        def vals():
            first = lax.broadcasted_iota(jnp.int32, (1, D_NA), 1) % LANES < NA_HEAD_DIM
            v = _bdot(hn_ref[...], w_ref[:, KV_START + D_NA:])
            kv_ref[:, D_NA:2 * D_NA] = jnp.where(first, v, 1.0).astype(BF16)
            kv_ref[:, 2 * D_NA:] = jnp.where(first, 1.0, v).astype(BF16)

        return ([mix] + [functools.partial(up, k) for k in range(TAIL_SPLIT)]
                + [functools.partial(down, k) for k in range(TAIL_SPLIT)] + [norm, keys, vals])

    row = lambda b: (b, 0)
    vec = pl.BlockSpec((1, d), lambda b: (0, 0))
    mod = pl.BlockSpec((1, N_MOD, d), lambda b: (nb, 0, 0))
    once = dict(pipeline_mode=pl.Buffered(1))
    return Rider(
        phases=phases,
        inputs=(h_ctx, yf_c, cv_c, at_c, mods_prev, mods_cur, g2_prev, g1_cur, wo, w1, w2, w_in),
        in_specs=(pl.BlockSpec((n_ctx, d), row),
                  pl.BlockSpec((n_ctx, D_FOURIER), lambda b: (0, b)),
                  pl.BlockSpec((n_ctx, D_CONV), row),
                  pl.BlockSpec((n_ctx, D_NA), row),
                  mod, mod, vec, vec,
                  pl.BlockSpec((None, D_MIX, d), lambda b: (layer - 1, 0, 0), **once),
                  pl.BlockSpec((None, d, d_ff), lambda b: (layer - 1, 0, 0), **once),
                  pl.BlockSpec((None, d_ff, d), lambda b: (layer - 1, 0, 0), **once),
                  pl.BlockSpec((None, d, d_in), lambda b: (layer, 0, 0), **once)),
        out_shapes=(jax.ShapeDtypeStruct((t, 3 * D_NA), BF16),),
        out_specs=(pl.BlockSpec((n_ctx, 3 * D_NA), row),),
        scratch=(pltpu.VMEM((n_ctx, d), F32), pltpu.VMEM((n_ctx, d), BF16), pltpu.VMEM((n_ctx, d_ff), BF16)))


def _head_mask():
    return lax.broadcasted_iota(jnp.int32, (1, LANES), 1) < NA_HEAD_DIM


Q_ROWS = 2
BAND_ROWS = 10
assert BAND_ROWS >= WIN_ROWS + Q_ROWS - 1 and (BAND_ROWS * GRID_W) % LANES == 0


def _na_patterns(rows):
    starts, sigs = [], []
    for g in range(rows // Q_ROWS):
        start = int(np.clip(Q_ROWS * g - WIN_ROWS // 2, 0, rows - BAND_ROWS))
        r = Q_ROWS * g + np.arange(Q_ROWS)
        rs = np.clip(r - WIN_ROWS // 2, 0, rows - WIN_ROWS)
        starts.append(start)
        sigs.append((start - Q_ROWS * g,) + tuple(rs - r))
    run_starts = [g for g in range(len(sigs)) if g == 0 or sigs[g] != sigs[g - 1]]
    assert len(set(sigs)) == len(run_starts)
    return starts, run_starts


def _na_kernel(q_ref, k_ref, va_ref, vb_ref, kc_ref, vca_ref, vcb_ref, rpb_ref, o_ref, s_ref, p_ref, bias_ref, *,
               seq, n_ctx):
    rows = seq // GRID_W
    per_batch = rows // Q_ROWS
    n_groups = q_ref.shape[0] // seq * per_batch
    n_q = Q_ROWS * GRID_W
    n_loc = BAND_ROWS * GRID_W
    _, run_starts = _na_patterns(rows)
    first = _head_mask()
    dn = (((1,), (1,)), ((), ()))
    assert n_groups >= 4 and seq == per_batch * n_q

    def split(t):
        t = jnp.asarray(t, jnp.int32)
        return t // per_batch, t % per_batch

    def band(t):
        b, g = split(t)
        start = jnp.clip(Q_ROWS * g - WIN_ROWS // 2, 0, rows - BAND_ROWS)
        return pl.multiple_of(b * seq + start * GRID_W, LANES)

    def q_start(t):
        return pl.multiple_of(jnp.asarray(t, jnp.int32) * n_q, n_q)

    def ctx_start(t):
        return pl.multiple_of(split(t)[0] * n_ctx, n_ctx)

    def scores(t, slot):
        g = split(t)[1]
        pat = sum((g >= s).astype(jnp.int32) for s in run_starts[1:])
        q = q_ref[pl.ds(q_start(t), n_q), :]
        kb = k_ref[pl.ds(band(t), n_loc), :]
        kc = kc_ref[pl.ds(ctx_start(t), n_ctx), :]
        for a in range(2):
            qa = jnp.where(first if a == 0 else ~first, q, jnp.zeros_like(q))
            s_ref[slot, a, :, :n_loc] = lax.dot_general(qa, kb, dn, preferred_element_type=F32) + bias_ref[a, pat]
            s_ref[slot, a, :, n_loc:] = lax.dot_general(qa, kc, dn, preferred_element_type=F32)

    def softmax(slot):
        for a in range(2):
            s = s_ref[slot, a]
            p_ref[slot, a] = jnp.exp((s - jnp.max(s, axis=-1, keepdims=True)).astype(BF16))

    def values(t, slot):
        outs = []
        for a, (v_ref, vc_ref) in enumerate(((va_ref, vca_ref), (vb_ref, vcb_ref))):
            o = (_bdot(p_ref[slot, a, :, :n_loc], v_ref[pl.ds(band(t), n_loc), :])
                 + _bdot(p_ref[slot, a, :, n_loc:], vc_ref[pl.ds(ctx_start(t), n_ctx), :]))
            outs.append(o * (1.0 / pltpu.roll(o, NA_HEAD_DIM, axis=1)))
        o_ref[pl.ds(q_start(t), n_q), :] = jnp.where(first, outs[0], outs[1]).astype(BF16)

    def tick(t, parity):
        values(t - 1, 1 - parity)
        scores(t + 1, 1 - parity)
        softmax(parity)

    def quad(j, carry):
        t = 4 * j + 1
        tick(t, 1)
        tick(t + 1, 0)
        tick(t + 2, 1)
        tick(t + 3, 0)
        return carry

    @pl.when(pl.program_id(1) == 0)
    def _():
        _na_assemble_bias(rpb_ref, bias_ref, rows)

    scores(0, 0)
    scores(1, 1)
    softmax(0)
    n_quads = (n_groups - 2) // 4
    lax.fori_loop(0, n_quads, quad, 0)
    for t in range(4 * n_quads + 1, n_groups - 1):
        tick(t, t % 2)
    softmax((n_groups - 1) % 2)
    values(n_groups - 2, n_groups % 2)
    values(n_groups - 1, (n_groups - 1) % 2)


def _na_bias(rpb):
    n_heads, n_dr, n_dc = rpb.shape
    r = rpb.astype(F32)
    w = jnp.concatenate([r[..., WIN_COLS - 1:], jnp.zeros((n_heads, n_dr, LANES - n_dc), F32), r[..., :WIN_COLS - 1]],
                        axis=-1)
    w = jnp.pad(w, ((0, 0), (0, 1), (0, 0)))
    return w.reshape(HEAD_PAIRS, 2, n_dr + 1, LANES)


def _na_assemble_bias(w_ref, bias_ref, rows):
    starts, run_starts = _na_patterns(rows)
    cq = lax.broadcasted_iota(jnp.int32, (GRID_W, LANES), 0)
    lane = lax.broadcasted_iota(jnp.int32, (GRID_W, LANES), 1)
    low = lane < GRID_W
    ck = jnp.where(low, lane, lane - GRID_W)
    col_start = jnp.clip(cq - WIN_COLS // 2, 0, GRID_W - WIN_COLS)
    in_window = (ck >= col_start) & (ck < col_start + WIN_COLS)
    masked = jnp.full((GRID_W, LANES), -jnp.inf, F32)
    rolled = {}

    def half(a, d, upper):
        if (a, d, upper) not in rolled:
            row = jnp.broadcast_to(w_ref[0, a, d:d + 1, :], (GRID_W, LANES))
            rolled[a, d, upper] = pltpu.roll(row, GRID_W if upper else 0, axis=1, stride=1, stride_axis=0)
        return rolled[a, d, upper]

    for a in range(2):
        for p, g in enumerate(run_starts):
            for qi in range(Q_ROWS):
                r = Q_ROWS * g + qi
                rs = int(np.clip(r - WIN_ROWS // 2, 0, rows - WIN_ROWS))
                below = rs - starts[g]
                dr0 = rs - r + WIN_ROWS - 1
                for t in range(BAND_ROWS // 2):
                    d0 = dr0 + 2 * t - below
                    v0 = below <= 2 * t < below + WIN_ROWS
                    v1 = below <= 2 * t + 1 < below + WIN_ROWS
                    if v0 and v1:
                        tile = jnp.where(low, half(a, d0, False), half(a, d0 + 1, True))
                    elif v0:
                        tile = jnp.where(low, half(a, d0, False), masked)
                    elif v1:
                        tile = jnp.where(low, masked, half(a, d0 + 1, True))
                    else:
                        tile = masked
                    if v0 or v1:
                        tile = jnp.where(in_window, tile, masked)
                    bias_ref[a, p, qi * GRID_W:(qi + 1) * GRID_W, t * LANES:(t + 1) * LANES] = tile


NA_BATCHES = 4


def _na(qkv, kv_ctx, rpb_rows, *, seq, n_ctx, kc_off):
    t = qkv.shape[0]
    hp = HEAD_PAIRS
    tb, tbc = NA_BATCHES * seq, NA_BATCHES * n_ctx
    n_pat = len(_na_patterns(seq // GRID_W)[1])
    n_keys = BAND_ROWS * GRID_W + n_ctx
    assert t % tb == 0
    return pl.pallas_call(
        functools.partial(_na_kernel, seq=seq, n_ctx=n_ctx),
        grid=(hp, t // tb),
        in_specs=[pl.BlockSpec((tb, LANES), lambda h, b: (b, h)),
                  pl.BlockSpec((tb, LANES), lambda h, b: (b, hp + h)),
                  pl.BlockSpec((tb, LANES), lambda h, b: (b, 2 * hp + h)),
                  pl.BlockSpec((tb, LANES), lambda h, b: (b, 3 * hp + h)),
                  pl.BlockSpec((tbc, LANES), lambda h, b: (b, kc_off + h)),
                  pl.BlockSpec((tbc, LANES), lambda h, b: (b, kc_off + hp + h)),
                  pl.BlockSpec((tbc, LANES), lambda h, b: (b, kc_off + 2 * hp + h)),
                  pl.BlockSpec((1,) + rpb_rows.shape[1:], lambda h, b: (h, 0, 0, 0))],
        out_specs=pl.BlockSpec((tb, LANES), lambda h, b: (b, h)),
        out_shape=jax.ShapeDtypeStruct((t, D_NA), BF16),
        scratch_shapes=[pltpu.VMEM((2, 2, Q_ROWS * GRID_W, n_keys), F32),
                        pltpu.VMEM((2, 2, Q_ROWS * GRID_W, n_keys), BF16),
                        pltpu.VMEM((2, n_pat, Q_ROWS * GRID_W, BAND_ROWS * GRID_W), F32)],
        compiler_params=_params("parallel", "arbitrary"),
        name="na_attention",
    )(qkv, qkv, qkv, qkv, kv_ctx, kv_ctx, kv_ctx, rpb_rows)


def _ctx_attn_kernel(q_ref, k_ref, va_ref, vb_ref, o_ref):
    first = _head_mask()
    for h in range(HEAD_PAIRS):
        lanes = slice(h * LANES, (h + 1) * LANES)
        q = q_ref[:, lanes]
        k = k_ref[:, lanes]
        outs = []
        for a, v_ref in enumerate((va_ref, vb_ref)):
            qa = jnp.where(first if a == 0 else ~first, q, jnp.zeros_like(q))
            s = lax.dot_general(qa, k, (((1,), (1,)), ((), ())), preferred_element_type=F32)
            p = jnp.exp(s - jnp.max(s, axis=-1, keepdims=True))
            den = jnp.sum(p, axis=-1, keepdims=True)
            outs.append(_bdot(p.astype(BF16), v_ref[:, lanes]) * (1.0 / den))
        o_ref[:, lanes] = jnp.where(first, outs[0], outs[1]).astype(BF16)


def _ctx_attn(qkv, *, n_ctx):
    t = qkv.shape[0]
    return pl.pallas_call(
        _ctx_attn_kernel,
        grid=(t // n_ctx,),
        in_specs=[pl.BlockSpec((n_ctx, D_NA), lambda b: (b, 0)),
                  pl.BlockSpec((n_ctx, D_NA), lambda b: (b, 1)),
                  pl.BlockSpec((n_ctx, D_NA), lambda b: (b, 2)),
                  pl.BlockSpec((n_ctx, D_NA), lambda b: (b, 3))],
        out_specs=pl.BlockSpec((n_ctx, D_NA), lambda b: (b, 0)),
        out_shape=jax.ShapeDtypeStruct((t, D_NA), BF16),
        compiler_params=_params("parallel"),
        name="ctx_attention",
    )(qkv, qkv, qkv, qkv)


def _outmlp_kernel(h_ref, yf_ref, cv_ref, at_ref, mod_ref, g_ref, gf_ref, wo_ref, w1_ref, w2_ref, o_ref, *, final_norm):
    mix = (_bdot(yf_ref[...], wo_ref[:D_FOURIER, :])
           + _bdot(cv_ref[...], wo_ref[D_FOURIER:D_FOURIER + D_CONV, :])
           + _bdot(at_ref[...], wo_ref[D_FOURIER + D_CONV:, :]))
    h1 = h_ref[...] + mod_ref[0, 2:3, :] * mix
    hn = _rms_mod(h1, g_ref[...], mod_ref[0, 3:4, :], mod_ref[0, 4:5, :]).astype(BF16)
    a = jnp.maximum(_bdot(hn, w1_ref[...]), 0.0)
    out = h1 + mod_ref[0, 5:6, :] * _bdot((a * a).astype(BF16), w2_ref[...])
    if final_norm:
        out = out * lax.rsqrt(jnp.mean(out * out, axis=-1, keepdims=True) + RMS_EPS) * gf_ref[...]
    o_ref[...] = out


def _outmlp(h, yf, cv, at, mods, g2, gf, wo, w1, w2, *, layer, seq, tm, ctx, final_norm):
    t, d = h.shape
    nb = t // seq
    per = seq // tm
    row = (lambda i: nb) if ctx else (lambda i: i // per)
    d_ff = w1.shape[2]
    vec = pl.BlockSpec((1, d), lambda i: (0, 0))
    once = dict(pipeline_mode=pl.Buffered(1))
    return pl.pallas_call(
        functools.partial(_outmlp_kernel, final_norm=final_norm),
        grid=(t // tm,),
        in_specs=[pl.BlockSpec((tm, d), lambda i: (i, 0)),
                  pl.BlockSpec((tm, D_FOURIER), lambda i: (i % per, i // per)),
                  pl.BlockSpec((tm, D_CONV), lambda i: (i, 0)),
                  pl.BlockSpec((tm, D_NA), lambda i: (i, 0)),
                  pl.BlockSpec((1, N_MOD, d), lambda i: (row(i), 0, 0)),
                  vec, vec,
                  pl.BlockSpec((None, D_MIX, d), lambda i: (layer, 0, 0), **once),
                  pl.BlockSpec((None, d, d_ff), lambda i: (layer, 0, 0), **once),
                  pl.BlockSpec((None, d_ff, d), lambda i: (layer, 0, 0), **once)],
        out_specs=pl.BlockSpec((tm, d), lambda i: (i, 0)),
        out_shape=jax.ShapeDtypeStruct((t, d), F32),
        compiler_params=_params("parallel"),
        name="outmlp_ctx" if ctx else "outmlp_lat",
    )(h, yf, cv, at, mods, g2, gf, wo, w1, w2)


def kernel(x, c, ctx, c_ctx, ada_w, ada_b, norm1_g, norm2_g, w_in, w_fourier, conv_dw_w, conv_dw_b, conv_norm_g,
           conv_norm_b, conv_pw_w, conv_pw_b, na_rpb, w_out, mlp_w1, mlp_w2, final_norm_g):
    nb, seq, d = x.shape
    n_ctx = ctx.shape[1]
    depth = ada_w.shape[0]
    assert nb < MOD_ROWS and seq % GRID_W == 0 and d == D_MODEL

    cc = jnp.concatenate([c, c_ctx[None], jnp.zeros((MOD_ROWS - nb - 1, d), F32)], axis=0)
    mods = _adaln(cc, ada_w, ada_b).reshape(depth, MOD_ROWS, N_MOD, d)
    ab_lat = _fold_fourier(w_fourier, seq)
    ab_ctx = _fold_fourier(w_fourier, n_ctx)
    dft_lat = _dft_mats(seq)
    dft_ctx = _dft_mats(n_ctx)

    w_in_b = w_in.astype(BF16)
    w_out_b = w_out.astype(BF16)
    w1_b = mlp_w1.astype(BF16)
    w2_b = mlp_w2.astype(BF16)
    pww_b = conv_pw_w.astype(BF16)
    gf = final_norm_g.reshape(1, d)

    h_lat = x.reshape(nb * seq, d)
    h_ctx = ctx.reshape(nb * n_ctx, d)
    ctx_mix = None
    for i in range(depth):
        last = i == depth - 1
        g1 = norm1_g[i].reshape(1, d)
        g2 = norm2_g[i].reshape(1, d)
        conv_p = (conv_dw_w[i], conv_dw_b[i].reshape(1, -1), conv_norm_g[i].reshape(1, -1),
                  conv_norm_b[i].reshape(1, -1), pww_b[i], conv_pw_b[i].reshape(1, -1))
        bias = _na_bias(na_rpb[i])

        zf, uc, qkv = _inproj(h_lat, g1, mods[i], w_in_b, ab_lat[i], layer=i, seq=seq, tm=TM_INPROJ, ctx=False)
        if last and ctx_mix is not None:
            rider = _ctx_tail_rider(h_ctx, *ctx_mix, mods[i - 1], mods[i], norm2_g[i - 1].reshape(1, d), g1,
                                    w_out_b, w1_b, w2_b, w_in_b, layer=i, nb=nb, n_ctx=n_ctx)
            cv, yf, kv_ctx = _convfourier(uc, conv_p, dft_lat, zf, seq=seq, rider=rider)
            kc_off = 0
        elif last:
            kv_ctx = _kvproj(h_ctx, g1, mods[i], w_in_b, layer=i, nb=nb, tm=n_ctx)
            cv, yf = _convfourier(uc, conv_p, dft_lat, zf, seq=seq)
            kc_off = 0
        else:
            if ctx_mix is not None:
                h_ctx = _outmlp(h_ctx, *ctx_mix, mods[i - 1], norm2_g[i - 1].reshape(1, d), gf, w_out_b, w1_b, w2_b,
                                layer=i - 1, seq=n_ctx, tm=n_ctx, ctx=True, final_norm=False)
            rider = _inproj_rider(h_ctx, g1, mods[i], w_in_b, ab_ctx[i], layer=i, nb=nb, n_ctx=n_ctx)
            cv, yf, zf_c, uc_c, qkv_c = _convfourier(uc, conv_p, dft_lat, zf, seq=seq, rider=rider)
            kv_ctx = qkv_c
            kc_off = HEAD_PAIRS

        at = _na(qkv, kv_ctx, bias, seq=seq, n_ctx=n_ctx, kc_off=kc_off)
        h_lat = _outmlp(h_lat, yf, cv, at, mods[i], g2, gf, w_out_b, w1_b, w2_b,
                        layer=i, seq=seq, tm=TM_MLP, ctx=False, final_norm=last)

        if not last:
            cv_c, yf_c = _convfourier(uc_c, conv_p, dft_ctx, zf_c, seq=n_ctx)
            at_c = _ctx_attn(qkv_c, n_ctx=n_ctx)
            ctx_mix = (yf_c, cv_c, at_c)
    return h_lat.reshape(nb, seq, d)
```

```python
import functools
from typing import Callable, NamedTuple

import numpy as np
import jax
import jax.numpy as jnp
from jax import lax
from jax.experimental import pallas as pl
from jax.experimental.pallas import tpu as pltpu

D_MODEL = 1024
GRID_W = 64
D_FOURIER = 256
FOURIER_GROUPS = 4
D_CONV = 256
CONV_WIDTH = 31
N_NA_HEADS = 8
NA_HEAD_DIM = 64
D_NA = N_NA_HEADS * NA_HEAD_DIM
WIN_ROWS = 8
WIN_COLS = 16
QKV_START = D_FOURIER + 2 * D_CONV
KV_START = QKV_START + D_NA
D_MIX = D_FOURIER + D_CONV + D_NA
N_MOD = 6
RMS_EPS = 1e-6
LN_EPS = 1e-5

LANES = 128
SUBLANES = 8
HEAD_PAIRS = N_NA_HEADS * NA_HEAD_DIM // LANES
MOD_ROWS = 16
VMEM_LIMIT = 60 * 1024 * 1024

TM_INPROJ = 1024
TM_MLP = 512
TN_ADALN = 2048

F32 = jnp.float32
BF16 = jnp.bfloat16


def _params(*sem):
    return pltpu.CompilerParams(dimension_semantics=sem, vmem_limit_bytes=VMEM_LIMIT)


def _sigmoid(x):
    return 1.0 / (1.0 + jnp.exp(-x))


def _rms_mod(x, g, shift, scale):
    return x * lax.rsqrt(jnp.mean(x * x, axis=-1, keepdims=True) + RMS_EPS) * (g * (1.0 + scale)) + shift


def _bdot(a, b):
    return jnp.dot(a, b, preferred_element_type=F32)


def _adaln_kernel(c_ref, w_ref, b_ref, o_ref):
    cc = c_ref[...]
    a = cc * _sigmoid(cc)
    w = w_ref[0]
    a_hi = a.astype(BF16)
    a_lo = (a - a_hi.astype(F32)).astype(BF16)
    w_hi = w.astype(BF16)
    w_lo = (w - w_hi.astype(F32)).astype(BF16)
    by_hi = _bdot(jnp.concatenate([a_hi, a_lo], axis=0), w_hi)
    o_ref[0] = by_hi[:MOD_ROWS] + by_hi[MOD_ROWS:] + _bdot(a_hi, w_lo) + b_ref[0]


def _adaln(cc, ada_w, ada_b):
    depth, d, n = ada_w.shape
    tn = TN_ADALN
    return pl.pallas_call(
        _adaln_kernel,
        grid=(depth, n // tn),
        in_specs=[pl.BlockSpec((MOD_ROWS, d), lambda l, j: (0, 0)),
                  pl.BlockSpec((1, d, tn), lambda l, j: (l, 0, j)),
                  pl.BlockSpec((1, 1, tn), lambda l, j: (l, 0, j))],
        out_specs=pl.BlockSpec((1, MOD_ROWS, tn), lambda l, j: (l, 0, j)),
        out_shape=jax.ShapeDtypeStruct((depth, MOD_ROWS, n), F32),
        compiler_params=_params("parallel", "parallel"),
        name="adaln",
    )(cc, ada_w, ada_b.reshape(depth, 1, n))


def _fold_kernel(cc_ref, sc_ref, w_ref, o_ref):
    w = w_ref[0]
    hi = lax.Precision.HIGHEST
    o_ref[0, :, :D_FOURIER] = jnp.dot(cc_ref[...], w, preferred_element_type=F32, precision=hi).astype(BF16)
    o_ref[0, :, D_FOURIER:] = jnp.dot(sc_ref[...], w, preferred_element_type=F32, precision=hi).astype(BF16)


def _fold_fourier(w_fourier, seq):
    depth = w_fourier.shape[0]
    gs = D_FOURIER // FOURIER_GROUPS
    idx = np.arange(D_FOURIER)
    same = (idx[:, None] // gs) == (idx[None, :] // gs)
    ang = 2.0 * np.pi * (((idx[:, None] % gs) * (idx[None, :] % gs)) % gs) / gs
    scale = 1.0 / np.sqrt(seq * gs)
    cc = jnp.asarray(np.where(same, np.cos(ang), 0.0) * scale, F32)
    sc = jnp.asarray(np.where(same, np.sin(ang), 0.0) * scale, F32)
    full = pl.BlockSpec((D_FOURIER, D_FOURIER), lambda l: (0, 0))
    return pl.pallas_call(
        _fold_kernel,
        grid=(depth,),
        in_specs=[full, full, pl.BlockSpec((1, D_FOURIER, D_FOURIER), lambda l: (l, 0, 0))],
        out_specs=pl.BlockSpec((1, D_FOURIER, 2 * D_FOURIER), lambda l: (l, 0, 0)),
        out_shape=jax.ShapeDtypeStruct((depth, D_FOURIER, 2 * D_FOURIER), BF16),
        compiler_params=_params("parallel"),
        name="fold_fourier",
    )(cc, sc, w_fourier)


FLIP_BLOCK = 128


def _dft_mats(seq):
    half = seq // 2
    k = np.arange(half)
    ang = 2.0 * np.pi * ((k[:, None] * k[None, :]) % seq) / seq
    r = np.arange(FLIP_BLOCK)
    perm = (np.arange(2 * FLIP_BLOCK)[None, :] == FLIP_BLOCK - r[:, None]).astype(np.float32)
    alt_col = np.where(k % 2 == 0, 1.0, -1.0).astype(np.float32)[:, None]
    alt_row = np.zeros((SUBLANES, seq), np.float32)
    alt_row[0] = np.where(np.arange(seq) % 2 == 0, 1.0, -1.0)
    return (jnp.asarray(np.cos(ang), F32).astype(BF16), jnp.asarray(np.sin(ang), F32).astype(BF16),
            jnp.asarray(perm, F32).astype(BF16), jnp.asarray(alt_col), jnp.asarray(alt_row, F32).astype(BF16))


def _store_kv(hn, w_ref, out_ref, rows, col):
    kv = _bdot(hn, w_ref[:, KV_START:])
    v = kv[:, D_NA:]
    first = lax.broadcasted_iota(jnp.int32, (1, D_NA), 1) % LANES < NA_HEAD_DIM
    out_ref[rows, col:col + D_NA] = kv[:, :D_NA].astype(BF16)
    out_ref[rows, col + D_NA:col + 2 * D_NA] = jnp.where(first, v, 1.0).astype(BF16)
    out_ref[rows, col + 2 * D_NA:col + 3 * D_NA] = jnp.where(first, 1.0, v).astype(BF16)


INPROJ_SPLIT_ROWS = 512


def _inproj_kernel(x_ref, g_ref, mod_ref, w_ref, ab_ref, zf_ref, uc_ref, qkv_ref):
    tm = x_ref.shape[0]
    part = min(tm, INPROJ_SPLIT_ROWS)
    for r in range(tm // part):
        rows = slice(r * part, (r + 1) * part)
        hn = _rms_mod(x_ref[rows, :], g_ref[...], mod_ref[0, 0:1, :], mod_ref[0, 1:2, :]).astype(BF16)
        uf = _bdot(hn, w_ref[:, :D_FOURIER])
        uc_ref[rows, :] = _bdot(hn, w_ref[:, D_FOURIER:QKV_START])
        qkv_ref[rows, :D_NA] = (_bdot(hn, w_ref[:, QKV_START:KV_START]) * (NA_HEAD_DIM ** -0.5)).astype(BF16)
        _store_kv(hn, w_ref, qkv_ref, rows, D_NA)
        zf_ref[rows, :] = _bdot(uf.astype(BF16), ab_ref[...]).astype(BF16)


def _inproj(h, g, mods, w_in, ab, *, layer, seq, tm, ctx):
    t, d = h.shape
    nb = t // seq
    per = seq // tm
    row = (lambda i: nb) if ctx else (lambda i: i // per)
    d_in = w_in.shape[2]
    return pl.pallas_call(
        _inproj_kernel,
        grid=(t // tm,),
        in_specs=[pl.BlockSpec((tm, d), lambda i: (i, 0)),
                  pl.BlockSpec((1, d), lambda i: (0, 0)),
                  pl.BlockSpec((1, N_MOD, d), lambda i: (row(i), 0, 0)),
                  pl.BlockSpec((None, d, d_in), lambda i: (layer, 0, 0), pipeline_mode=pl.Buffered(1)),
                  pl.BlockSpec((D_FOURIER, 2 * D_FOURIER), lambda i: (0, 0))],
        out_specs=[pl.BlockSpec((tm, 2 * D_FOURIER), lambda i: (i % per, i // per)),
                   pl.BlockSpec((tm, 2 * D_CONV), lambda i: (i, 0)),
                   pl.BlockSpec((tm, 4 * D_NA), lambda i: (i, 0))],
        out_shape=[jax.ShapeDtypeStruct((seq, nb * 2 * D_FOURIER), BF16),
                   jax.ShapeDtypeStruct((t, 2 * D_CONV), F32),
                   jax.ShapeDtypeStruct((t, 4 * D_NA), BF16)],
        compiler_params=_params("parallel"),
        name="inproj_ctx" if ctx else "inproj_lat",
    )(h, g, mods, w_in, ab)


def _kvproj_kernel(x_ref, g_ref, mod_ref, w_ref, kv_ref):
    hn = _rms_mod(x_ref[...], g_ref[...], mod_ref[0, 0:1, :], mod_ref[0, 1:2, :]).astype(BF16)
    _store_kv(hn, w_ref, kv_ref, slice(None), 0)


def _kvproj(h, g, mods, w_in, *, layer, nb, tm):
    t, d = h.shape
    d_in = w_in.shape[2]
    n = 3 * D_NA
    return pl.pallas_call(
        _kvproj_kernel,
        grid=(t // tm,),
        in_specs=[pl.BlockSpec((tm, d), lambda i: (i, 0)),
                  pl.BlockSpec((1, d), lambda i: (0, 0)),
                  pl.BlockSpec((1, N_MOD, d), lambda i: (nb, 0, 0)),
                  pl.BlockSpec((None, d, d_in), lambda i: (layer, 0, 0))],
        out_specs=pl.BlockSpec((tm, n), lambda i: (i, 0)),
        out_shape=jax.ShapeDtypeStruct((t, n), BF16),
        compiler_params=_params("parallel"),
        name="kvproj_ctx",
    )(h, g, mods, w_in)


CONV_PAD = 16
CONV_CHUNK = 512
CONV_SPAN = 512


N_CONVFOURIER_IN = 13


def _convfourier_kernel(*refs, rider):
    (u_ref, dww_ref, dwb_ref, lng_ref, lnb_ref, pww_ref, pwb_ref,
     c_ref, s_ref, perm_ref, altc_ref, altr_ref, z_ref) = refs[:N_CONVFOURIER_IN]
    n_in = N_CONVFOURIER_IN + (rider.n_in if rider else 0)
    n_out = 2 + (rider.n_out if rider else 0)
    o_ref, y_ref = refs[n_in:n_in + 2]
    vs_ref, ze_ref, yr_ref = refs[n_in + n_out:n_in + n_out + 3]
    seq = u_ref.shape[0]
    half = seq // 2
    fb = FLIP_BLOCK
    nblk = half // fb
    first = CONV_PAD - CONV_WIDTH // 2
    rows = min(CONV_CHUNK, seq)
    span = min(CONV_SPAN, seq)
    n_pad = span + 2 * CONV_PAD
    n_chunks = seq // rows
    assert (first + CONV_WIDTH - 1) // SUBLANES * SUBLANES + span <= n_pad - SUBLANES

    def fill_span(lo):
        a, b = max(lo - CONV_PAD, 0), min(lo + span + CONV_PAD, seq)
        zeros = jnp.zeros((CONV_PAD, D_CONV), F32)
        if lo == 0:
            vs_ref[0, 0:CONV_PAD, :] = zeros
        if lo + span == seq:
            vs_ref[0, span + CONV_PAD:n_pad, :] = zeros
        dst = a - (lo - CONV_PAD)
        vs_ref[0, dst:dst + b - a, :] = u_ref[a:b, :D_CONV] * _sigmoid(u_ref[a:b, D_CONV:])
        for s in range(1, SUBLANES):
            vs_ref[s, 0:n_pad - SUBLANES, :] = vs_ref[0, s:s + n_pad - SUBLANES, :]

    def conv_chunk(lo, base):
        acc = jnp.zeros((rows, D_CONV), F32) + dwb_ref[...]
        for t in range(CONV_WIDTH):
            s, a = (first + t) % SUBLANES, (first + t) // SUBLANES
            acc = acc + vs_ref[s, base + a * SUBLANES:base + a * SUBLANES + rows, :] * dww_ref[t:t + 1, :]
        mu = jnp.mean(acc, axis=-1, keepdims=True)
        cen = acc - mu
        var = jnp.mean(cen * cen, axis=-1, keepdims=True)
        y = cen * lax.rsqrt(var + LN_EPS) * lng_ref[...] + lnb_ref[...]
        y = y * _sigmoid(y)
        o_ref[lo + base:lo + base + rows, :] = (_bdot(y.astype(BF16), pww_ref[...]) + pwb_ref[...]).astype(BF16)

    def reversed_block(ref, i, n):
        if i == 0:
            return _bdot(perm_ref[:, :fb], ref[n - fb:n, :])
        lo = n - (i + 1) * fb
        return _bdot(perm_ref[...], ref[lo:lo + 2 * fb, :])

    def fold_in():
        for i in range(nblk):
            rows_i = slice(i * fb, (i + 1) * fb)
            zr = reversed_block(z_ref, i, seq)
            ze_ref[rows_i, :D_FOURIER] = (z_ref[rows_i, :D_FOURIER].astype(F32) + zr[:, :D_FOURIER]).astype(BF16)
            ze_ref[rows_i, D_FOURIER:] = (z_ref[rows_i, D_FOURIER:].astype(F32) - zr[:, D_FOURIER:]).astype(BF16)

    def dft(part, n_parts):
        r = slice(part * half // n_parts, (part + 1) * half // n_parts)
        nyq = z_ref[half:half + 1, :D_FOURIER].astype(F32)
        yc = _bdot(c_ref[r, :], ze_ref[:, :D_FOURIER]) + altc_ref[r, :] * nyq
        ys = _bdot(s_ref[r, :], ze_ref[:, D_FOURIER:])
        y_ref[r, :] = (yc - ys).astype(BF16)
        yr_ref[r, :] = (yc + ys).astype(BF16)

    def fold_out():
        y_mid = _bdot(altr_ref[...], z_ref[:, :D_FOURIER])[0:1, :]
        first_row = lax.broadcasted_iota(jnp.int32, (fb, 1), 0) == 0
        for i in range(nblk):
            blk = reversed_block(yr_ref, i, half)
            if i == 0:
                blk = jnp.where(first_row, y_mid, blk)
            y_ref[half + i * fb:half + (i + 1) * fb, :] = blk.astype(BF16)

    phases = [[fold_in], [functools.partial(dft, 0, 2)], [functools.partial(dft, 1, 2)], [fold_out]]
    if rider:
        extra = rider.phases(refs[N_CONVFOURIER_IN:n_in], refs[n_in + 2:n_in + n_out], refs[n_in + n_out + 3:])
        assert len(extra) == len(phases)
        phases = [own + more for own, more in zip(phases, extra)]
    for ci in range(n_chunks):
        lo = ci * rows // span * span
        if ci * rows == lo:
            fill_span(lo)
        for group in phases[ci * len(phases) // n_chunks:(ci + 1) * len(phases) // n_chunks]:
            for phase in group:
                phase()
        conv_chunk(lo, ci * rows - lo)


class Rider(NamedTuple):
    phases: Callable
    inputs: tuple
    in_specs: tuple
    out_shapes: tuple
    out_specs: tuple
    scratch: tuple

    @property
    def n_in(self):
        return len(self.inputs)

    @property
    def n_out(self):
        return len(self.out_shapes)


def _convfourier(uc, conv_p, mats, zf, *, seq, rider=None):
    t = uc.shape[0]
    cmat, smat, perm, alt_col, alt_row = mats
    half = seq // 2
    vec = pl.BlockSpec((1, D_CONV), lambda b: (0, 0))
    const = lambda a: pl.BlockSpec(a.shape, lambda b: (0, 0), pipeline_mode=pl.Buffered(1))
    in_specs = [pl.BlockSpec((seq, 2 * D_CONV), lambda b: (b, 0)),
                pl.BlockSpec((CONV_WIDTH, D_CONV), lambda b: (0, 0)),
                vec, vec, vec,
                pl.BlockSpec((D_CONV, D_CONV), lambda b: (0, 0)),
                vec,
                const(cmat), const(smat), const(perm), const(alt_col), const(alt_row),
                pl.BlockSpec((seq, 2 * D_FOURIER), lambda b: (0, b))]
    assert len(in_specs) == N_CONVFOURIER_IN
    out_specs = [pl.BlockSpec((seq, D_CONV), lambda b: (b, 0)),
                 pl.BlockSpec((seq, D_FOURIER), lambda b: (0, b))]
    out_shape = [jax.ShapeDtypeStruct((t, D_CONV), BF16),
                 jax.ShapeDtypeStruct((seq, t // seq * D_FOURIER), BF16)]
    scratch = [pltpu.VMEM((SUBLANES, min(CONV_SPAN, seq) + 2 * CONV_PAD, D_CONV), F32),
               pltpu.VMEM((half, 2 * D_FOURIER), BF16), pltpu.VMEM((half, D_FOURIER), BF16)]
    operands = [uc, *conv_p, cmat, smat, perm, alt_col, alt_row, zf]
    if rider:
        in_specs += rider.in_specs
        out_specs += rider.out_specs
        out_shape += rider.out_shapes
        scratch += rider.scratch
        operands += rider.inputs
    return pl.pallas_call(
        functools.partial(_convfourier_kernel, rider=rider),
        grid=(t // seq,),
        in_specs=in_specs,
        out_specs=out_specs,
        out_shape=out_shape,
        scratch_shapes=scratch,
        compiler_params=_params("parallel"),
        name="conv_fourier",
    )(*operands)


def _inproj_rider(h_ctx, g1, mods, w_in, ab, *, layer, nb, n_ctx):
    d = h_ctx.shape[1]
    d_in = w_in.shape[2]
    t = h_ctx.shape[0]

    def phases(in_refs, out_refs, scratch_refs):
        return [[], [functools.partial(_inproj_kernel, *in_refs, *out_refs)], [], []]

    return Rider(
        phases=phases,
        inputs=(h_ctx, g1, mods, w_in, ab),
        in_specs=(pl.BlockSpec((n_ctx, d), lambda b: (b, 0)),
                  pl.BlockSpec((1, d), lambda b: (0, 0)),
                  pl.BlockSpec((1, N_MOD, d), lambda b: (nb, 0, 0)),
                  pl.BlockSpec((None, d, d_in), lambda b: (layer, 0, 0), pipeline_mode=pl.Buffered(1)),
                  pl.BlockSpec((D_FOURIER, 2 * D_FOURIER), lambda b: (0, 0))),
        out_shapes=(jax.ShapeDtypeStruct((n_ctx, nb * 2 * D_FOURIER), BF16),
                    jax.ShapeDtypeStruct((t, 2 * D_CONV), F32),
                    jax.ShapeDtypeStruct((t, 4 * D_NA), BF16)),
        out_specs=(pl.BlockSpec((n_ctx, 2 * D_FOURIER), lambda b: (0, b)),
                   pl.BlockSpec((n_ctx, 2 * D_CONV), lambda b: (b, 0)),
                   pl.BlockSpec((n_ctx, 4 * D_NA), lambda b: (b, 0))),
        scratch=())


def _ctx_tail_rider(h_ctx, yf_c, cv_c, at_c, mods_prev, mods_cur, g2_prev, g1_cur, wo, w1, w2, w_in, *,
                    mlp_layer, in_layer, nb, n_ctx):
    d = h_ctx.shape[1]
    d_ff = w1.shape[2]
    d_in = w_in.shape[2]
    t = h_ctx.shape[0]

    def phases(in_refs, out_refs, scratch_refs):
        (h_ref, yf_ref, cv_ref, at_ref, modp_ref, modc_ref, g2_ref, g1_ref, wo_ref, w1_ref, w2_ref, w_ref) = in_refs
        (kv_ref,) = out_refs
        h1_ref, hn_ref, a_ref = scratch_refs

        def mix():
            m = (_bdot(yf_ref[...], wo_ref[:D_FOURIER, :])
                 + _bdot(cv_ref[...], wo_ref[D_FOURIER:D_FOURIER + D_CONV, :])
                 + _bdot(at_ref[...], wo_ref[D_FOURIER + D_CONV:, :]))
            h1 = h_ref[...] + modp_ref[0, 2:3, :] * m
            h1_ref[...] = h1
            hn_ref[...] = _rms_mod(h1, g2_ref[...], modp_ref[0, 3:4, :], modp_ref[0, 4:5, :]).astype(BF16)

        def up(lo, hi):
            a = jnp.maximum(_bdot(hn_ref[...], w1_ref[:, lo:hi]), 0.0)
            a_ref[:, lo:hi] = (a * a).astype(BF16)

        def down():
            h2 = h1_ref[...] + modp_ref[0, 5:6, :] * _bdot(a_ref[...], w2_ref[...])
            hn = _rms_mod(h2, g1_ref[...], modc_ref[0, 0:1, :], modc_ref[0, 1:2, :]).astype(BF16)
            _store_kv(hn, w_ref, kv_ref, slice(None), 0)

        return [[mix], [functools.partial(up, 0, d_ff // 2)], [functools.partial(up, d_ff // 2, d_ff)], [down]]

    row = lambda b: (b, 0)
    vec = pl.BlockSpec((1, d), lambda b: (0, 0))
    mod = pl.BlockSpec((1, N_MOD, d), lambda b: (nb, 0, 0))
    once = dict(pipeline_mode=pl.Buffered(1))
    return Rider(
        phases=phases,
        inputs=(h_ctx, yf_c, cv_c, at_c, mods_prev, mods_cur, g2_prev, g1_cur, wo, w1, w2, w_in),
        in_specs=(pl.BlockSpec((n_ctx, d), row),
                  pl.BlockSpec((n_ctx, D_FOURIER), lambda b: (0, b)),
                  pl.BlockSpec((n_ctx, D_CONV), row),
                  pl.BlockSpec((n_ctx, D_NA), row),
                  mod, mod, vec, vec,
                  pl.BlockSpec((None, D_MIX, d), lambda b: (mlp_layer, 0, 0), **once),
                  pl.BlockSpec((None, d, d_ff), lambda b: (mlp_layer, 0, 0), **once),
                  pl.BlockSpec((None, d_ff, d), lambda b: (mlp_layer, 0, 0), **once),
                  pl.BlockSpec((None, d, d_in), lambda b: (in_layer, 0, 0), **once)),
        out_shapes=(jax.ShapeDtypeStruct((t, 3 * D_NA), BF16),),
        out_specs=(pl.BlockSpec((n_ctx, 3 * D_NA), row),),
        scratch=(pltpu.VMEM((n_ctx, d), F32), pltpu.VMEM((n_ctx, d), BF16), pltpu.VMEM((n_ctx, d_ff), BF16)))


def _head_mask():
    return lax.broadcasted_iota(jnp.int32, (1, LANES), 1) < NA_HEAD_DIM


Q_ROWS = 2
BAND_ROWS = 10
assert BAND_ROWS >= WIN_ROWS + Q_ROWS - 1 and (BAND_ROWS * GRID_W) % LANES == 0


def _na_patterns(rows):
    starts, sigs = [], []
    for g in range(rows // Q_ROWS):
        start = int(np.clip(Q_ROWS * g - WIN_ROWS // 2, 0, rows - BAND_ROWS))
        r = Q_ROWS * g + np.arange(Q_ROWS)
        rs = np.clip(r - WIN_ROWS // 2, 0, rows - WIN_ROWS)
        starts.append(start)
        sigs.append((start - Q_ROWS * g,) + tuple(rs - r))
    run_starts = [g for g in range(len(sigs)) if g == 0 or sigs[g] != sigs[g - 1]]
    assert len(set(sigs)) == len(run_starts)
    return starts, run_starts


def _na_kernel(q_ref, k_ref, va_ref, vb_ref, kc_ref, vca_ref, vcb_ref, rpb_ref, o_ref, s_ref, p_ref, bias_ref, *,
               seq, n_ctx):
    rows = seq // GRID_W
    per_batch = rows // Q_ROWS
    n_groups = q_ref.shape[0] // seq * per_batch
    n_q = Q_ROWS * GRID_W
    n_loc = BAND_ROWS * GRID_W
    _, run_starts = _na_patterns(rows)
    first = _head_mask()
    dn = (((1,), (1,)), ((), ()))
    assert n_groups >= 4 and seq == per_batch * n_q

    def split(t):
        t = jnp.asarray(t, jnp.int32)
        return t // per_batch, t % per_batch

    def band(t):
        b, g = split(t)
        start = jnp.clip(Q_ROWS * g - WIN_ROWS // 2, 0, rows - BAND_ROWS)
        return pl.multiple_of(b * seq + start * GRID_W, LANES)

    def q_start(t):
        return pl.multiple_of(jnp.asarray(t, jnp.int32) * n_q, n_q)

    def ctx_start(t):
        return pl.multiple_of(split(t)[0] * n_ctx, n_ctx)

    def scores(t, slot):
        g = split(t)[1]
        pat = sum((g >= s).astype(jnp.int32) for s in run_starts[1:])
        q = q_ref[pl.ds(q_start(t), n_q), :]
        kb = k_ref[pl.ds(band(t), n_loc), :]
        kc = kc_ref[pl.ds(ctx_start(t), n_ctx), :]
        for a in range(2):
            qa = jnp.where(first if a == 0 else ~first, q, jnp.zeros_like(q))
            s_ref[slot, a, :, :n_loc] = lax.dot_general(qa, kb, dn, preferred_element_type=F32) + bias_ref[a, pat]
            s_ref[slot, a, :, n_loc:] = lax.dot_general(qa, kc, dn, preferred_element_type=F32)

    def softmax(slot):
        for a in range(2):
            s = s_ref[slot, a]
            p_ref[slot, a] = jnp.exp((s - jnp.max(s, axis=-1, keepdims=True)).astype(BF16))

    def values(t, slot):
        outs = []
        for a, (v_ref, vc_ref) in enumerate(((va_ref, vca_ref), (vb_ref, vcb_ref))):
            o = (_bdot(p_ref[slot, a, :, :n_loc], v_ref[pl.ds(band(t), n_loc), :])
                 + _bdot(p_ref[slot, a, :, n_loc:], vc_ref[pl.ds(ctx_start(t), n_ctx), :]))
            outs.append(o * (1.0 / pltpu.roll(o, NA_HEAD_DIM, axis=1)))
        o_ref[pl.ds(q_start(t), n_q), :] = jnp.where(first, outs[0], outs[1]).astype(BF16)

    def tick(t, parity):
        values(t - 1, 1 - parity)
        scores(t + 1, 1 - parity)
        softmax(parity)

    def quad(j, carry):
        t = 4 * j + 1
        tick(t, 1)
        tick(t + 1, 0)
        tick(t + 2, 1)
        tick(t + 3, 0)
        return carry

    @pl.when(pl.program_id(1) == 0)
    def _():
        _na_assemble_bias(rpb_ref, bias_ref, rows)

    scores(0, 0)
    scores(1, 1)
    softmax(0)
    n_quads = (n_groups - 2) // 4
    lax.fori_loop(0, n_quads, quad, 0)
    for t in range(4 * n_quads + 1, n_groups - 1):
        tick(t, t % 2)
    softmax((n_groups - 1) % 2)
    values(n_groups - 2, n_groups % 2)
    values(n_groups - 1, (n_groups - 1) % 2)


def _na_bias(rpb):
    n_heads, n_dr, n_dc = rpb.shape
    r = rpb.astype(F32)
    w = jnp.concatenate([r[..., WIN_COLS - 1:], jnp.zeros((n_heads, n_dr, LANES - n_dc), F32), r[..., :WIN_COLS - 1]],
                        axis=-1)
    w = jnp.pad(w, ((0, 0), (0, 1), (0, 0)))
    return w.reshape(HEAD_PAIRS, 2, n_dr + 1, LANES)


def _na_assemble_bias(w_ref, bias_ref, rows):
    starts, run_starts = _na_patterns(rows)
    cq = lax.broadcasted_iota(jnp.int32, (GRID_W, LANES), 0)
    lane = lax.broadcasted_iota(jnp.int32, (GRID_W, LANES), 1)
    low = lane < GRID_W
    ck = jnp.where(low, lane, lane - GRID_W)
    col_start = jnp.clip(cq - WIN_COLS // 2, 0, GRID_W - WIN_COLS)
    in_window = (ck >= col_start) & (ck < col_start + WIN_COLS)
    masked = jnp.full((GRID_W, LANES), -jnp.inf, F32)
    rolled = {}

    def half(a, d, upper):
        if (a, d, upper) not in rolled:
            row = jnp.broadcast_to(w_ref[0, a, d:d + 1, :], (GRID_W, LANES))
            rolled[a, d, upper] = pltpu.roll(row, GRID_W if upper else 0, axis=1, stride=1, stride_axis=0)
        return rolled[a, d, upper]

    for a in range(2):
        for p, g in enumerate(run_starts):
            for qi in range(Q_ROWS):
                r = Q_ROWS * g + qi
                rs = int(np.clip(r - WIN_ROWS // 2, 0, rows - WIN_ROWS))
                below = rs - starts[g]
                dr0 = rs - r + WIN_ROWS - 1
                for t in range(BAND_ROWS // 2):
                    d0 = dr0 + 2 * t - below
                    v0 = below <= 2 * t < below + WIN_ROWS
                    v1 = below <= 2 * t + 1 < below + WIN_ROWS
                    if v0 and v1:
                        tile = jnp.where(low, half(a, d0, False), half(a, d0 + 1, True))
                    elif v0:
                        tile = jnp.where(low, half(a, d0, False), masked)
                    elif v1:
                        tile = jnp.where(low, masked, half(a, d0 + 1, True))
                    else:
                        tile = masked
                    if v0 or v1:
                        tile = jnp.where(in_window, tile, masked)
                    bias_ref[a, p, qi * GRID_W:(qi + 1) * GRID_W, t * LANES:(t + 1) * LANES] = tile


NA_BATCHES = 4


def _na(qkv, kv_ctx, rpb_rows, *, seq, n_ctx, kc_off):
    t = qkv.shape[0]
    hp = HEAD_PAIRS
    tb, tbc = NA_BATCHES * seq, NA_BATCHES * n_ctx
    n_pat = len(_na_patterns(seq // GRID_W)[1])
    n_keys = BAND_ROWS * GRID_W + n_ctx
    assert t % tb == 0
    return pl.pallas_call(
        functools.partial(_na_kernel, seq=seq, n_ctx=n_ctx),
        grid=(hp, t // tb),
        in_specs=[pl.BlockSpec((tb, LANES), lambda h, b: (b, h)),
                  pl.BlockSpec((tb, LANES), lambda h, b: (b, hp + h)),
                  pl.BlockSpec((tb, LANES), lambda h, b: (b, 2 * hp + h)),
                  pl.BlockSpec((tb, LANES), lambda h, b: (b, 3 * hp + h)),
                  pl.BlockSpec((tbc, LANES), lambda h, b: (b, kc_off + h)),
                  pl.BlockSpec((tbc, LANES), lambda h, b: (b, kc_off + hp + h)),
                  pl.BlockSpec((tbc, LANES), lambda h, b: (b, kc_off + 2 * hp + h)),
                  pl.BlockSpec((1,) + rpb_rows.shape[1:], lambda h, b: (h, 0, 0, 0))],
        out_specs=pl.BlockSpec((tb, LANES), lambda h, b: (b, h)),
        out_shape=jax.ShapeDtypeStruct((t, D_NA), BF16),
        scratch_shapes=[pltpu.VMEM((2, 2, Q_ROWS * GRID_W, n_keys), F32),
                        pltpu.VMEM((2, 2, Q_ROWS * GRID_W, n_keys), BF16),
                        pltpu.VMEM((2, n_pat, Q_ROWS * GRID_W, BAND_ROWS * GRID_W), F32)],
        compiler_params=_params("parallel", "arbitrary"),
        name="na_attention",
    )(qkv, qkv, qkv, qkv, kv_ctx, kv_ctx, kv_ctx, rpb_rows)


def _ctx_attn_kernel(q_ref, k_ref, va_ref, vb_ref, o_ref):
    first = _head_mask()
    for h in range(HEAD_PAIRS):
        lanes = slice(h * LANES, (h + 1) * LANES)
        q = q_ref[:, lanes]
        k = k_ref[:, lanes]
        outs = []
        for a, v_ref in enumerate((va_ref, vb_ref)):
            qa = jnp.where(first if a == 0 else ~first, q, jnp.zeros_like(q))
            s = lax.dot_general(qa, k, (((1,), (1,)), ((), ())), preferred_element_type=F32)
            p = jnp.exp(s - jnp.max(s, axis=-1, keepdims=True))
            den = jnp.sum(p, axis=-1, keepdims=True)
            outs.append(_bdot(p.astype(BF16), v_ref[:, lanes]) * (1.0 / den))
        o_ref[:, lanes] = jnp.where(first, outs[0], outs[1]).astype(BF16)


def _ctx_attn(qkv, *, n_ctx):
    t = qkv.shape[0]
    return pl.pallas_call(
        _ctx_attn_kernel,
        grid=(t // n_ctx,),
        in_specs=[pl.BlockSpec((n_ctx, D_NA), lambda b: (b, 0)),
                  pl.BlockSpec((n_ctx, D_NA), lambda b: (b, 1)),
                  pl.BlockSpec((n_ctx, D_NA), lambda b: (b, 2)),
                  pl.BlockSpec((n_ctx, D_NA), lambda b: (b, 3))],
        out_specs=pl.BlockSpec((n_ctx, D_NA), lambda b: (b, 0)),
        out_shape=jax.ShapeDtypeStruct((t, D_NA), BF16),
        compiler_params=_params("parallel"),
        name="ctx_attention",
    )(qkv, qkv, qkv, qkv)


N_OUTMLP_IN = 10


def _outmlp_kernel(*refs, final_norm, n_cast):
    h_ref, yf_ref, cv_ref, at_ref, mod_ref, g_ref, gf_ref, wo_ref, w1_ref, w2_ref = refs[:N_OUTMLP_IN]
    o_ref = refs[N_OUTMLP_IN + n_cast]
    for src, dst in zip(refs[N_OUTMLP_IN:N_OUTMLP_IN + n_cast], refs[N_OUTMLP_IN + n_cast + 1:]):
        dst[...] = src[...].astype(BF16)
    mix = (_bdot(yf_ref[...], wo_ref[:D_FOURIER, :])
           + _bdot(cv_ref[...], wo_ref[D_FOURIER:D_FOURIER + D_CONV, :])
           + _bdot(at_ref[...], wo_ref[D_FOURIER + D_CONV:, :]))
    h1 = h_ref[...] + mod_ref[0, 2:3, :] * mix
    hn = _rms_mod(h1, g_ref[...], mod_ref[0, 3:4, :], mod_ref[0, 4:5, :]).astype(BF16)
    a = jnp.maximum(_bdot(hn, w1_ref[...]), 0.0)
    out = h1 + mod_ref[0, 5:6, :] * _bdot((a * a).astype(BF16), w2_ref[...])
    if final_norm:
        out = out * lax.rsqrt(jnp.mean(out * out, axis=-1, keepdims=True) + RMS_EPS) * gf_ref[...]
    o_ref[...] = out


def _outmlp(h, yf, cv, at, mods, g2, gf, wo, w1, w2, *, layer, seq, tm, ctx, final_norm, cast=(), cast_layer=0):
    t, d = h.shape
    nb = t // seq
    per = seq // tm
    row = (lambda i: nb) if ctx else (lambda i: i // per)
    d_ff = w1.shape[2]
    vec = pl.BlockSpec((1, d), lambda i: (0, 0))
    once = dict(pipeline_mode=pl.Buffered(1))
    steps = t // tm
    cast_in = [pl.BlockSpec((None, w.shape[1] // steps, w.shape[2]), lambda i: (cast_layer, i, 0)) for w in cast]
    cast_out = [pl.BlockSpec((None, w.shape[1] // steps, w.shape[2]), lambda i: (0, i, 0)) for w in cast]
    cast_shape = [jax.ShapeDtypeStruct((1,) + w.shape[1:], BF16) for w in cast]
    assert all(w.shape[1] % (steps * 16) == 0 for w in cast)
    outs = pl.pallas_call(
        functools.partial(_outmlp_kernel, final_norm=final_norm, n_cast=len(cast)),
        grid=(steps,),
        in_specs=[pl.BlockSpec((tm, d), lambda i: (i, 0)),
                  pl.BlockSpec((tm, D_FOURIER), lambda i: (i % per, i // per)),
                  pl.BlockSpec((tm, D_CONV), lambda i: (i, 0)),
                  pl.BlockSpec((tm, D_NA), lambda i: (i, 0)),
                  pl.BlockSpec((1, N_MOD, d), lambda i: (row(i), 0, 0)),
                  vec, vec,
                  pl.BlockSpec((None, D_MIX, d), lambda i: (layer, 0, 0), **once),
                  pl.BlockSpec((None, d, d_ff), lambda i: (layer, 0, 0), **once),
                  pl.BlockSpec((None, d_ff, d), lambda i: (layer, 0, 0), **once)] + cast_in,
        out_specs=[pl.BlockSpec((tm, d), lambda i: (i, 0))] + cast_out,
        out_shape=[jax.ShapeDtypeStruct((t, d), F32)] + cast_shape,
        compiler_params=_params("parallel"),
        name="outmlp_ctx" if ctx else "outmlp_lat",
    )(h, yf, cv, at, mods, g2, gf, wo, w1, w2, *cast)
    return outs if cast else outs[0]


def kernel(x, c, ctx, c_ctx, ada_w, ada_b, norm1_g, norm2_g, w_in, w_fourier, conv_dw_w, conv_dw_b, conv_norm_g,
           conv_norm_b, conv_pw_w, conv_pw_b, na_rpb, w_out, mlp_w1, mlp_w2, final_norm_g):
    nb, seq, d = x.shape
    n_ctx = ctx.shape[1]
    depth = ada_w.shape[0]
    assert nb < MOD_ROWS and seq % GRID_W == 0 and d == D_MODEL

    cc = jnp.concatenate([c, c_ctx[None], jnp.zeros((MOD_ROWS - nb - 1, d), F32)], axis=0)
    mods = _adaln(cc, ada_w, ada_b).reshape(depth, MOD_ROWS, N_MOD, d)
    ab_lat = _fold_fourier(w_fourier, seq)
    ab_ctx = _fold_fourier(w_fourier, n_ctx)
    dft_lat = _dft_mats(seq)
    dft_ctx = _dft_mats(n_ctx)

    stacked = (w_in, w_out, mlp_w1, mlp_w2)
    weights = [tuple(w[:1].astype(BF16) for w in stacked)]
    pww_b = conv_pw_w.astype(BF16)
    gf = final_norm_g.reshape(1, d)

    h_lat = x.reshape(nb * seq, d)
    h_ctx = ctx.reshape(nb * n_ctx, d)
    ctx_mix = None
    for i in range(depth):
        last = i == depth - 1
        g1 = norm1_g[i].reshape(1, d)
        g2 = norm2_g[i].reshape(1, d)
        conv_p = (conv_dw_w[i], conv_dw_b[i].reshape(1, -1), conv_norm_g[i].reshape(1, -1),
                  conv_norm_b[i].reshape(1, -1), pww_b[i], conv_pw_b[i].reshape(1, -1))
        bias = _na_bias(na_rpb[i])

        w_in_b, w_out_b, w1_b, w2_b = weights[i]
        zf, uc, qkv = _inproj(h_lat, g1, mods[i], w_in_b, ab_lat[i], layer=0, seq=seq, tm=TM_INPROJ, ctx=False)
        if last and ctx_mix is not None:
            rider = _ctx_tail_rider(h_ctx, *ctx_mix, mods[i - 1], mods[i], norm2_g[i - 1].reshape(1, d), g1,
                                    *weights[i - 1][1:], w_in_b, mlp_layer=0, in_layer=0, nb=nb, n_ctx=n_ctx)
            cv, yf, kv_ctx = _convfourier(uc, conv_p, dft_lat, zf, seq=seq, rider=rider)
            kc_off = 0
        elif last:
            kv_ctx = _kvproj(h_ctx, g1, mods[i], w_in_b, layer=0, nb=nb, tm=n_ctx)
            cv, yf = _convfourier(uc, conv_p, dft_lat, zf, seq=seq)
            kc_off = 0
        else:
            if ctx_mix is not None:
                h_ctx = _outmlp(h_ctx, *ctx_mix, mods[i - 1], norm2_g[i - 1].reshape(1, d), gf, *weights[i - 1][1:],
                                layer=0, seq=n_ctx, tm=n_ctx, ctx=True, final_norm=False)
            rider = _inproj_rider(h_ctx, g1, mods[i], w_in_b, ab_ctx[i], layer=0, nb=nb, n_ctx=n_ctx)
            cv, yf, zf_c, uc_c, qkv_c = _convfourier(uc, conv_p, dft_lat, zf, seq=seq, rider=rider)
            kv_ctx = qkv_c
            kc_off = HEAD_PAIRS

        at = _na(qkv, kv_ctx, bias, seq=seq, n_ctx=n_ctx, kc_off=kc_off)
        if last:
            h_lat = _outmlp(h_lat, yf, cv, at, mods[i], g2, gf, w_out_b, w1_b, w2_b,
                            layer=0, seq=seq, tm=TM_MLP, ctx=False, final_norm=True)
        else:
            h_lat, *cast = _outmlp(h_lat, yf, cv, at, mods[i], g2, gf, w_out_b, w1_b, w2_b, layer=0, seq=seq,
                                   tm=TM_MLP, ctx=False, final_norm=False, cast=stacked, cast_layer=i + 1)
            weights.append(tuple(cast))

        if not last:
            cv_c, yf_c = _convfourier(uc_c, conv_p, dft_ctx, zf_c, seq=n_ctx)
            at_c = _ctx_attn(qkv_c, n_ctx=n_ctx)
            ctx_mix = (yf_c, cv_c, at_c)
    return h_lat.reshape(nb, seq, d)
```

```python
import functools
from typing import Callable, NamedTuple

import numpy as np
import jax
import jax.numpy as jnp
from jax import lax
from jax.experimental import pallas as pl
from jax.experimental.pallas import tpu as pltpu

D_MODEL = 1024
GRID_W = 64
D_FOURIER = 256
FOURIER_GROUPS = 4
D_CONV = 256
CONV_WIDTH = 31
N_NA_HEADS = 8
NA_HEAD_DIM = 64
D_NA = N_NA_HEADS * NA_HEAD_DIM
WIN_ROWS = 8
WIN_COLS = 16
QKV_START = D_FOURIER + 2 * D_CONV
KV_START = QKV_START + D_NA
D_MIX = D_FOURIER + D_CONV + D_NA
N_MOD = 6
RMS_EPS = 1e-6
LN_EPS = 1e-5

LANES = 128
SUBLANES = 8
HEAD_PAIRS = N_NA_HEADS * NA_HEAD_DIM // LANES
MOD_ROWS = 16
VMEM_LIMIT = 60 * 1024 * 1024

TM_INPROJ = 1024
TM_MLP = 512
TN_ADALN = 2048

F32 = jnp.float32
BF16 = jnp.bfloat16


def _params(*sem):
    return pltpu.CompilerParams(dimension_semantics=sem, vmem_limit_bytes=VMEM_LIMIT)


def _sigmoid(x):
    return 1.0 / (1.0 + jnp.exp(-x))


def _rms_mod(x, g, shift, scale):
    return x * lax.rsqrt(jnp.mean(x * x, axis=-1, keepdims=True) + RMS_EPS) * (g * (1.0 + scale)) + shift


def _bdot(a, b):
    return jnp.dot(a, b, preferred_element_type=F32)


def _adaln_kernel(c_ref, w_ref, b_ref, o_ref):
    cc = c_ref[...]
    a = cc * _sigmoid(cc)
    w = w_ref[0]
    a_hi = a.astype(BF16)
    a_lo = (a - a_hi.astype(F32)).astype(BF16)
    w_hi = w.astype(BF16)
    w_lo = (w - w_hi.astype(F32)).astype(BF16)
    by_hi = _bdot(jnp.concatenate([a_hi, a_lo], axis=0), w_hi)
    o_ref[0] = by_hi[:MOD_ROWS] + by_hi[MOD_ROWS:] + _bdot(a_hi, w_lo) + b_ref[0]


def _adaln(cc, ada_w, ada_b):
    depth, d, n = ada_w.shape
    tn = TN_ADALN
    return pl.pallas_call(
        _adaln_kernel,
        grid=(depth, n // tn),
        in_specs=[pl.BlockSpec((MOD_ROWS, d), lambda l, j: (0, 0)),
                  pl.BlockSpec((1, d, tn), lambda l, j: (l, 0, j)),
                  pl.BlockSpec((1, 1, tn), lambda l, j: (l, 0, j))],
        out_specs=pl.BlockSpec((1, MOD_ROWS, tn), lambda l, j: (l, 0, j)),
        out_shape=jax.ShapeDtypeStruct((depth, MOD_ROWS, n), F32),
        compiler_params=_params("parallel", "parallel"),
        name="adaln",
    )(cc, ada_w, ada_b.reshape(depth, 1, n))


def _fold_kernel(cc_ref, sc_ref, w_ref, o_ref):
    w = w_ref[0]
    hi = lax.Precision.HIGHEST
    o_ref[0, :, :D_FOURIER] = jnp.dot(cc_ref[...], w, preferred_element_type=F32, precision=hi).astype(BF16)
    o_ref[0, :, D_FOURIER:] = jnp.dot(sc_ref[...], w, preferred_element_type=F32, precision=hi).astype(BF16)


def _fold_fourier(w_fourier, seq):
    depth = w_fourier.shape[0]
    gs = D_FOURIER // FOURIER_GROUPS
    idx = np.arange(D_FOURIER)
    same = (idx[:, None] // gs) == (idx[None, :] // gs)
    ang = 2.0 * np.pi * (((idx[:, None] % gs) * (idx[None, :] % gs)) % gs) / gs
    scale = 1.0 / np.sqrt(seq * gs)
    cc = jnp.asarray(np.where(same, np.cos(ang), 0.0) * scale, F32)
    sc = jnp.asarray(np.where(same, np.sin(ang), 0.0) * scale, F32)
    full = pl.BlockSpec((D_FOURIER, D_FOURIER), lambda l: (0, 0))
    return pl.pallas_call(
        _fold_kernel,
        grid=(depth,),
        in_specs=[full, full, pl.BlockSpec((1, D_FOURIER, D_FOURIER), lambda l: (l, 0, 0))],
        out_specs=pl.BlockSpec((1, D_FOURIER, 2 * D_FOURIER), lambda l: (l, 0, 0)),
        out_shape=jax.ShapeDtypeStruct((depth, D_FOURIER, 2 * D_FOURIER), BF16),
        compiler_params=_params("parallel"),
        name="fold_fourier",
    )(cc, sc, w_fourier)


FLIP_BLOCK = 128


def _dft_mats(seq):
    half = seq // 2
    k = np.arange(half)
    ang = 2.0 * np.pi * ((k[:, None] * k[None, :]) % seq) / seq
    r = np.arange(FLIP_BLOCK)
    perm = (np.arange(2 * FLIP_BLOCK)[None, :] == FLIP_BLOCK - r[:, None]).astype(np.float32)
    alt_col = np.where(k % 2 == 0, 1.0, -1.0).astype(np.float32)[:, None]
    alt_row = np.zeros((SUBLANES, seq), np.float32)
    alt_row[0] = np.where(np.arange(seq) % 2 == 0, 1.0, -1.0)
    return (jnp.asarray(np.cos(ang), F32).astype(BF16), jnp.asarray(np.sin(ang), F32).astype(BF16),
            jnp.asarray(perm, F32).astype(BF16), jnp.asarray(alt_col), jnp.asarray(alt_row, F32).astype(BF16))


def _store_kv(hn, w_ref, out_ref, rows, col):
    kv = _bdot(hn, w_ref[:, KV_START:])
    v = kv[:, D_NA:]
    first = lax.broadcasted_iota(jnp.int32, (1, D_NA), 1) % LANES < NA_HEAD_DIM
    out_ref[rows, col:col + D_NA] = kv[:, :D_NA].astype(BF16)
    out_ref[rows, col + D_NA:col + 2 * D_NA] = jnp.where(first, v, 1.0).astype(BF16)
    out_ref[rows, col + 2 * D_NA:col + 3 * D_NA] = jnp.where(first, 1.0, v).astype(BF16)


INPROJ_SPLIT_ROWS = 512


def _inproj_kernel(x_ref, g_ref, mod_ref, w_ref, ab_ref, *rest, n_cast=0):
    zf_ref, uc_ref, qkv_ref = rest[n_cast:n_cast + 3]
    for src, dst in zip(rest[:n_cast], rest[n_cast + 3:]):
        dst[...] = src[...].astype(BF16)
    tm = x_ref.shape[0]
    part = min(tm, INPROJ_SPLIT_ROWS)
    for r in range(tm // part):
        rows = slice(r * part, (r + 1) * part)
        hn = _rms_mod(x_ref[rows, :], g_ref[...], mod_ref[0, 0:1, :], mod_ref[0, 1:2, :]).astype(BF16)
        uf = _bdot(hn, w_ref[:, :D_FOURIER])
        uc_ref[rows, :] = _bdot(hn, w_ref[:, D_FOURIER:QKV_START])
        qkv_ref[rows, :D_NA] = (_bdot(hn, w_ref[:, QKV_START:KV_START]) * (NA_HEAD_DIM ** -0.5)).astype(BF16)
        _store_kv(hn, w_ref, qkv_ref, rows, D_NA)
        zf_ref[rows, :] = _bdot(uf.astype(BF16), ab_ref[...]).astype(BF16)


def _inproj(h, g, mods, w_in, ab, *, layer, seq, tm, ctx, cast=(), cast_layer=0):
    t, d = h.shape
    nb = t // seq
    per = seq // tm
    row = (lambda i: nb) if ctx else (lambda i: i // per)
    d_in = w_in.shape[2]
    steps = t // tm
    cast_in = [pl.BlockSpec((None, w.shape[1] // steps, w.shape[2]), lambda i: (cast_layer, i, 0)) for w in cast]
    cast_out = [pl.BlockSpec((None, w.shape[1] // steps, w.shape[2]), lambda i: (0, i, 0)) for w in cast]
    cast_shape = [jax.ShapeDtypeStruct((1,) + w.shape[1:], BF16) for w in cast]
    assert all(w.shape[1] % (steps * 16) == 0 for w in cast)
    return pl.pallas_call(
        functools.partial(_inproj_kernel, n_cast=len(cast)),
        grid=(steps,),
        in_specs=[pl.BlockSpec((tm, d), lambda i: (i, 0)),
                  pl.BlockSpec((1, d), lambda i: (0, 0)),
                  pl.BlockSpec((1, N_MOD, d), lambda i: (row(i), 0, 0)),
                  pl.BlockSpec((None, d, d_in), lambda i: (layer, 0, 0), pipeline_mode=pl.Buffered(1)),
                  pl.BlockSpec((D_FOURIER, 2 * D_FOURIER), lambda i: (0, 0))] + cast_in,
        out_specs=[pl.BlockSpec((tm, 2 * D_FOURIER), lambda i: (i % per, i // per)),
                   pl.BlockSpec((tm, 2 * D_CONV), lambda i: (i, 0)),
                   pl.BlockSpec((tm, 4 * D_NA), lambda i: (i, 0))] + cast_out,
        out_shape=[jax.ShapeDtypeStruct((seq, nb * 2 * D_FOURIER), BF16),
                   jax.ShapeDtypeStruct((t, 2 * D_CONV), F32),
                   jax.ShapeDtypeStruct((t, 4 * D_NA), BF16)] + cast_shape,
        compiler_params=_params("parallel"),
        name="inproj_ctx" if ctx else "inproj_lat",
    )(h, g, mods, w_in, ab, *cast)


def _kvproj_kernel(x_ref, g_ref, mod_ref, w_ref, kv_ref):
    hn = _rms_mod(x_ref[...], g_ref[...], mod_ref[0, 0:1, :], mod_ref[0, 1:2, :]).astype(BF16)
    _store_kv(hn, w_ref, kv_ref, slice(None), 0)


def _kvproj(h, g, mods, w_in, *, layer, nb, tm):
    t, d = h.shape
    d_in = w_in.shape[2]
    n = 3 * D_NA
    return pl.pallas_call(
        _kvproj_kernel,
        grid=(t // tm,),
        in_specs=[pl.BlockSpec((tm, d), lambda i: (i, 0)),
                  pl.BlockSpec((1, d), lambda i: (0, 0)),
                  pl.BlockSpec((1, N_MOD, d), lambda i: (nb, 0, 0)),
                  pl.BlockSpec((None, d, d_in), lambda i: (layer, 0, 0))],
        out_specs=pl.BlockSpec((tm, n), lambda i: (i, 0)),
        out_shape=jax.ShapeDtypeStruct((t, n), BF16),
        compiler_params=_params("parallel"),
        name="kvproj_ctx",
    )(h, g, mods, w_in)


CONV_PAD = 16
CONV_CHUNK = 512
CONV_SPAN = 512


N_CONVFOURIER_IN = 13


def _convfourier_kernel(*refs, rider):
    (u_ref, dww_ref, dwb_ref, lng_ref, lnb_ref, pww_ref, pwb_ref,
     c_ref, s_ref, perm_ref, altc_ref, altr_ref, z_ref) = refs[:N_CONVFOURIER_IN]
    n_in = N_CONVFOURIER_IN + (rider.n_in if rider else 0)
    n_out = 2 + (rider.n_out if rider else 0)
    o_ref, y_ref = refs[n_in:n_in + 2]
    vs_ref, ze_ref, yr_ref = refs[n_in + n_out:n_in + n_out + 3]
    seq = u_ref.shape[0]
    half = seq // 2
    fb = FLIP_BLOCK
    nblk = half // fb
    first = CONV_PAD - CONV_WIDTH // 2
    rows = min(CONV_CHUNK, seq)
    span = min(CONV_SPAN, seq)
    n_pad = span + 2 * CONV_PAD
    n_chunks = seq // rows
    assert (first + CONV_WIDTH - 1) // SUBLANES * SUBLANES + span <= n_pad - SUBLANES

    def fill_span(lo):
        a, b = max(lo - CONV_PAD, 0), min(lo + span + CONV_PAD, seq)
        zeros = jnp.zeros((CONV_PAD, D_CONV), F32)
        if lo == 0:
            vs_ref[0, 0:CONV_PAD, :] = zeros
        if lo + span == seq:
            vs_ref[0, span + CONV_PAD:n_pad, :] = zeros
        dst = a - (lo - CONV_PAD)
        vs_ref[0, dst:dst + b - a, :] = u_ref[a:b, :D_CONV] * _sigmoid(u_ref[a:b, D_CONV:])
        for s in range(1, SUBLANES):
            vs_ref[s, 0:n_pad - SUBLANES, :] = vs_ref[0, s:s + n_pad - SUBLANES, :]

    def conv_chunk(lo, base):
        acc = jnp.zeros((rows, D_CONV), F32) + dwb_ref[...]
        for t in range(CONV_WIDTH):
            s, a = (first + t) % SUBLANES, (first + t) // SUBLANES
            acc = acc + vs_ref[s, base + a * SUBLANES:base + a * SUBLANES + rows, :] * dww_ref[t:t + 1, :]
        mu = jnp.mean(acc, axis=-1, keepdims=True)
        cen = acc - mu
        var = jnp.mean(cen * cen, axis=-1, keepdims=True)
        y = cen * lax.rsqrt(var + LN_EPS) * lng_ref[...] + lnb_ref[...]
        y = y * _sigmoid(y)
        o_ref[lo + base:lo + base + rows, :] = (_bdot(y.astype(BF16), pww_ref[...]) + pwb_ref[...]).astype(BF16)

    def reversed_block(ref, i, n):
        if i == 0:
            return _bdot(perm_ref[:, :fb], ref[n - fb:n, :])
        lo = n - (i + 1) * fb
        return _bdot(perm_ref[...], ref[lo:lo + 2 * fb, :])

    def fold_in():
        for i in range(nblk):
            rows_i = slice(i * fb, (i + 1) * fb)
            zr = reversed_block(z_ref, i, seq)
            ze_ref[rows_i, :D_FOURIER] = (z_ref[rows_i, :D_FOURIER].astype(F32) + zr[:, :D_FOURIER]).astype(BF16)
            ze_ref[rows_i, D_FOURIER:] = (z_ref[rows_i, D_FOURIER:].astype(F32) - zr[:, D_FOURIER:]).astype(BF16)

    def dft(part, n_parts):
        r = slice(part * half // n_parts, (part + 1) * half // n_parts)
        nyq = z_ref[half:half + 1, :D_FOURIER].astype(F32)
        yc = _bdot(c_ref[r, :], ze_ref[:, :D_FOURIER]) + altc_ref[r, :] * nyq
        ys = _bdot(s_ref[r, :], ze_ref[:, D_FOURIER:])
        y_ref[r, :] = (yc - ys).astype(BF16)
        yr_ref[r, :] = (yc + ys).astype(BF16)

    def fold_out():
        y_mid = _bdot(altr_ref[...], z_ref[:, :D_FOURIER])[0:1, :]
        first_row = lax.broadcasted_iota(jnp.int32, (fb, 1), 0) == 0
        for i in range(nblk):
            blk = reversed_block(yr_ref, i, half)
            if i == 0:
                blk = jnp.where(first_row, y_mid, blk)
            y_ref[half + i * fb:half + (i + 1) * fb, :] = blk.astype(BF16)

    phases = [[fold_in], [functools.partial(dft, 0, 2)], [functools.partial(dft, 1, 2)], [fold_out]]
    if rider:
        extra = rider.phases(refs[N_CONVFOURIER_IN:n_in], refs[n_in + 2:n_in + n_out], refs[n_in + n_out + 3:])
        assert len(extra) == len(phases)
        phases = [own + more for own, more in zip(phases, extra)]
    for ci in range(n_chunks):
        lo = ci * rows // span * span
        if ci * rows == lo:
            fill_span(lo)
        for group in phases[ci * len(phases) // n_chunks:(ci + 1) * len(phases) // n_chunks]:
            for phase in group:
                phase()
        conv_chunk(lo, ci * rows - lo)


class Rider(NamedTuple):
    phases: Callable
    inputs: tuple
    in_specs: tuple
    out_shapes: tuple
    out_specs: tuple
    scratch: tuple

    @property
    def n_in(self):
        return len(self.inputs)

    @property
    def n_out(self):
        return len(self.out_shapes)


def _convfourier(uc, conv_p, mats, zf, *, seq, rider=None):
    t = uc.shape[0]
    cmat, smat, perm, alt_col, alt_row = mats
    half = seq // 2
    vec = pl.BlockSpec((1, D_CONV), lambda b: (0, 0))
    const = lambda a: pl.BlockSpec(a.shape, lambda b: (0, 0), pipeline_mode=pl.Buffered(1))
    in_specs = [pl.BlockSpec((seq, 2 * D_CONV), lambda b: (b, 0)),
                pl.BlockSpec((CONV_WIDTH, D_CONV), lambda b: (0, 0)),
                vec, vec, vec,
                pl.BlockSpec((D_CONV, D_CONV), lambda b: (0, 0)),
                vec,
                const(cmat), const(smat), const(perm), const(alt_col), const(alt_row),
                pl.BlockSpec((seq, 2 * D_FOURIER), lambda b: (0, b))]
    assert len(in_specs) == N_CONVFOURIER_IN
    out_specs = [pl.BlockSpec((seq, D_CONV), lambda b: (b, 0)),
                 pl.BlockSpec((seq, D_FOURIER), lambda b: (0, b))]
    out_shape = [jax.ShapeDtypeStruct((t, D_CONV), BF16),
                 jax.ShapeDtypeStruct((seq, t // seq * D_FOURIER), BF16)]
    scratch = [pltpu.VMEM((SUBLANES, min(CONV_SPAN, seq) + 2 * CONV_PAD, D_CONV), F32),
               pltpu.VMEM((half, 2 * D_FOURIER), BF16), pltpu.VMEM((half, D_FOURIER), BF16)]
    operands = [uc, *conv_p, cmat, smat, perm, alt_col, alt_row, zf]
    if rider:
        in_specs += rider.in_specs
        out_specs += rider.out_specs
        out_shape += rider.out_shapes
        scratch += rider.scratch
        operands += rider.inputs
    return pl.pallas_call(
        functools.partial(_convfourier_kernel, rider=rider),
        grid=(t // seq,),
        in_specs=in_specs,
        out_specs=out_specs,
        out_shape=out_shape,
        scratch_shapes=scratch,
        compiler_params=_params("parallel"),
        name="conv_fourier",
    )(*operands)


def _inproj_rider(h_ctx, g1, mods, w_in, ab, *, layer, nb, n_ctx):
    d = h_ctx.shape[1]
    d_in = w_in.shape[2]
    t = h_ctx.shape[0]

    def phases(in_refs, out_refs, scratch_refs):
        return [[], [functools.partial(_inproj_kernel, *in_refs, *out_refs)], [], []]

    return Rider(
        phases=phases,
        inputs=(h_ctx, g1, mods, w_in, ab),
        in_specs=(pl.BlockSpec((n_ctx, d), lambda b: (b, 0)),
                  pl.BlockSpec((1, d), lambda b: (0, 0)),
                  pl.BlockSpec((1, N_MOD, d), lambda b: (nb, 0, 0)),
                  pl.BlockSpec((None, d, d_in), lambda b: (layer, 0, 0), pipeline_mode=pl.Buffered(1)),
                  pl.BlockSpec((D_FOURIER, 2 * D_FOURIER), lambda b: (0, 0))),
        out_shapes=(jax.ShapeDtypeStruct((n_ctx, nb * 2 * D_FOURIER), BF16),
                    jax.ShapeDtypeStruct((t, 2 * D_CONV), F32),
                    jax.ShapeDtypeStruct((t, 4 * D_NA), BF16)),
        out_specs=(pl.BlockSpec((n_ctx, 2 * D_FOURIER), lambda b: (0, b)),
                   pl.BlockSpec((n_ctx, 2 * D_CONV), lambda b: (b, 0)),
                   pl.BlockSpec((n_ctx, 4 * D_NA), lambda b: (b, 0))),
        scratch=())


def _ctx_tail_rider(h_ctx, yf_c, cv_c, at_c, mods_prev, mods_cur, g2_prev, g1_cur, wo, w1, w2, w_in, *,
                    mlp_layer, in_layer, nb, n_ctx):
    d = h_ctx.shape[1]
    d_ff = w1.shape[2]
    d_in = w_in.shape[2]
    t = h_ctx.shape[0]

    def phases(in_refs, out_refs, scratch_refs):
        (h_ref, yf_ref, cv_ref, at_ref, modp_ref, modc_ref, g2_ref, g1_ref, wo_ref, w1_ref, w2_ref, w_ref) = in_refs
        (kv_ref,) = out_refs
        h1_ref, hn_ref, a_ref = scratch_refs

        def mix():
            m = (_bdot(yf_ref[...], wo_ref[:D_FOURIER, :])
                 + _bdot(cv_ref[...], wo_ref[D_FOURIER:D_FOURIER + D_CONV, :])
                 + _bdot(at_ref[...], wo_ref[D_FOURIER + D_CONV:, :]))
            h1 = h_ref[...] + modp_ref[0, 2:3, :] * m
            h1_ref[...] = h1
            hn_ref[...] = _rms_mod(h1, g2_ref[...], modp_ref[0, 3:4, :], modp_ref[0, 4:5, :]).astype(BF16)

        def up(lo, hi):
            a = jnp.maximum(_bdot(hn_ref[...], w1_ref[:, lo:hi]), 0.0)
            a_ref[:, lo:hi] = (a * a).astype(BF16)

        def down():
            h2 = h1_ref[...] + modp_ref[0, 5:6, :] * _bdot(a_ref[...], w2_ref[...])
            hn = _rms_mod(h2, g1_ref[...], modc_ref[0, 0:1, :], modc_ref[0, 1:2, :]).astype(BF16)
            _store_kv(hn, w_ref, kv_ref, slice(None), 0)

        return [[mix], [functools.partial(up, 0, d_ff // 2)], [functools.partial(up, d_ff // 2, d_ff)], [down]]

    row = lambda b: (b, 0)
    vec = pl.BlockSpec((1, d), lambda b: (0, 0))
    mod = pl.BlockSpec((1, N_MOD, d), lambda b: (nb, 0, 0))
    once = dict(pipeline_mode=pl.Buffered(1))
    return Rider(
        phases=phases,
        inputs=(h_ctx, yf_c, cv_c, at_c, mods_prev, mods_cur, g2_prev, g1_cur, wo, w1, w2, w_in),
        in_specs=(pl.BlockSpec((n_ctx, d), row),
                  pl.BlockSpec((n_ctx, D_FOURIER), lambda b: (0, b)),
                  pl.BlockSpec((n_ctx, D_CONV), row),
                  pl.BlockSpec((n_ctx, D_NA), row),
                  mod, mod, vec, vec,
                  pl.BlockSpec((None, D_MIX, d), lambda b: (mlp_layer, 0, 0), **once),
                  pl.BlockSpec((None, d, d_ff), lambda b: (mlp_layer, 0, 0), **once),
                  pl.BlockSpec((None, d_ff, d), lambda b: (mlp_layer, 0, 0), **once),
                  pl.BlockSpec((None, d, d_in), lambda b: (in_layer, 0, 0), **once)),
        out_shapes=(jax.ShapeDtypeStruct((t, 3 * D_NA), BF16),),
        out_specs=(pl.BlockSpec((n_ctx, 3 * D_NA), row),),
        scratch=(pltpu.VMEM((n_ctx, d), F32), pltpu.VMEM((n_ctx, d), BF16), pltpu.VMEM((n_ctx, d_ff), BF16)))


def _head_mask():
    return lax.broadcasted_iota(jnp.int32, (1, LANES), 1) < NA_HEAD_DIM


Q_ROWS = 2
BAND_ROWS = 10
assert BAND_ROWS >= WIN_ROWS + Q_ROWS - 1 and (BAND_ROWS * GRID_W) % LANES == 0


def _na_patterns(rows):
    starts, sigs = [], []
    for g in range(rows // Q_ROWS):
        start = int(np.clip(Q_ROWS * g - WIN_ROWS // 2, 0, rows - BAND_ROWS))
        r = Q_ROWS * g + np.arange(Q_ROWS)
        rs = np.clip(r - WIN_ROWS // 2, 0, rows - WIN_ROWS)
        starts.append(start)
        sigs.append((start - Q_ROWS * g,) + tuple(rs - r))
    run_starts = [g for g in range(len(sigs)) if g == 0 or sigs[g] != sigs[g - 1]]
    assert len(set(sigs)) == len(run_starts)
    return starts, run_starts


def _na_kernel(q_ref, k_ref, va_ref, vb_ref, kc_ref, vca_ref, vcb_ref, rpb_ref, o_ref, s_ref, p_ref, bias_ref, *,
               seq, n_ctx):
    rows = seq // GRID_W
    per_batch = rows // Q_ROWS
    n_groups = q_ref.shape[0] // seq * per_batch
    n_q = Q_ROWS * GRID_W
    n_loc = BAND_ROWS * GRID_W
    _, run_starts = _na_patterns(rows)
    first = _head_mask()
    dn = (((1,), (1,)), ((), ()))
    assert n_groups >= 4 and seq == per_batch * n_q

    def split(t):
        t = jnp.asarray(t, jnp.int32)
        return t // per_batch, t % per_batch

    def band(t):
        b, g = split(t)
        start = jnp.clip(Q_ROWS * g - WIN_ROWS // 2, 0, rows - BAND_ROWS)
        return pl.multiple_of(b * seq + start * GRID_W, LANES)

    def q_start(t):
        return pl.multiple_of(jnp.asarray(t, jnp.int32) * n_q, n_q)

    def ctx_start(t):
        return pl.multiple_of(split(t)[0] * n_ctx, n_ctx)

    def scores(t, slot):
        g = split(t)[1]
        pat = sum((g >= s).astype(jnp.int32) for s in run_starts[1:])
        q = q_ref[pl.ds(q_start(t), n_q), :]
        kb = k_ref[pl.ds(band(t), n_loc), :]
        kc = kc_ref[pl.ds(ctx_start(t), n_ctx), :]
        for a in range(2):
            qa = jnp.where(first if a == 0 else ~first, q, jnp.zeros_like(q))
            s_ref[slot, a, :, :n_loc] = lax.dot_general(qa, kb, dn, preferred_element_type=F32) + bias_ref[a, pat]
            s_ref[slot, a, :, n_loc:] = lax.dot_general(qa, kc, dn, preferred_element_type=F32)

    def softmax(slot):
        for a in range(2):
            s = s_ref[slot, a]
            p_ref[slot, a] = jnp.exp((s - jnp.max(s, axis=-1, keepdims=True)).astype(BF16))

    def values(t, slot):
        outs = []
        for a, (v_ref, vc_ref) in enumerate(((va_ref, vca_ref), (vb_ref, vcb_ref))):
            o = (_bdot(p_ref[slot, a, :, :n_loc], v_ref[pl.ds(band(t), n_loc), :])
                 + _bdot(p_ref[slot, a, :, n_loc:], vc_ref[pl.ds(ctx_start(t), n_ctx), :]))
            outs.append(o * (1.0 / pltpu.roll(o, NA_HEAD_DIM, axis=1)))
        o_ref[pl.ds(q_start(t), n_q), :] = jnp.where(first, outs[0], outs[1]).astype(BF16)

    def tick(t, parity):
        values(t - 1, 1 - parity)
        scores(t + 1, 1 - parity)
        softmax(parity)

    def quad(j, carry):
        t = 4 * j + 1
        tick(t, 1)
        tick(t + 1, 0)
        tick(t + 2, 1)
        tick(t + 3, 0)
        return carry

    @pl.when(pl.program_id(1) == 0)
    def _():
        _na_assemble_bias(rpb_ref, bias_ref, rows)

    scores(0, 0)
    scores(1, 1)
    softmax(0)
    n_quads = (n_groups - 2) // 4
    lax.fori_loop(0, n_quads, quad, 0)
    for t in range(4 * n_quads + 1, n_groups - 1):
        tick(t, t % 2)
    softmax((n_groups - 1) % 2)
    values(n_groups - 2, n_groups % 2)
    values(n_groups - 1, (n_groups - 1) % 2)


def _na_bias(rpb):
    n_heads, n_dr, n_dc = rpb.shape
    r = rpb.astype(F32)
    w = jnp.concatenate([r[..., WIN_COLS - 1:], jnp.zeros((n_heads, n_dr, LANES - n_dc), F32), r[..., :WIN_COLS - 1]],
                        axis=-1)
    w = jnp.pad(w, ((0, 0), (0, 1), (0, 0)))
    return w.reshape(HEAD_PAIRS, 2, n_dr + 1, LANES)


def _na_assemble_bias(w_ref, bias_ref, rows):
    starts, run_starts = _na_patterns(rows)
    cq = lax.broadcasted_iota(jnp.int32, (GRID_W, LANES), 0)
    lane = lax.broadcasted_iota(jnp.int32, (GRID_W, LANES), 1)
    low = lane < GRID_W
    ck = jnp.where(low, lane, lane - GRID_W)
    col_start = jnp.clip(cq - WIN_COLS // 2, 0, GRID_W - WIN_COLS)
    in_window = (ck >= col_start) & (ck < col_start + WIN_COLS)
    masked = jnp.full((GRID_W, LANES), -jnp.inf, F32)
    rolled = {}

    def half(a, d, upper):
        if (a, d, upper) not in rolled:
            row = jnp.broadcast_to(w_ref[0, a, d:d + 1, :], (GRID_W, LANES))
            rolled[a, d, upper] = pltpu.roll(row, GRID_W if upper else 0, axis=1, stride=1, stride_axis=0)
        return rolled[a, d, upper]

    for a in range(2):
        for p, g in enumerate(run_starts):
            for qi in range(Q_ROWS):
                r = Q_ROWS * g + qi
                rs = int(np.clip(r - WIN_ROWS // 2, 0, rows - WIN_ROWS))
                below = rs - starts[g]
                dr0 = rs - r + WIN_ROWS - 1
                for t in range(BAND_ROWS // 2):
                    d0 = dr0 + 2 * t - below
                    v0 = below <= 2 * t < below + WIN_ROWS
                    v1 = below <= 2 * t + 1 < below + WIN_ROWS
                    if v0 and v1:
                        tile = jnp.where(low, half(a, d0, False), half(a, d0 + 1, True))
                    elif v0:
                        tile = jnp.where(low, half(a, d0, False), masked)
                    elif v1:
                        tile = jnp.where(low, masked, half(a, d0 + 1, True))
                    else:
                        tile = masked
                    if v0 or v1:
                        tile = jnp.where(in_window, tile, masked)
                    bias_ref[a, p, qi * GRID_W:(qi + 1) * GRID_W, t * LANES:(t + 1) * LANES] = tile


NA_BATCHES = 4


def _na(qkv, kv_ctx, rpb_rows, *, seq, n_ctx, kc_off):
    t = qkv.shape[0]
    hp = HEAD_PAIRS
    tb, tbc = NA_BATCHES * seq, NA_BATCHES * n_ctx
    n_pat = len(_na_patterns(seq // GRID_W)[1])
    n_keys = BAND_ROWS * GRID_W + n_ctx
    assert t % tb == 0
    return pl.pallas_call(
        functools.partial(_na_kernel, seq=seq, n_ctx=n_ctx),
        grid=(hp, t // tb),
        in_specs=[pl.BlockSpec((tb, LANES), lambda h, b: (b, h)),
                  pl.BlockSpec((tb, LANES), lambda h, b: (b, hp + h)),
                  pl.BlockSpec((tb, LANES), lambda h, b: (b, 2 * hp + h)),
                  pl.BlockSpec((tb, LANES), lambda h, b: (b, 3 * hp + h)),
                  pl.BlockSpec((tbc, LANES), lambda h, b: (b, kc_off + h)),
                  pl.BlockSpec((tbc, LANES), lambda h, b: (b, kc_off + hp + h)),
                  pl.BlockSpec((tbc, LANES), lambda h, b: (b, kc_off + 2 * hp + h)),
                  pl.BlockSpec((1,) + rpb_rows.shape[1:], lambda h, b: (h, 0, 0, 0))],
        out_specs=pl.BlockSpec((tb, LANES), lambda h, b: (b, h)),
        out_shape=jax.ShapeDtypeStruct((t, D_NA), BF16),
        scratch_shapes=[pltpu.VMEM((2, 2, Q_ROWS * GRID_W, n_keys), F32),
                        pltpu.VMEM((2, 2, Q_ROWS * GRID_W, n_keys), BF16),
                        pltpu.VMEM((2, n_pat, Q_ROWS * GRID_W, BAND_ROWS * GRID_W), F32)],
        compiler_params=_params("parallel", "arbitrary"),
        name="na_attention",
    )(qkv, qkv, qkv, qkv, kv_ctx, kv_ctx, kv_ctx, rpb_rows)


def _ctx_attn_kernel(q_ref, k_ref, va_ref, vb_ref, o_ref):
    first = _head_mask()
    for h in range(HEAD_PAIRS):
        lanes = slice(h * LANES, (h + 1) * LANES)
        q = q_ref[:, lanes]
        k = k_ref[:, lanes]
        outs = []
        for a, v_ref in enumerate((va_ref, vb_ref)):
            qa = jnp.where(first if a == 0 else ~first, q, jnp.zeros_like(q))
            s = lax.dot_general(qa, k, (((1,), (1,)), ((), ())), preferred_element_type=F32)
            p = jnp.exp(s - jnp.max(s, axis=-1, keepdims=True))
            den = jnp.sum(p, axis=-1, keepdims=True)
            outs.append(_bdot(p.astype(BF16), v_ref[:, lanes]) * (1.0 / den))
        o_ref[:, lanes] = jnp.where(first, outs[0], outs[1]).astype(BF16)


def _ctx_attn(qkv, *, n_ctx):
    t = qkv.shape[0]
    return pl.pallas_call(
        _ctx_attn_kernel,
        grid=(t // n_ctx,),
        in_specs=[pl.BlockSpec((n_ctx, D_NA), lambda b: (b, 0)),
                  pl.BlockSpec((n_ctx, D_NA), lambda b: (b, 1)),
                  pl.BlockSpec((n_ctx, D_NA), lambda b: (b, 2)),
                  pl.BlockSpec((n_ctx, D_NA), lambda b: (b, 3))],
        out_specs=pl.BlockSpec((n_ctx, D_NA), lambda b: (b, 0)),
        out_shape=jax.ShapeDtypeStruct((t, D_NA), BF16),
        compiler_params=_params("parallel"),
        name="ctx_attention",
    )(qkv, qkv, qkv, qkv)


N_OUTMLP_IN = 10


def _outmlp_kernel(*refs, final_norm, n_cast):
    h_ref, yf_ref, cv_ref, at_ref, mod_ref, g_ref, gf_ref, wo_ref, w1_ref, w2_ref = refs[:N_OUTMLP_IN]
    o_ref = refs[N_OUTMLP_IN + n_cast]
    for src, dst in zip(refs[N_OUTMLP_IN:N_OUTMLP_IN + n_cast], refs[N_OUTMLP_IN + n_cast + 1:]):
        dst[...] = src[...].astype(BF16)
    mix = (_bdot(yf_ref[...], wo_ref[:D_FOURIER, :])
           + _bdot(cv_ref[...], wo_ref[D_FOURIER:D_FOURIER + D_CONV, :])
           + _bdot(at_ref[...], wo_ref[D_FOURIER + D_CONV:, :]))
    h1 = h_ref[...] + mod_ref[0, 2:3, :] * mix
    hn = _rms_mod(h1, g_ref[...], mod_ref[0, 3:4, :], mod_ref[0, 4:5, :]).astype(BF16)
    a = jnp.maximum(_bdot(hn, w1_ref[...]), 0.0)
    out = h1 + mod_ref[0, 5:6, :] * _bdot((a * a).astype(BF16), w2_ref[...])
    if final_norm:
        out = out * lax.rsqrt(jnp.mean(out * out, axis=-1, keepdims=True) + RMS_EPS) * gf_ref[...]
    o_ref[...] = out


def _outmlp(h, yf, cv, at, mods, g2, gf, wo, w1, w2, *, layer, seq, tm, ctx, final_norm, cast=(), cast_layer=0):
    t, d = h.shape
    nb = t // seq
    per = seq // tm
    row = (lambda i: nb) if ctx else (lambda i: i // per)
    d_ff = w1.shape[2]
    vec = pl.BlockSpec((1, d), lambda i: (0, 0))
    once = dict(pipeline_mode=pl.Buffered(1))
    steps = t // tm
    cast_in = [pl.BlockSpec((None, w.shape[1] // steps, w.shape[2]), lambda i: (cast_layer, i, 0)) for w in cast]
    cast_out = [pl.BlockSpec((None, w.shape[1] // steps, w.shape[2]), lambda i: (0, i, 0)) for w in cast]
    cast_shape = [jax.ShapeDtypeStruct((1,) + w.shape[1:], BF16) for w in cast]
    assert all(w.shape[1] % (steps * 16) == 0 for w in cast)
    outs = pl.pallas_call(
        functools.partial(_outmlp_kernel, final_norm=final_norm, n_cast=len(cast)),
        grid=(steps,),
        in_specs=[pl.BlockSpec((tm, d), lambda i: (i, 0)),
                  pl.BlockSpec((tm, D_FOURIER), lambda i: (i % per, i // per)),
                  pl.BlockSpec((tm, D_CONV), lambda i: (i, 0)),
                  pl.BlockSpec((tm, D_NA), lambda i: (i, 0)),
                  pl.BlockSpec((1, N_MOD, d), lambda i: (row(i), 0, 0)),
                  vec, vec,
                  pl.BlockSpec((None, D_MIX, d), lambda i: (layer, 0, 0), **once),
                  pl.BlockSpec((None, d, d_ff), lambda i: (layer, 0, 0), **once),
                  pl.BlockSpec((None, d_ff, d), lambda i: (layer, 0, 0), **once)] + cast_in,
        out_specs=[pl.BlockSpec((tm, d), lambda i: (i, 0))] + cast_out,
        out_shape=[jax.ShapeDtypeStruct((t, d), F32)] + cast_shape,
        compiler_params=_params("parallel"),
        name="outmlp_ctx" if ctx else "outmlp_lat",
    )(h, yf, cv, at, mods, g2, gf, wo, w1, w2, *cast)
    return outs if cast else outs[0]


def kernel(x, c, ctx, c_ctx, ada_w, ada_b, norm1_g, norm2_g, w_in, w_fourier, conv_dw_w, conv_dw_b, conv_norm_g,
           conv_norm_b, conv_pw_w, conv_pw_b, na_rpb, w_out, mlp_w1, mlp_w2, final_norm_g):
    nb, seq, d = x.shape
    n_ctx = ctx.shape[1]
    depth = ada_w.shape[0]
    assert nb < MOD_ROWS and seq % GRID_W == 0 and d == D_MODEL

    cc = jnp.concatenate([c, c_ctx[None], jnp.zeros((MOD_ROWS - nb - 1, d), F32)], axis=0)
    mods = _adaln(cc, ada_w, ada_b).reshape(depth, MOD_ROWS, N_MOD, d)
    ab_lat = _fold_fourier(w_fourier, seq)
    ab_ctx = _fold_fourier(w_fourier, n_ctx)
    dft_lat = _dft_mats(seq)
    dft_ctx = _dft_mats(n_ctx)

    stacked = (w_in, w_out, mlp_w1, mlp_w2)
    weights = [(w_in[:1].astype(BF16),)]
    pww_b = conv_pw_w.astype(BF16)
    gf = final_norm_g.reshape(1, d)

    h_lat = x.reshape(nb * seq, d)
    h_ctx = ctx.reshape(nb * n_ctx, d)
    ctx_mix = None
    for i in range(depth):
        last = i == depth - 1
        g1 = norm1_g[i].reshape(1, d)
        g2 = norm2_g[i].reshape(1, d)
        conv_p = (conv_dw_w[i], conv_dw_b[i].reshape(1, -1), conv_norm_g[i].reshape(1, -1),
                  conv_norm_b[i].reshape(1, -1), pww_b[i], conv_pw_b[i].reshape(1, -1))
        bias = _na_bias(na_rpb[i])

        if i == 0:
            zf, uc, qkv, *cast = _inproj(h_lat, g1, mods[i], weights[0][0], ab_lat[i], layer=0, seq=seq, tm=TM_INPROJ,
                                         ctx=False, cast=stacked[1:], cast_layer=0)
            weights[0] += tuple(cast)
        else:
            zf, uc, qkv = _inproj(h_lat, g1, mods[i], weights[i][0], ab_lat[i], layer=0, seq=seq, tm=TM_INPROJ,
                                  ctx=False)
        w_in_b, w_out_b, w1_b, w2_b = weights[i]
        if last and ctx_mix is not None:
            rider = _ctx_tail_rider(h_ctx, *ctx_mix, mods[i - 1], mods[i], norm2_g[i - 1].reshape(1, d), g1,
                                    *weights[i - 1][1:], w_in_b, mlp_layer=0, in_layer=0, nb=nb, n_ctx=n_ctx)
            cv, yf, kv_ctx = _convfourier(uc, conv_p, dft_lat, zf, seq=seq, rider=rider)
            kc_off = 0
        elif last:
            kv_ctx = _kvproj(h_ctx, g1, mods[i], w_in_b, layer=0, nb=nb, tm=n_ctx)
            cv, yf = _convfourier(uc, conv_p, dft_lat, zf, seq=seq)
            kc_off = 0
        else:
            if ctx_mix is not None:
                h_ctx = _outmlp(h_ctx, *ctx_mix, mods[i - 1], norm2_g[i - 1].reshape(1, d), gf, *weights[i - 1][1:],
                                layer=0, seq=n_ctx, tm=n_ctx, ctx=True, final_norm=False)
            rider = _inproj_rider(h_ctx, g1, mods[i], w_in_b, ab_ctx[i], layer=0, nb=nb, n_ctx=n_ctx)
            cv, yf, zf_c, uc_c, qkv_c = _convfourier(uc, conv_p, dft_lat, zf, seq=seq, rider=rider)
            kv_ctx = qkv_c
            kc_off = HEAD_PAIRS

        at = _na(qkv, kv_ctx, bias, seq=seq, n_ctx=n_ctx, kc_off=kc_off)
        if last:
            h_lat = _outmlp(h_lat, yf, cv, at, mods[i], g2, gf, w_out_b, w1_b, w2_b,
                            layer=0, seq=seq, tm=TM_MLP, ctx=False, final_norm=True)
        else:
            h_lat, *cast = _outmlp(h_lat, yf, cv, at, mods[i], g2, gf, w_out_b, w1_b, w2_b, layer=0, seq=seq,
                                   tm=TM_MLP, ctx=False, final_norm=False, cast=stacked, cast_layer=i + 1)
            weights.append(tuple(cast))

        if not last:
            cv_c, yf_c = _convfourier(uc_c, conv_p, dft_ctx, zf_c, seq=n_ctx)
            at_c = _ctx_attn(qkv_c, n_ctx=n_ctx)
            ctx_mix = (yf_c, cv_c, at_c)
    return h_lat.reshape(nb, seq, d)
```
